```python
import math
import jax, jax.numpy as jnp
from jax import lax
import numpy as np

D_MODEL = 1024
BATCH = 8
SEQ = 2048
DEPTH = 1
DEC_BATCH = 128
DEC_SEQ = 1
PAST_LEN = 16384
PAGE_SIZE = 128

GDN_HEADS = 8
GDN_DK = 128
GDN_DV = 128
QK_DIM = GDN_HEADS * GDN_DK
V_DIM = GDN_HEADS * GDN_DV
QKV_CH = 2 * QK_DIM + V_DIM
SHORT_CONV = 4
CHUNK = 64
CONF_CH = D_MODEL // 2
CONF_K = 31
N_EXPERTS = 32
TOP_K = 4
D_FF = D_MODEL
SWIGLU_LIMIT = 7.0
SWIGLU_ALPHA = 1.702
PLE_DIM = 256
NORM_EPS = 1e-6
LN_EPS = 1e-5

OFF_Q = 0
OFF_K = OFF_Q + QK_DIM
OFF_V = OFF_K + QK_DIM
OFF_Z = OFF_V + V_DIM
OFF_BETA = OFF_Z + V_DIM
OFF_A = OFF_BETA + GDN_HEADS
OFF_GLU = OFF_A + GDN_HEADS
OFF_GATE = OFF_GLU + 2 * CONF_CH
IN_COLS = OFF_GATE + 2 * D_MODEL

kernel_name = "hybrid_gdn_conformer_moe_step"


def _rmsnorm(x, g):
    xf = x.astype(jnp.float32)
    y = xf * lax.rsqrt(jnp.mean(xf * xf, axis=-1, keepdims=True) + NORM_EPS)
    return (y * g.astype(jnp.float32)).astype(x.dtype)


def _layernorm(x, g, b):
    xf = x.astype(jnp.float32)
    mu = jnp.mean(xf, axis=-1, keepdims=True)
    var = jnp.mean(jnp.square(xf - mu), axis=-1, keepdims=True)
    y = (xf - mu) * lax.rsqrt(var + LN_EPS)
    return (y * g.astype(jnp.float32) + b.astype(jnp.float32)).astype(x.dtype)


def _l2norm(x):
    return x * lax.rsqrt(jnp.sum(x * x, axis=-1, keepdims=True) + NORM_EPS)


def _causal_dwconv(xpad, w):
    c = xpad.shape[-1]
    return lax.conv_general_dilated(xpad, w[:, None, :].astype(xpad.dtype), window_strides=(1,), padding="VALID",
                                    dimension_numbers=("NWC", "WIO", "NWC"), feature_group_count=c)


def _gated_delta_chunked(q, k, v, beta, g, s0):
    b, l, h, _ = q.shape
    c = min(CHUNK, l)
    nc = -(-l // c)
    pad = nc * c - l
    def prep(t):
        t = jnp.pad(t, ((0, 0), (0, pad)) + ((0, 0),) * (t.ndim - 2))
        t = jnp.moveaxis(t, 2, 1)
        return t.reshape((b, h, nc, c) + t.shape[3:])
    qc, kc, vc, bc, gcn = prep(q), prep(k), prep(v), prep(beta), prep(g)
    gc = jnp.cumsum(gcn, axis=-1)
    idx = jnp.arange(c)
    tril = idx[:, None] >= idx[None, :]
    strict = idx[:, None] > idx[None, :]
    decay = jnp.exp(jnp.where(tril, gc[..., :, None] - gc[..., None, :], -jnp.inf))
    kb = kc * bc[..., None]
    a_mat = jnp.where(strict, jnp.einsum("bhncd,bhnsd->bhncs", kb, kc) * decay, 0.0)
    eye = jnp.eye(c, dtype=jnp.float32)
    t_inv = lax.linalg.triangular_solve(eye + a_mat, jnp.broadcast_to(eye, a_mat.shape), left_side=True, lower=True)
    u = jnp.einsum("bhncs,bhnse->bhnce", t_inv, vc * bc[..., None])
    w = jnp.einsum("bhncs,bhnsd->bhncd", t_inv, kb * jnp.exp(gc)[..., None])
    qk = jnp.einsum("bhncd,bhnsd->bhncs", qc, kc) * decay
    qg = qc * jnp.exp(gc)[..., None]
    kd = kc * jnp.exp(gc[..., -1:] - gc)[..., None]
    dl = jnp.exp(gc[..., -1])

    def step(s, inp):
        qg_c, kd_c, u_c, w_c, qk_c, dl_c = inp
        v_new = u_c - jnp.einsum("bhcd,bhde->bhce", w_c, s)
        o = jnp.einsum("bhcd,bhde->bhce", qg_c, s) + jnp.einsum("bhcs,bhse->bhce", qk_c, v_new)
        s = s * dl_c[..., None, None] + jnp.einsum("bhcd,bhce->bhde", kd_c, v_new)
        return s, o

    xs = tuple(jnp.moveaxis(t, 2, 0) for t in (qg, kd, u, w, qk, dl))
    s_fin, o = lax.scan(step, s0.astype(jnp.float32), xs)
    o = jnp.moveaxis(o, 0, 2).reshape(b, h, nc * c, GDN_DV)
    o = jnp.moveaxis(o, 1, 2)[:, :l]
    return o, s_fin


def _moe(x2d, w_router, b_router, w_gate_up, b_gate_up, w_down, b_down):
    logits = x2d.astype(jnp.float32) @ w_router.astype(jnp.float32) + b_router.astype(jnp.float32)
    top_v, top_i = lax.top_k(logits, TOP_K)
    top_w = jax.nn.softmax(top_v, axis=-1)
    gates = jnp.sum(jax.nn.one_hot(top_i, N_EXPERTS, dtype=jnp.float32) * top_w[..., None], axis=1)
    out = jnp.zeros(x2d.shape, jnp.float32)
    for e in range(N_EXPERTS):
        gu = x2d @ w_gate_up[e] + b_gate_up[e]
        gate = jnp.minimum(gu[:, :D_FF], SWIGLU_LIMIT)
        up = jnp.clip(gu[:, D_FF:], -SWIGLU_LIMIT, SWIGLU_LIMIT)
        hid = (up + 1.0) * (gate * jax.nn.sigmoid(SWIGLU_ALPHA * gate))
        ye = hid @ w_down[e] + b_down[e]
        out = out + gates[:, e:e + 1] * ye.astype(jnp.float32)
    return out.astype(x2d.dtype)


def _layer(x, p_l, s0, qkv_buf, conf_buf, norm_mix, w_in, conv_qkv_w, a_log, dt_bias, gdn_norm, w_gdn_out,
           conf_dw_w, conf_dw_b, conf_ln_g, conf_ln_b, w_conf_out, w_out, norm_ffn, w_router, b_router,
           w_gate_up, b_gate_up, w_down, b_down, norm_ple, w_ple_gate, w_ple_proj):
    b, l, _ = x.shape
    a = _rmsnorm(x, norm_mix)
    proj = jnp.einsum("bld,dc->blc", a, w_in)
    qkv_raw = proj[..., :OFF_Z]
    z = proj[..., OFF_Z:OFF_BETA]
    beta_raw = proj[..., OFF_BETA:OFF_A]
    a_raw = proj[..., OFF_A:OFF_GLU]
    glu_in = proj[..., OFF_GLU:OFF_GATE]
    gate_logits = proj[..., OFF_GATE:]

    qkv_pad = jnp.concatenate([qkv_buf.astype(x.dtype), qkv_raw], axis=1)
    new_qkv_buf = qkv_pad[:, -(SHORT_CONV - 1):]
    qkv = jax.nn.silu(_causal_dwconv(qkv_pad, conv_qkv_w)).astype(jnp.float32)
    q = _l2norm(qkv[..., OFF_Q:OFF_K].reshape(b, l, GDN_HEADS, GDN_DK)) * (GDN_DK ** -0.5)
    k = _l2norm(qkv[..., OFF_K:OFF_V].reshape(b, l, GDN_HEADS, GDN_DK))
    v = qkv[..., OFF_V:OFF_Z].reshape(b, l, GDN_HEADS, GDN_DV)
    beta = jax.nn.sigmoid(beta_raw.astype(jnp.float32))
    g = -jnp.exp(a_log.astype(jnp.float32)) * jax.nn.softplus(a_raw.astype(jnp.float32) + dt_bias.astype(jnp.float32))
    o, s_new = _gated_delta_chunked(q, k, v, beta, g, s0)
    o = o * lax.rsqrt(jnp.mean(o * o, axis=-1, keepdims=True) + NORM_EPS) * gdn_norm.astype(jnp.float32)
    o = o * jax.nn.silu(z.astype(jnp.float32).reshape(b, l, GDN_HEADS, GDN_DV))
    branch_a = jnp.einsum("blc,cd->bld", o.reshape(b, l, V_DIM).astype(x.dtype), w_gdn_out)

    u = glu_in[..., :CONF_CH] * jax.nn.sigmoid(glu_in[..., CONF_CH:])
    u_pad = jnp.concatenate([conf_buf.astype(x.dtype), u], axis=1)
    new_conf_buf = u_pad[:, -(CONF_K - 1):]
    cv = _causal_dwconv(u_pad, conf_dw_w) + conf_dw_b
    cv = jax.nn.silu(_layernorm(cv, conf_ln_g, conf_ln_b))
    branch_b = jnp.einsum("blc,cd->bld", cv, w_conf_out)

    g_a = jax.nn.sigmoid(gate_logits[..., :D_MODEL])
    g_b = jax.nn.sigmoid(gate_logits[..., D_MODEL:])
    x = x + jnp.einsum("bld,de->ble", g_a * branch_a + g_b * branch_b, w_out)

    hm = _rmsnorm(x, norm_ffn).reshape(b * l, D_MODEL)
    x = x + _moe(hm, w_router, b_router, w_gate_up, b_gate_up, w_down, b_down).reshape(b, l, D_MODEL)

    gate = jax.nn.sigmoid(jnp.einsum("bld,de->ble", _rmsnorm(x, norm_ple), w_ple_gate))
    x = x + jnp.einsum("blp,pd->bld", p_l, w_ple_proj) * gate
    return x, s_new, new_qkv_buf, new_conf_buf


def setup_inputs(seed: int = 0) -> dict:
    key = jax.random.key(seed)
    ks = jax.random.split(key, 40)
    f32 = jnp.float32
    def nrm(i, shape, scale):
        return jax.random.normal(ks[i], shape, f32) * scale
    def gain(i, shape):
        return 1.0 + 0.02 * jax.random.normal(ks[i], shape, f32)
    return {
        "x_prompt": nrm(0, (BATCH, SEQ, D_MODEL), 1.0),
        "x_sample": nrm(1, (DEC_BATCH, DEC_SEQ, D_MODEL), 1.0),
        "p_prompt": nrm(2, (DEPTH, BATCH, SEQ, PLE_DIM), 1.0),
        "p_sample": nrm(3, (DEPTH, DEC_BATCH, DEC_SEQ, PLE_DIM), 1.0),
        "state_gdn": nrm(4, (DEPTH, DEC_BATCH, GDN_HEADS, GDN_DK, GDN_DV), GDN_DK ** -0.5),
        "state_qkv_conv": nrm(5, (DEPTH, DEC_BATCH, SHORT_CONV - 1, QKV_CH), 1.0),
        "state_conf_conv": nrm(6, (DEPTH, DEC_BATCH, CONF_K - 1, CONF_CH), 0.5),
        "norm_mix": gain(7, (DEPTH, D_MODEL)),
        "w_in": nrm(8, (DEPTH, D_MODEL, IN_COLS), D_MODEL ** -0.5),
        "conv_qkv_w": nrm(9, (DEPTH, SHORT_CONV, QKV_CH), SHORT_CONV ** -0.5),
        "a_log": jnp.log(jax.random.uniform(ks[10], (DEPTH, GDN_HEADS), f32, 1.0, 16.0)),
        "dt_bias": nrm(11, (DEPTH, GDN_HEADS), 0.1) - 2.0,
        "gdn_norm": gain(12, (DEPTH, GDN_DV)),
        "w_gdn_out": nrm(13, (DEPTH, V_DIM, D_MODEL), V_DIM ** -0.5),
        "conf_dw_w": nrm(14, (DEPTH, CONF_K, CONF_CH), CONF_K ** -0.5),
        "conf_dw_b": nrm(15, (DEPTH, CONF_CH), 0.02),
        "conf_ln_g": gain(16, (DEPTH, CONF_CH)),
        "conf_ln_b": nrm(17, (DEPTH, CONF_CH), 0.02),
        "w_conf_out": nrm(18, (DEPTH, CONF_CH, D_MODEL), CONF_CH ** -0.5),
        "w_out": nrm(19, (DEPTH, D_MODEL, D_MODEL), D_MODEL ** -0.5),
        "norm_ffn": gain(20, (DEPTH, D_MODEL)),
        "w_router": nrm(21, (DEPTH, D_MODEL, N_EXPERTS), D_MODEL ** -0.5),
        "b_router": nrm(22, (DEPTH, N_EXPERTS), 0.01),
        "w_gate_up": nrm(23, (DEPTH, N_EXPERTS, D_MODEL, 2 * D_FF), D_MODEL ** -0.5),
        "b_gate_up": nrm(24, (DEPTH, N_EXPERTS, 2 * D_FF), 0.02),
        "w_down": nrm(25, (DEPTH, N_EXPERTS, D_FF, D_MODEL), D_FF ** -0.5),
        "b_down": nrm(26, (DEPTH, N_EXPERTS, D_MODEL), 0.02),
        "norm_ple": gain(27, (DEPTH, D_MODEL)),
        "w_ple_gate": nrm(28, (DEPTH, D_MODEL, D_MODEL), D_MODEL ** -0.5),
        "w_ple_proj": nrm(29, (DEPTH, PLE_DIM, D_MODEL), PLE_DIM ** -0.5),
        "norm_final": gain(30, (D_MODEL,)),
    }


def reference(x_prompt, x_sample, p_prompt, p_sample, state_gdn, state_qkv_conv, state_conf_conv,
              norm_mix, w_in, conv_qkv_w, a_log, dt_bias, gdn_norm, w_gdn_out, conf_dw_w, conf_dw_b,
              conf_ln_g, conf_ln_b, w_conf_out, w_out, norm_ffn, w_router, b_router, w_gate_up, b_gate_up,
              w_down, b_down, norm_ple, w_ple_gate, w_ple_proj, norm_final):
    hp, hs = x_prompt, x_sample
    bp = x_prompt.shape[0]
    sp_l, qp_l, cp_l, ss_l, qs_l, cs_l = [], [], [], [], [], []
    for i in range(DEPTH):
        lp = (norm_mix[i], w_in[i], conv_qkv_w[i], a_log[i], dt_bias[i], gdn_norm[i], w_gdn_out[i],
              conf_dw_w[i], conf_dw_b[i], conf_ln_g[i], conf_ln_b[i], w_conf_out[i], w_out[i], norm_ffn[i],
              w_router[i], b_router[i], w_gate_up[i], b_gate_up[i], w_down[i], b_down[i], norm_ple[i],
              w_ple_gate[i], w_ple_proj[i])
        s0 = jnp.zeros((bp, GDN_HEADS, GDN_DK, GDN_DV), jnp.float32)
        q0 = jnp.zeros((bp, SHORT_CONV - 1, QKV_CH), x_prompt.dtype)
        c0 = jnp.zeros((bp, CONF_K - 1, CONF_CH), x_prompt.dtype)
        hp, sp, qp, cp = _layer(hp, p_prompt[i], s0, q0, c0, *lp)
        hs, ss, qs, cs = _layer(hs, p_sample[i], state_gdn[i], state_qkv_conv[i], state_conf_conv[i], *lp)
        sp_l.append(sp); qp_l.append(qp); cp_l.append(cp)
        ss_l.append(ss); qs_l.append(qs); cs_l.append(cs)
    y_prompt = _rmsnorm(hp, norm_final)
    y_sample = _rmsnorm(hs, norm_final)
    return (y_prompt, y_sample, jnp.stack(sp_l), jnp.stack(qp_l), jnp.stack(cp_l),
            jnp.stack(ss_l), jnp.stack(qs_l), jnp.stack(cs_l))
```

```python
import functools

import jax
import jax.numpy as jnp
from jax import lax
from jax.experimental import pallas as pl
from jax.experimental.pallas import tpu as pltpu

F32 = jnp.float32
BF16 = jnp.bfloat16

D_MODEL = 1024
GDN_HEADS = 8
GDN_DK = 128
GDN_DV = 128
QK_DIM = GDN_HEADS * GDN_DK
V_DIM = GDN_HEADS * GDN_DV
QKV_CH = 2 * QK_DIM + V_DIM
SHORT_CONV = 4
CHUNK = 64
CONF_CH = D_MODEL // 2
CONF_K = 31
N_EXPERTS = 32
TOP_K = 4
D_FF = D_MODEL
SWIGLU_LIMIT = 7.0
SWIGLU_ALPHA = 1.702
PLE_DIM = 256
NORM_EPS = 1e-6
LN_EPS = 1e-5

OFF_Z = QKV_CH
OFF_BETA = OFF_Z + V_DIM
OFF_GLU = OFF_BETA + 2 * GDN_HEADS
OFF_GATE = OFF_GLU + 2 * CONF_CH

VMEM_LIMIT_BYTES = 52 * 1024 * 1024
SUBLANES = 8
LANES = 128
CONF_HALO = 32


def _dot(a, b):
    return jnp.dot(a, b, preferred_element_type=F32)


def _bdot(a, b):
    return jnp.dot(a.astype(BF16), b.astype(BF16), preferred_element_type=F32)


def _hdot(a, b):
    return jnp.dot(a, b, preferred_element_type=F32, precision=lax.Precision.HIGHEST)


def _sigmoid(x):
    return 1.0 / (1.0 + jnp.exp(-x))


def _silu(x):
    return x * _sigmoid(x)


def _rms(x, g):
    return x * lax.rsqrt(jnp.mean(x * x, axis=-1, keepdims=True) + NORM_EPS) * g


def _const_spec(shape):
    nd = len(shape)
    return pl.BlockSpec(shape, lambda *_: (0,) * nd, pipeline_mode=pl.Buffered(1))


def _params(sem):
    return pltpu.CompilerParams(dimension_semantics=sem, vmem_limit_bytes=VMEM_LIMIT_BYTES)


def _inproj_kernel(x_ref, nw_ref, wqkv_ref, wz_ref, wba_ref, wglu_ref, wgate_ref, alog_ref, dtb_ref,
                   qkv_ref, z_ref, bg_ref, u_ref, ga_ref, gb_ref):
    a = _rms(x_ref[...], nw_ref[...]).astype(BF16)
    for c in range(QKV_CH // D_MODEL):
        cols = slice(c * D_MODEL, (c + 1) * D_MODEL)
        qkv_ref[:, cols] = _dot(a, wqkv_ref[:, cols])
    z_ref[...] = _dot(a, wz_ref[...])
    ba = _dot(a, wba_ref[...])
    lane = lax.broadcasted_iota(jnp.int32, ba.shape, 1)
    t = ba + dtb_ref[...]
    softplus = jnp.maximum(t, 0.0) + jnp.log(1.0 + jnp.exp(-jnp.abs(t)))
    bg_ref[...] = jnp.where(lane < GDN_HEADS, _sigmoid(ba), -jnp.exp(alog_ref[...]) * softplus)
    glu = _dot(a, wglu_ref[...])
    u_ref[...] = glu[:, :CONF_CH] * _sigmoid(glu[:, CONF_CH:])
    ga_ref[...] = _sigmoid(_dot(a, wgate_ref[:, :D_MODEL]))
    gb_ref[...] = _sigmoid(_dot(a, wgate_ref[:, D_MODEL:]))


def _inproj(x, norm_w, wqkv, wz, wba, wglu, wgate, alog16, dtb16, tm):
    n = x.shape[0]
    row = lambda w: pl.BlockSpec((tm, w), lambda i: (i, 0))
    return pl.pallas_call(
        _inproj_kernel,
        grid=(n // tm,),
        in_specs=[row(D_MODEL), _const_spec((1, D_MODEL)), _const_spec(wqkv.shape), _const_spec(wz.shape),
                  _const_spec(wba.shape), _const_spec(wglu.shape), _const_spec(wgate.shape),
                  _const_spec((1, 2 * GDN_HEADS)), _const_spec((1, 2 * GDN_HEADS))],
        out_specs=[row(QKV_CH), row(V_DIM), row(2 * GDN_HEADS), row(CONF_CH), row(D_MODEL), row(D_MODEL)],
        out_shape=[jax.ShapeDtypeStruct((n, w), F32)
                   for w in (QKV_CH, V_DIM, 2 * GDN_HEADS, CONF_CH, D_MODEL, D_MODEL)],
        compiler_params=_params(("parallel",)),
        name="inproj",
    )(x, norm_w, wqkv, wz, wba, wglu, wgate, alog16, dtb16)


def _unit_lower_inverse(a, eye):
    n = a.shape[0]
    b = -a
    t = eye + b
    p = b
    k = 1
    while 2 * k < n:
        p = _hdot(p, p)
        t = t + _hdot(t, p)
        k *= 2
    return t


def _gdn_prompt_kernel(qkv_ref, z_ref, bg_ref, bgt_ref, cw_ref, gn_ref, o_ref, s_out_ref, ext_ref, s_ref):
    c = pl.program_id(1)

    @pl.when(c == 0)
    def _():
        ext_ref[0:SUBLANES, :] = jnp.zeros((SUBLANES, QKV_CH), F32)
        s_ref[...] = jnp.zeros(s_ref.shape, F32)

    ext_ref[SUBLANES:SUBLANES + CHUNK, :] = qkv_ref[...]

    def convsilu(c0):
        cols = slice(c0, c0 + LANES)
        acc = cw_ref[SHORT_CONV - 1:SHORT_CONV, cols] * ext_ref[SUBLANES:SUBLANES + CHUNK, cols]
        for j in range(SHORT_CONV - 1):
            r0 = SUBLANES - (SHORT_CONV - 1) + j
            acc = acc + cw_ref[j:j + 1, cols] * ext_ref[r0:r0 + CHUNK, cols]
        return _silu(acc)

    row = lax.broadcasted_iota(jnp.int32, (CHUNK, CHUNK), 0)
    col = lax.broadcasted_iota(jnp.int32, (CHUNK, CHUNK), 1)
    tril = row >= col
    strict = row > col
    eye = (row == col).astype(F32)
    bg = bg_ref[...]
    gc_all = _hdot(tril.astype(F32), bg)
    gct_all = _hdot(bgt_ref[...], (row <= col).astype(F32))

    for h in range(GDN_HEADS):
        q = convsilu(h * GDN_DK)
        k = convsilu(QK_DIM + h * GDN_DK)
        v = convsilu(2 * QK_DIM + h * GDN_DV)
        q = q * lax.rsqrt(jnp.sum(q * q, axis=-1, keepdims=True) + NORM_EPS) * (GDN_DK ** -0.5)
        k = k * lax.rsqrt(jnp.sum(k * k, axis=-1, keepdims=True) + NORM_EPS)
        beta = bg[:, h:h + 1]
        gc = gc_all[:, GDN_HEADS + h:GDN_HEADS + h + 1]
        gct = gct_all[GDN_HEADS + h:GDN_HEADS + h + 1, :]
        g_last = gc_all[CHUNK - 1:CHUNK, GDN_HEADS + h:GDN_HEADS + h + 1]
        decay = jnp.exp(jnp.where(tril, gc - gct, -jnp.inf))
        kb = k * beta
        k16 = k.astype(BF16)
        kk = lax.dot_general(kb.astype(BF16), k16, (((1,), (1,)), ((), ())), preferred_element_type=F32)
        t_inv = _unit_lower_inverse(jnp.where(strict, kk * decay, 0.0), eye)
        eg = jnp.exp(gc)
        uw = _bdot(t_inv, jnp.concatenate([v * beta, kb * eg], axis=1))
        u = uw[:, :GDN_DV]
        w = uw[:, GDN_DV:]
        qk = lax.dot_general(q.astype(BF16), k16, (((1,), (1,)), ((), ())), preferred_element_type=F32) * decay
        kd = k * jnp.exp(g_last - gc)
        s = s_ref[h]
        ws_qs = _bdot(jnp.concatenate([w, q * eg], axis=0), s)
        v_new = u - ws_qs[:CHUNK]
        o = ws_qs[CHUNK:] + _bdot(qk, v_new)
        s_ref[h] = s * jnp.exp(g_last) + lax.dot_general(
            kd.astype(BF16), v_new.astype(BF16), (((0,), (0,)), ((), ())), preferred_element_type=F32)
        o = _rms(o, gn_ref[...])
        cols = slice(h * GDN_DV, (h + 1) * GDN_DV)
        o_ref[:, cols] = (o * _silu(z_ref[:, cols])).astype(o_ref.dtype)

    ext_ref[0:SUBLANES, :] = ext_ref[CHUNK:CHUNK + SUBLANES, :]

    @pl.when(c == pl.num_programs(1) - 1)
    def _():
        s_out_ref[...] = s_ref[...]


def _gdn_prompt(qkv, z, bg, conv_w, gdn_norm):
    b, l, _ = qkv.shape
    nc = l // CHUNK
    bg4 = bg.reshape(b, nc, CHUNK, 2 * GDN_HEADS)
    bgt4 = jnp.swapaxes(bg4, 2, 3)
    seq = lambda w: pl.BlockSpec((None, CHUNK, w), lambda i, c: (i, c, 0))
    return pl.pallas_call(
        _gdn_prompt_kernel,
        grid=(b, nc),
        in_specs=[seq(QKV_CH), seq(V_DIM),
                  pl.BlockSpec((None, None, CHUNK, 2 * GDN_HEADS), lambda i, c: (i, c, 0, 0)),
                  pl.BlockSpec((None, None, 2 * GDN_HEADS, CHUNK), lambda i, c: (i, c, 0, 0)),
                  _const_spec(conv_w.shape), _const_spec((1, GDN_DV))],
        out_specs=[seq(V_DIM),
                   pl.BlockSpec((None, GDN_HEADS, GDN_DK, GDN_DV), lambda i, c: (i, 0, 0, 0))],
        out_shape=[jax.ShapeDtypeStruct((b, l, V_DIM), BF16),
                   jax.ShapeDtypeStruct((b, GDN_HEADS, GDN_DK, GDN_DV), F32)],
        scratch_shapes=[pltpu.VMEM((SUBLANES + CHUNK, QKV_CH), F32),
                        pltpu.VMEM((GDN_HEADS, GDN_DK, GDN_DV), F32)],
        compiler_params=_params(("parallel", "arbitrary")),
        name="gdn_prompt",
    )(qkv, z, bg4, bgt4, conv_w, gdn_norm)


GDN_SAMPLE_ROWS = 8


def _gdn_sample_kernel(raw_ref, cs_ref, z_ref, bg_ref, s_in_ref, cw_ref, gn_ref, o_ref, s_out_ref):
    conv = cw_ref[SHORT_CONV - 1:SHORT_CONV, :] * raw_ref[...]
    for j in range(SHORT_CONV - 1):
        conv = conv + cw_ref[j:j + 1, :] * cs_ref[j]
    qkv = _silu(conv)
    bg = bg_ref[...]
    for h in range(GDN_HEADS):
        q = qkv[:, h * GDN_DK:(h + 1) * GDN_DK]
        k = qkv[:, QK_DIM + h * GDN_DK:QK_DIM + (h + 1) * GDN_DK]
        v = qkv[:, 2 * QK_DIM + h * GDN_DV:2 * QK_DIM + (h + 1) * GDN_DV]
        q = q * lax.rsqrt(jnp.sum(q * q, axis=-1, keepdims=True) + NORM_EPS) * (GDN_DK ** -0.5)
        k = k * lax.rsqrt(jnp.sum(k * k, axis=-1, keepdims=True) + NORM_EPS)
        beta = bg[:, h:h + 1]
        eg = jnp.exp(bg[:, GDN_HEADS + h:GDN_HEADS + h + 1])
        u = v * beta
        qk = jnp.sum(q * k, axis=-1, keepdims=True)
        w_t = (k * (beta * eg)).T
        qg_t = (q * eg).T
        k_t = k.T
        o_rows = []
        for r in range(GDN_SAMPLE_ROWS):
            s = s_in_ref[r, h]
            w_s = jnp.sum(w_t[:, r:r + 1] * s, axis=0, keepdims=True)
            q_s = jnp.sum(qg_t[:, r:r + 1] * s, axis=0, keepdims=True)
            v_new = u[r:r + 1] - w_s
            o_rows.append(q_s + qk[r:r + 1] * v_new)
            s_out_ref[r, h] = s * eg[r:r + 1] + k_t[:, r:r + 1] * v_new
        o = _rms(jnp.concatenate(o_rows, axis=0), gn_ref[...])
        cols = slice(h * GDN_DV, (h + 1) * GDN_DV)
        o_ref[:, cols] = (o * _silu(z_ref[:, cols])).astype(o_ref.dtype)


def _gdn_sample(raw, conv_state_t, z, bg, state, conv_w, gdn_norm):
    n = raw.shape[0]
    rb = GDN_SAMPLE_ROWS
    row = lambda w: pl.BlockSpec((rb, w), lambda i: (i, 0))
    st = pl.BlockSpec((rb, GDN_HEADS, GDN_DK, GDN_DV), lambda i: (i, 0, 0, 0))
    return pl.pallas_call(
        _gdn_sample_kernel,
        grid=(n // rb,),
        in_specs=[row(QKV_CH), pl.BlockSpec((SHORT_CONV - 1, rb, QKV_CH), lambda i: (0, i, 0)),
                  row(V_DIM), row(2 * GDN_HEADS), st, _const_spec(conv_w.shape), _const_spec((1, GDN_DV))],
        out_specs=[row(V_DIM), st],
        out_shape=[jax.ShapeDtypeStruct((n, V_DIM), BF16), jax.ShapeDtypeStruct(state.shape, F32)],
        compiler_params=_params(("parallel",)),
        name="gdn_sample",
    )(raw, conv_state_t, z, bg, state, conv_w, gdn_norm)


CONF_TILE = 256
CONF_ROWS = 64


def _ln_silu(x, g, b):
    mu = jnp.mean(x, axis=-1, keepdims=True)
    xc = x - mu
    var = jnp.mean(xc * xc, axis=-1, keepdims=True)
    return _silu(xc * lax.rsqrt(var + LN_EPS) * g + b)


def _conf_prompt_kernel(u_ref, w_ref, b_ref, lng_ref, lnb_ref, cv_ref, ubuf_ref, acc_ref):
    @pl.when(pl.program_id(1) == 0)
    def _():
        ubuf_ref[0:CONF_HALO, :] = jnp.zeros((CONF_HALO, CONF_CH), F32)

    ubuf_ref[CONF_HALO:CONF_HALO + CONF_TILE, :] = u_ref[...]
    base = CONF_HALO - (CONF_K - 1)
    for rb in range(CONF_TILE // CONF_ROWS):
        for cb in range(CONF_CH // LANES):
            cols = slice(cb * LANES, (cb + 1) * LANES)
            r0 = base + rb * CONF_ROWS
            acc = w_ref[0:1, cols] * ubuf_ref[r0:r0 + CONF_ROWS, cols]
            for j in range(1, CONF_K):
                acc = acc + w_ref[j:j + 1, cols] * ubuf_ref[r0 + j:r0 + j + CONF_ROWS, cols]
            acc_ref[rb * CONF_ROWS:(rb + 1) * CONF_ROWS, cols] = acc + b_ref[:, cols]
    ubuf_ref[0:CONF_HALO, :] = ubuf_ref[CONF_TILE:CONF_TILE + CONF_HALO, :]
    cv_ref[...] = _ln_silu(acc_ref[...], lng_ref[...], lnb_ref[...]).astype(cv_ref.dtype)


def _conf_prompt(u, w, b, ln_g, ln_b):
    bsz, l, _ = u.shape
    seq = pl.BlockSpec((None, CONF_TILE, CONF_CH), lambda i, t: (i, t, 0))
    return pl.pallas_call(
        _conf_prompt_kernel,
        grid=(bsz, l // CONF_TILE),
        in_specs=[seq, _const_spec(w.shape), _const_spec((1, CONF_CH)), _const_spec((1, CONF_CH)),
                  _const_spec((1, CONF_CH))],
        out_specs=seq,
        out_shape=jax.ShapeDtypeStruct((bsz, l, CONF_CH), BF16),
        scratch_shapes=[pltpu.VMEM((CONF_HALO + CONF_TILE, CONF_CH), F32), pltpu.VMEM((CONF_TILE, CONF_CH), F32)],
        compiler_params=_params(("parallel", "arbitrary")),
        name="conf_prompt",
    )(u, w, b, ln_g, ln_b)


def _conf_sample_kernel(u_ref, cs_ref, w_ref, b_ref, lng_ref, lnb_ref, cv_ref):
    acc = w_ref[CONF_K - 1:CONF_K, :] * u_ref[...] + b_ref[...]
    for j in range(CONF_K - 1):
        acc = acc + w_ref[j:j + 1, :] * cs_ref[j]
    cv_ref[...] = _ln_silu(acc, lng_ref[...], lnb_ref[...]).astype(cv_ref.dtype)


def _conf_sample(u, conf_state_t, w, b, ln_g, ln_b):
    n = u.shape[0]
    return pl.pallas_call(
        _conf_sample_kernel,
        out_shape=jax.ShapeDtypeStruct((n, CONF_CH), BF16),
        compiler_params=pltpu.CompilerParams(vmem_limit_bytes=VMEM_LIMIT_BYTES),
        name="conf_sample",
    )(u, conf_state_t, w, b, ln_g, ln_b)


def _mix_kernel(x_ref, o_ref, cv_ref, ga_ref, gb_ref, wgo_ref, wco_ref, wout_ref, nffn_ref, wr_ref, br_ref,
                x1_ref, hm_ref, gates_ref):
    branch_a = _dot(o_ref[...], wgo_ref[...])
    branch_b = _dot(cv_ref[...], wco_ref[...])
    merged = ga_ref[...] * branch_a + gb_ref[...] * branch_b
    x1 = x_ref[...] + _bdot(merged, wout_ref[...])
    x1_ref[...] = x1
    hm = _rms(x1, nffn_ref[...])
    hm_ref[...] = hm.astype(hm_ref.dtype)
    logits = _hdot(hm, wr_ref[...]) + br_ref[...]
    lane = lax.broadcasted_iota(jnp.int32, logits.shape, 1)
    work = logits
    top_v, picks = [], []
    for _ in range(TOP_K):
        m = jnp.max(work, axis=-1, keepdims=True)
        first = jnp.min(jnp.where(work == m, lane, N_EXPERTS), axis=-1, keepdims=True)
        pick = lane == first
        top_v.append(m)
        picks.append(pick)
        work = jnp.where(pick, -jnp.inf, work)
    ex = [jnp.exp(v - top_v[0]) for v in top_v]
    den = ex[0]
    for e in ex[1:]:
        den = den + e
    gates = jnp.zeros(logits.shape, F32)
    for pick, e in zip(picks, ex):
        gates = gates + jnp.where(pick, e / den, 0.0)
    gates_ref[...] = gates


def _mix(x, o, cv, ga, gb, wgo, wco, wout, norm_ffn, w_router, b_router, tm):
    n = x.shape[0]
    row = lambda w: pl.BlockSpec((tm, w), lambda i: (i, 0))
    return pl.pallas_call(
        _mix_kernel,
        grid=(n // tm,),
        in_specs=[row(D_MODEL), row(V_DIM), row(CONF_CH), row(D_MODEL), row(D_MODEL),
                  _const_spec(wgo.shape), _const_spec(wco.shape), _const_spec(wout.shape),
                  _const_spec((1, D_MODEL)), _const_spec(w_router.shape), _const_spec((1, N_EXPERTS))],
        out_specs=[row(D_MODEL), row(D_MODEL), row(N_EXPERTS)],
        out_shape=[jax.ShapeDtypeStruct((n, D_MODEL), F32), jax.ShapeDtypeStruct((n, D_MODEL), BF16),
                   jax.ShapeDtypeStruct((n, N_EXPERTS), F32)],
        compiler_params=_params(("parallel",)),
        name="mix",
    )(x, o, cv, ga, gb, wgo, wco, wout, norm_ffn, w_router, b_router)


def _moe_kernel(x1_ref, hm_ref, gates_ref, wgu_ref, bgu_ref, wd_ref, bd_ref, x2_ref, acc_ref):
    e = pl.program_id(1)

    @pl.when(e == 0)
    def _():
        acc_ref[...] = jnp.zeros(acc_ref.shape, F32)

    gates = gates_ref[...]
    lane = lax.broadcasted_iota(jnp.int32, gates.shape, 1)
    ge = jnp.sum(jnp.where(lane == e, gates, 0.0), axis=-1, keepdims=True)
    gu = _dot(hm_ref[...], wgu_ref[...]) + bgu_ref[...]
    gate = jnp.minimum(gu[:, :D_FF], SWIGLU_LIMIT)
    up = jnp.clip(gu[:, D_FF:], -SWIGLU_LIMIT, SWIGLU_LIMIT)
    hid = (up + 1.0) * (gate * _sigmoid(SWIGLU_ALPHA * gate))
    ye = _bdot(hid, wd_ref[...]) + bd_ref[...]
    acc_ref[...] += ge * ye

    @pl.when(e == pl.num_programs(1) - 1)
    def _():
        x2_ref[...] = x1_ref[...] + acc_ref[...]


def _moe(x1, hm, gates, wgu, bgu, wd, bd, tm):
    n = x1.shape[0]
    row = lambda w: pl.BlockSpec((tm, w), lambda i, e: (i, 0))
    return pl.pallas_call(
        _moe_kernel,
        grid=(n // tm, N_EXPERTS),
        in_specs=[row(D_MODEL), row(D_MODEL), row(N_EXPERTS),
                  pl.BlockSpec((None, D_MODEL, 2 * D_FF), lambda i, e: (e, 0, 0)),
                  pl.BlockSpec((None, 1, 2 * D_FF), lambda i, e: (e, 0, 0)),
                  pl.BlockSpec((None, D_FF, D_MODEL), lambda i, e: (e, 0, 0)),
                  pl.BlockSpec((None, 1, D_MODEL), lambda i, e: (e, 0, 0))],
        out_specs=row(D_MODEL),
        out_shape=jax.ShapeDtypeStruct((n, D_MODEL), F32),
        scratch_shapes=[pltpu.VMEM((tm, D_MODEL), F32)],
        compiler_params=_params(("parallel", "arbitrary")),
        name="moe",
    )(x1, hm, gates, wgu, bgu, wd, bd)


def _ple_kernel(x2_ref, p_ref, nple_ref, wg_ref, wp_ref, nfin_ref, y_ref):
    x2 = x2_ref[...]
    gate = _sigmoid(_bdot(_rms(x2, nple_ref[...]), wg_ref[...]))
    x3 = x2 + _bdot(p_ref[...], wp_ref[...]) * gate
    y_ref[...] = _rms(x3, nfin_ref[...])


def _ple(x2, p, norm_ple, wg, wp, norm_final, tm):
    n = x2.shape[0]
    row = lambda w: pl.BlockSpec((tm, w), lambda i: (i, 0))
    return pl.pallas_call(
        _ple_kernel,
        grid=(n // tm,),
        in_specs=[row(D_MODEL), row(PLE_DIM), _const_spec((1, D_MODEL)), _const_spec(wg.shape),
                  _const_spec(wp.shape), _const_spec((1, D_MODEL))],
        out_specs=row(D_MODEL),
        out_shape=jax.ShapeDtypeStruct((n, D_MODEL), F32),
        compiler_params=_params(("parallel",)),
        name="ple",
    )(x2, p, norm_ple, wg, wp, norm_final)


def kernel(x_prompt, x_sample, p_prompt, p_sample, state_gdn, state_qkv_conv, state_conf_conv, norm_mix, w_in, conv_qkv_w, a_log, dt_bias, gdn_norm, w_gdn_out, conf_dw_w, conf_dw_b, conf_ln_g, conf_ln_b, w_conf_out, w_out, norm_ffn, w_router, b_router, w_gate_up, b_gate_up, w_down, b_down, norm_ple, w_ple_gate, w_ple_proj, norm_final):
    assert w_in.shape[0] == 1, "single-layer trunk"
    bsz, seq, _ = x_prompt.shape
    n_p = bsz * seq
    n_s = x_sample.shape[0] * x_sample.shape[1]
    assert x_sample.shape[1] == 1

    w_in0 = w_in[0]
    wqkv = w_in0[:, :OFF_Z].astype(BF16)
    wz = w_in0[:, OFF_Z:OFF_BETA].astype(BF16)
    wba = w_in0[:, OFF_BETA:OFF_GLU].astype(BF16)
    wglu = w_in0[:, OFF_GLU:OFF_GATE].astype(BF16)
    wgate = w_in0[:, OFF_GATE:].astype(BF16)
    zeros_h = jnp.zeros((GDN_HEADS,), F32)
    alog16 = jnp.concatenate([zeros_h, a_log[0]]).reshape(1, 2 * GDN_HEADS)
    dtb16 = jnp.concatenate([zeros_h, dt_bias[0]]).reshape(1, 2 * GDN_HEADS)
    wgo = w_gdn_out[0].astype(BF16)
    wco = w_conf_out[0].astype(BF16)
    wout = w_out[0].astype(BF16)
    wpg = w_ple_gate[0].astype(BF16)
    wpp = w_ple_proj[0].astype(BF16)
    wgu = w_gate_up[0].astype(BF16)
    wd = w_down[0].astype(BF16)
    bgu = b_gate_up[0].reshape(N_EXPERTS, 1, 2 * D_FF)
    bd = b_down[0].reshape(N_EXPERTS, 1, D_MODEL)
    row = lambda v: v.reshape(1, -1)
    inproj_w = (row(norm_mix[0]), wqkv, wz, wba, wglu, wgate, alog16, dtb16)
    mix_w = (wgo, wco, wout, row(norm_ffn[0]), w_router[0], row(b_router[0]))
    ple_w = (row(norm_ple[0]), wpg, wpp, row(norm_final))
    conf_w = (conf_dw_w[0], row(conf_dw_b[0]), row(conf_ln_g[0]), row(conf_ln_b[0]))

    def tail(x, o, cv, ga, gb, p, tm):
        x1, hm, gates = _mix(x, o, cv, ga, gb, *mix_w, tm=tm)
        x2 = _moe(x1, hm, gates, wgu, bgu, wd, bd, tm=tm)
        return _ple(x2, p, *ple_w, tm=tm)

    xp = x_prompt.reshape(n_p, D_MODEL)
    qkv_p, z_p, bg_p, u_p, ga_p, gb_p = _inproj(xp, *inproj_w, tm=256)
    qkv_p3 = qkv_p.reshape(bsz, seq, QKV_CH)
    u_p3 = u_p.reshape(bsz, seq, CONF_CH)
    o_p, s_p = _gdn_prompt(qkv_p3, z_p.reshape(bsz, seq, V_DIM), bg_p, conv_qkv_w[0], row(gdn_norm[0]))
    cv_p = _conf_prompt(u_p3, *conf_w)
    y_p = tail(xp, o_p.reshape(n_p, V_DIM), cv_p.reshape(n_p, CONF_CH), ga_p, gb_p,
               p_prompt[0].reshape(n_p, PLE_DIM), tm=512)

    xs = x_sample.reshape(n_s, D_MODEL)
    qkv_s, z_s, bg_s, u_s, ga_s, gb_s = _inproj(xs, *inproj_w, tm=n_s)
    o_s, s_s = _gdn_sample(qkv_s, jnp.swapaxes(state_qkv_conv[0], 0, 1), z_s, bg_s, state_gdn[0],
                           conv_qkv_w[0], row(gdn_norm[0]))
    cv_s = _conf_sample(u_s, jnp.swapaxes(state_conf_conv[0], 0, 1), *conf_w)
    y_s = tail(xs, o_s, cv_s, ga_s, gb_s, p_sample[0].reshape(n_s, PLE_DIM), tm=n_s)

    new_qkv_s = jnp.concatenate([state_qkv_conv[0][:, 1:], qkv_s[:, None, :]], axis=1)
    new_conf_s = jnp.concatenate([state_conf_conv[0][:, 1:], u_s[:, None, :]], axis=1)
    return (y_p.reshape(bsz, seq, D_MODEL), y_s.reshape(n_s, 1, D_MODEL),
            s_p[None], qkv_p3[:, seq - (SHORT_CONV - 1):][None], u_p3[:, seq - (CONF_K - 1):][None],
            s_s[None], new_qkv_s[None], new_conf_s[None])
```

```python
import functools

import jax
import jax.numpy as jnp
from jax import lax
from jax.experimental import pallas as pl
from jax.experimental.pallas import tpu as pltpu

F32 = jnp.float32
BF16 = jnp.bfloat16

D_MODEL = 1024
GDN_HEADS = 8
GDN_DK = 128
GDN_DV = 128
QK_DIM = GDN_HEADS * GDN_DK
V_DIM = GDN_HEADS * GDN_DV
QKV_CH = 2 * QK_DIM + V_DIM
SHORT_CONV = 4
CHUNK = 64
CONF_CH = D_MODEL // 2
CONF_K = 31
N_EXPERTS = 32
TOP_K = 4
D_FF = D_MODEL
SWIGLU_LIMIT = 7.0
SWIGLU_ALPHA = 1.702
PLE_DIM = 256
NORM_EPS = 1e-6
LN_EPS = 1e-5

OFF_Z = QKV_CH
OFF_BETA = OFF_Z + V_DIM
OFF_GLU = OFF_BETA + 2 * GDN_HEADS
OFF_GATE = OFF_GLU + 2 * CONF_CH

VMEM_LIMIT_BYTES = 52 * 1024 * 1024
SUBLANES = 8
LANES = 128
CONF_HALO = 32


def _dot(a, b):
    return jnp.dot(a, b, preferred_element_type=F32)


def _bdot(a, b):
    return jnp.dot(a.astype(BF16), b.astype(BF16), preferred_element_type=F32)


def _hdot(a, b):
    return jnp.dot(a, b, preferred_element_type=F32, precision=lax.Precision.HIGHEST)


def _sigmoid(x):
    return 1.0 / (1.0 + jnp.exp(-x))


def _silu(x):
    return x * _sigmoid(x)


def _rms(x, g):
    return x * lax.rsqrt(jnp.mean(x * x, axis=-1, keepdims=True) + NORM_EPS) * g


def _const_spec(shape):
    nd = len(shape)
    return pl.BlockSpec(shape, lambda *_: (0,) * nd, pipeline_mode=pl.Buffered(1))


def _params(sem):
    return pltpu.CompilerParams(dimension_semantics=sem, vmem_limit_bytes=VMEM_LIMIT_BYTES)


def _inproj_kernel(x_ref, nw_ref, wqkv_ref, wz_ref, wba_ref, wglu_ref, wgate_ref, alog_ref, dtb_ref,
                   qkv_ref, z_ref, bg_ref, u_ref, ga_ref, gb_ref):
    a = _rms(x_ref[...], nw_ref[...]).astype(BF16)
    for c in range(QKV_CH // D_MODEL):
        cols = slice(c * D_MODEL, (c + 1) * D_MODEL)
        qkv_ref[:, cols] = _dot(a, wqkv_ref[:, cols])
    z_ref[...] = _dot(a, wz_ref[...])
    ba = _dot(a, wba_ref[...])
    lane = lax.broadcasted_iota(jnp.int32, ba.shape, 1)
    t = ba + dtb_ref[...]
    softplus = jnp.maximum(t, 0.0) + jnp.log(1.0 + jnp.exp(-jnp.abs(t)))
    bg_ref[...] = jnp.where(lane < GDN_HEADS, _sigmoid(ba), -jnp.exp(alog_ref[...]) * softplus)
    glu = _dot(a, wglu_ref[...])
    u_ref[...] = glu[:, :CONF_CH] * _sigmoid(glu[:, CONF_CH:])
    ga_ref[...] = _sigmoid(_dot(a, wgate_ref[:, :D_MODEL]))
    gb_ref[...] = _sigmoid(_dot(a, wgate_ref[:, D_MODEL:]))


def _inproj(x, norm_w, wqkv, wz, wba, wglu, wgate, alog16, dtb16, tm):
    n = x.shape[0]
    row = lambda w: pl.BlockSpec((tm, w), lambda i: (i, 0))
    return pl.pallas_call(
        _inproj_kernel,
        grid=(n // tm,),
        in_specs=[row(D_MODEL), _const_spec((1, D_MODEL)), _const_spec(wqkv.shape), _const_spec(wz.shape),
                  _const_spec(wba.shape), _const_spec(wglu.shape), _const_spec(wgate.shape),
                  _const_spec((1, 2 * GDN_HEADS)), _const_spec((1, 2 * GDN_HEADS))],
        out_specs=[row(QKV_CH), row(V_DIM), row(2 * GDN_HEADS), row(CONF_CH), row(D_MODEL), row(D_MODEL)],
        out_shape=[jax.ShapeDtypeStruct((n, w), F32)
                   for w in (QKV_CH, V_DIM, 2 * GDN_HEADS, CONF_CH, D_MODEL, D_MODEL)],
        compiler_params=_params(("parallel",)),
        name="inproj",
    )(x, norm_w, wqkv, wz, wba, wglu, wgate, alog16, dtb16)


GDN_GROUP = 4
GROUP_ROWS = GDN_GROUP * CHUNK
N_GROUPS = GDN_HEADS // GDN_GROUP


def _block_unit_lower_inverse(a, eye):
    b = -a
    t = eye + b
    p = _bdot(b, b)
    covered = 2
    while covered < CHUNK:
        covered *= 2
        if covered < CHUNK:
            r = _bdot(jnp.concatenate([t, p], axis=0), p)
            t = t + r[:GROUP_ROWS]
            p = r[GROUP_ROWS:]
        else:
            t = t + _bdot(t, p)
    return t


def _gdn_prep_kernel(qkv_ref, halo_ref, bg_ref, bgt_ref, cw_ref, wqg_ref, kdt_ref, qk_ref, u_ref, gl_ref, ext_ref):
    c = pl.program_id(1)
    ext_ref[0:SUBLANES, :] = jnp.where(c > 0, halo_ref[...], 0.0)
    ext_ref[SUBLANES:SUBLANES + CHUNK, :] = qkv_ref[...]

    def convsilu(c0):
        cols = slice(c0, c0 + LANES)
        acc = cw_ref[SHORT_CONV - 1:SHORT_CONV, cols] * ext_ref[SUBLANES:SUBLANES + CHUNK, cols]
        for j in range(SHORT_CONV - 1):
            r0 = SUBLANES - (SHORT_CONV - 1) + j
            acc = acc + cw_ref[j:j + 1, cols] * ext_ref[r0:r0 + CHUNK, cols]
        return _silu(acc)

    r64 = lax.broadcasted_iota(jnp.int32, (CHUNK, CHUNK), 0)
    c64 = lax.broadcasted_iota(jnp.int32, (CHUNK, CHUNK), 1)
    bg = bg_ref[...]
    gc_all = _hdot((r64 >= c64).astype(F32), bg)
    gct_all = _hdot(bgt_ref[...], (r64 <= c64).astype(F32))
    gl_ref[...] = gc_all[CHUNK - 1:CHUNK, :]

    row = lax.broadcasted_iota(jnp.int32, (GROUP_ROWS, GROUP_ROWS), 0)
    col = lax.broadcasted_iota(jnp.int32, (GROUP_ROWS, GROUP_ROWS), 1)
    blk = row - col + (col & (CHUNK - 1))
    same = (blk >= 0) & (blk < CHUNK)
    tril = same & (row >= col)
    strict = same & (row > col)
    eye = (row == col).astype(F32)
    nt = (((1,), (1,)), ((), ()))

    for g in range(N_GROUPS):
        heads = range(g * GDN_GROUP, (g + 1) * GDN_GROUP)
        stack = lambda f: jnp.concatenate([f(h) for h in heads], axis=0)
        q = stack(lambda h: convsilu(h * GDN_DK))
        k = stack(lambda h: convsilu(QK_DIM + h * GDN_DK))
        v = stack(lambda h: convsilu(2 * QK_DIM + h * GDN_DV))
        q = q * lax.rsqrt(jnp.sum(q * q, axis=-1, keepdims=True) + NORM_EPS) * (GDN_DK ** -0.5)
        k = k * lax.rsqrt(jnp.sum(k * k, axis=-1, keepdims=True) + NORM_EPS)
        beta = stack(lambda h: bg[:, h:h + 1])
        gc = stack(lambda h: gc_all[:, GDN_HEADS + h:GDN_HEADS + h + 1])
        g_last = stack(lambda h: jnp.broadcast_to(
            gc_all[CHUNK - 1:CHUNK, GDN_HEADS + h:GDN_HEADS + h + 1], (CHUNK, 1)))
        gct = jnp.concatenate([gct_all[GDN_HEADS + h:GDN_HEADS + h + 1, :] for h in heads], axis=1)
        decay = jnp.exp(jnp.where(tril, gc - gct, -jnp.inf))
        kb = k * beta
        k16 = k.astype(BF16)
        kk = lax.dot_general(kb.astype(BF16), k16, nt, preferred_element_type=F32)
        t_inv = _block_unit_lower_inverse(jnp.where(strict, kk * decay, 0.0), eye)
        eg = jnp.exp(gc)
        uw = _bdot(t_inv, jnp.concatenate([v * beta, kb * eg], axis=1))
        u_ref[g] = uw[:, :GDN_DV]
        w = uw[:, GDN_DV:].astype(BF16)
        qg = (q * eg).astype(BF16)
        for i, h in enumerate(heads):
            rows = slice(i * CHUNK, (i + 1) * CHUNK)
            wqg_ref[h, 0:CHUNK, :] = w[rows]
            wqg_ref[h, CHUNK:2 * CHUNK, :] = qg[rows]
        qk = lax.dot_general(q.astype(BF16), k16, nt, preferred_element_type=F32) * decay
        qk_ref[g] = qk.astype(BF16)
        kdt_ref[g] = (k * jnp.exp(g_last - gc)).T.astype(BF16)


def _gdn_scan_kernel(wqg_ref, kdt_ref, qk_ref, u_ref, gl_ref, z_ref, gn_ref, o_ref, s_out_ref, s_ref):
    c = pl.program_id(1)

    @pl.when(c == 0)
    def _():
        s_ref[...] = jnp.zeros(s_ref.shape, F32)

    dl_all = jnp.exp(gl_ref[...])
    zeros = jnp.zeros((CHUNK, GDN_DV), BF16)
    for g in range(N_GROUPS):
        heads = list(range(g * GDN_GROUP, (g + 1) * GDN_GROUP))
        v_new, q_s = [], []
        for i, h in enumerate(heads):
            r = _dot(wqg_ref[h], s_ref[h].astype(BF16))
            v_new.append((u_ref[g, i * CHUNK:(i + 1) * CHUNK, :] - r[:CHUNK]).astype(BF16))
            q_s.append(r[CHUNK:])
        o_all = jnp.concatenate(q_s, axis=0) + _dot(qk_ref[g], jnp.concatenate(v_new, axis=0))
        for i, h in enumerate(heads):
            vm = jnp.concatenate([v_new[j] if j == i else zeros for j in range(GDN_GROUP)], axis=0)
            s_ref[h] = s_ref[h] * dl_all[:, GDN_HEADS + h:GDN_HEADS + h + 1] + _dot(kdt_ref[g], vm)
            o = _rms(o_all[i * CHUNK:(i + 1) * CHUNK], gn_ref[...])
            cols = slice(h * GDN_DV, (h + 1) * GDN_DV)
            o_ref[:, cols] = (o * _silu(z_ref[:, cols])).astype(o_ref.dtype)

    @pl.when(c == pl.num_programs(1) - 1)
    def _():
        s_out_ref[...] = s_ref[...]


def _gdn_prompt(qkv, z, bg, conv_w, gdn_norm):
    b, l, _ = qkv.shape
    nc = l // CHUNK
    bg4 = bg.reshape(b, nc, CHUNK, 2 * GDN_HEADS)
    bgt4 = jnp.swapaxes(bg4, 2, 3)
    halo_blocks = CHUNK // SUBLANES
    seq = lambda w: pl.BlockSpec((None, CHUNK, w), lambda i, c: (i, c, 0))
    chunk = lambda *s: pl.BlockSpec((None, None) + s, lambda i, c: (i, c) + (0,) * len(s))
    inter_shapes = [((GDN_HEADS, 2 * CHUNK, GDN_DK), BF16), ((N_GROUPS, GDN_DK, GROUP_ROWS), BF16),
                    ((N_GROUPS, GROUP_ROWS, GROUP_ROWS), BF16), ((N_GROUPS, GROUP_ROWS, GDN_DV), F32),
                    ((1, 2 * GDN_HEADS), F32)]
    inter_specs = [chunk(*s) for s, _ in inter_shapes]
    wqg, kdt, qk, u, gl = pl.pallas_call(
        _gdn_prep_kernel,
        grid=(b, nc),
        in_specs=[seq(QKV_CH),
                  pl.BlockSpec((None, SUBLANES, QKV_CH), lambda i, c: (i, jnp.maximum(c * halo_blocks - 1, 0), 0)),
                  chunk(CHUNK, 2 * GDN_HEADS), chunk(2 * GDN_HEADS, CHUNK), _const_spec(conv_w.shape)],
        out_specs=inter_specs,
        out_shape=[jax.ShapeDtypeStruct((b, nc) + s, d) for s, d in inter_shapes],
        scratch_shapes=[pltpu.VMEM((SUBLANES + CHUNK, QKV_CH), F32)],
        compiler_params=_params(("parallel", "parallel")),
        name="gdn_prep",
    )(qkv, qkv, bg4, bgt4, conv_w)
    return pl.pallas_call(
        _gdn_scan_kernel,
        grid=(b, nc),
        in_specs=inter_specs + [seq(V_DIM), _const_spec((1, GDN_DV))],
        out_specs=[seq(V_DIM),
                   pl.BlockSpec((None, GDN_HEADS, GDN_DK, GDN_DV), lambda i, c: (i, 0, 0, 0))],
        out_shape=[jax.ShapeDtypeStruct((b, l, V_DIM), BF16),
                   jax.ShapeDtypeStruct((b, GDN_HEADS, GDN_DK, GDN_DV), F32)],
        scratch_shapes=[pltpu.VMEM((GDN_HEADS, GDN_DK, GDN_DV), F32)],
        compiler_params=_params(("parallel", "arbitrary")),
        name="gdn_scan",
    )(wqg, kdt, qk, u, gl, z, gdn_norm)


GDN_SAMPLE_ROWS = 8


def _gdn_sample_kernel(raw_ref, cs_ref, z_ref, bg_ref, s_in_ref, cw_ref, gn_ref, o_ref, s_out_ref):
    conv = cw_ref[SHORT_CONV - 1:SHORT_CONV, :] * raw_ref[...]
    for j in range(SHORT_CONV - 1):
        conv = conv + cw_ref[j:j + 1, :] * cs_ref[j]
    qkv = _silu(conv)
    bg = bg_ref[...]
    for h in range(GDN_HEADS):
        q = qkv[:, h * GDN_DK:(h + 1) * GDN_DK]
        k = qkv[:, QK_DIM + h * GDN_DK:QK_DIM + (h + 1) * GDN_DK]
        v = qkv[:, 2 * QK_DIM + h * GDN_DV:2 * QK_DIM + (h + 1) * GDN_DV]
        q = q * lax.rsqrt(jnp.sum(q * q, axis=-1, keepdims=True) + NORM_EPS) * (GDN_DK ** -0.5)
        k = k * lax.rsqrt(jnp.sum(k * k, axis=-1, keepdims=True) + NORM_EPS)
        beta = bg[:, h:h + 1]
        eg = jnp.exp(bg[:, GDN_HEADS + h:GDN_HEADS + h + 1])
        u = v * beta
        qk = jnp.sum(q * k, axis=-1, keepdims=True)
        w_t = (k * (beta * eg)).T
        qg_t = (q * eg).T
        k_t = k.T
        o_rows = []
        for r in range(GDN_SAMPLE_ROWS):
            s = s_in_ref[r, h]
            w_s = jnp.sum(w_t[:, r:r + 1] * s, axis=0, keepdims=True)
            q_s = jnp.sum(qg_t[:, r:r + 1] * s, axis=0, keepdims=True)
            v_new = u[r:r + 1] - w_s
            o_rows.append(q_s + qk[r:r + 1] * v_new)
            s_out_ref[r, h] = s * eg[r:r + 1] + k_t[:, r:r + 1] * v_new
        o = _rms(jnp.concatenate(o_rows, axis=0), gn_ref[...])
        cols = slice(h * GDN_DV, (h + 1) * GDN_DV)
        o_ref[:, cols] = (o * _silu(z_ref[:, cols])).astype(o_ref.dtype)


def _gdn_sample(raw, conv_state_t, z, bg, state, conv_w, gdn_norm):
    n = raw.shape[0]
    rb = GDN_SAMPLE_ROWS
    row = lambda w: pl.BlockSpec((rb, w), lambda i: (i, 0))
    st = pl.BlockSpec((rb, GDN_HEADS, GDN_DK, GDN_DV), lambda i: (i, 0, 0, 0))
    return pl.pallas_call(
        _gdn_sample_kernel,
        grid=(n // rb,),
        in_specs=[row(QKV_CH), pl.BlockSpec((SHORT_CONV - 1, rb, QKV_CH), lambda i: (0, i, 0)),
                  row(V_DIM), row(2 * GDN_HEADS), st, _const_spec(conv_w.shape), _const_spec((1, GDN_DV))],
        out_specs=[row(V_DIM), st],
        out_shape=[jax.ShapeDtypeStruct((n, V_DIM), BF16), jax.ShapeDtypeStruct(state.shape, F32)],
        compiler_params=_params(("parallel",)),
        name="gdn_sample",
    )(raw, conv_state_t, z, bg, state, conv_w, gdn_norm)


CONF_TILE = 256
CONF_ROWS = 64


def _ln_silu(x, g, b):
    mu = jnp.mean(x, axis=-1, keepdims=True)
    xc = x - mu
    var = jnp.mean(xc * xc, axis=-1, keepdims=True)
    return _silu(xc * lax.rsqrt(var + LN_EPS) * g + b)


def _conf_prompt_kernel(u_ref, w_ref, b_ref, lng_ref, lnb_ref, cv_ref, ubuf_ref, acc_ref):
    @pl.when(pl.program_id(1) == 0)
    def _():
        ubuf_ref[0:CONF_HALO, :] = jnp.zeros((CONF_HALO, CONF_CH), F32)

    ubuf_ref[CONF_HALO:CONF_HALO + CONF_TILE, :] = u_ref[...]
    base = CONF_HALO - (CONF_K - 1)
    for rb in range(CONF_TILE // CONF_ROWS):
        for cb in range(CONF_CH // LANES):
            cols = slice(cb * LANES, (cb + 1) * LANES)
            r0 = base + rb * CONF_ROWS
            acc = w_ref[0:1, cols] * ubuf_ref[r0:r0 + CONF_ROWS, cols]
            for j in range(1, CONF_K):
                acc = acc + w_ref[j:j + 1, cols] * ubuf_ref[r0 + j:r0 + j + CONF_ROWS, cols]
            acc_ref[rb * CONF_ROWS:(rb + 1) * CONF_ROWS, cols] = acc + b_ref[:, cols]
    ubuf_ref[0:CONF_HALO, :] = ubuf_ref[CONF_TILE:CONF_TILE + CONF_HALO, :]
    cv_ref[...] = _ln_silu(acc_ref[...], lng_ref[...], lnb_ref[...]).astype(cv_ref.dtype)


def _conf_prompt(u, w, b, ln_g, ln_b):
    bsz, l, _ = u.shape
    seq = pl.BlockSpec((None, CONF_TILE, CONF_CH), lambda i, t: (i, t, 0))
    return pl.pallas_call(
        _conf_prompt_kernel,
        grid=(bsz, l // CONF_TILE),
        in_specs=[seq, _const_spec(w.shape), _const_spec((1, CONF_CH)), _const_spec((1, CONF_CH)),
                  _const_spec((1, CONF_CH))],
        out_specs=seq,
        out_shape=jax.ShapeDtypeStruct((bsz, l, CONF_CH), BF16),
        scratch_shapes=[pltpu.VMEM((CONF_HALO + CONF_TILE, CONF_CH), F32), pltpu.VMEM((CONF_TILE, CONF_CH), F32)],
        compiler_params=_params(("parallel", "arbitrary")),
        name="conf_prompt",
    )(u, w, b, ln_g, ln_b)


def _conf_sample_kernel(u_ref, cs_ref, w_ref, b_ref, lng_ref, lnb_ref, cv_ref):
    acc = w_ref[CONF_K - 1:CONF_K, :] * u_ref[...] + b_ref[...]
    for j in range(CONF_K - 1):
        acc = acc + w_ref[j:j + 1, :] * cs_ref[j]
    cv_ref[...] = _ln_silu(acc, lng_ref[...], lnb_ref[...]).astype(cv_ref.dtype)


def _conf_sample(u, conf_state_t, w, b, ln_g, ln_b):
    n = u.shape[0]
    return pl.pallas_call(
        _conf_sample_kernel,
        out_shape=jax.ShapeDtypeStruct((n, CONF_CH), BF16),
        compiler_params=pltpu.CompilerParams(vmem_limit_bytes=VMEM_LIMIT_BYTES),
        name="conf_sample",
    )(u, conf_state_t, w, b, ln_g, ln_b)


def _mix_kernel(x_ref, o_ref, cv_ref, ga_ref, gb_ref, wgo_ref, wco_ref, wout_ref, nffn_ref, wr_ref, br_ref,
                x1_ref, hm_ref, gates_ref):
    branch_a = _dot(o_ref[...], wgo_ref[...])
    branch_b = _dot(cv_ref[...], wco_ref[...])
    merged = ga_ref[...] * branch_a + gb_ref[...] * branch_b
    x1 = x_ref[...] + _bdot(merged, wout_ref[...])
    x1_ref[...] = x1
    hm = _rms(x1, nffn_ref[...])
    hm_ref[...] = hm.astype(hm_ref.dtype)
    logits = _hdot(hm, wr_ref[...]) + br_ref[...]
    lane = lax.broadcasted_iota(jnp.int32, logits.shape, 1)
    work = logits
    top_v, picks = [], []
    for _ in range(TOP_K):
        m = jnp.max(work, axis=-1, keepdims=True)
        first = jnp.min(jnp.where(work == m, lane, N_EXPERTS), axis=-1, keepdims=True)
        pick = lane == first
        top_v.append(m)
        picks.append(pick)
        work = jnp.where(pick, -jnp.inf, work)
    ex = [jnp.exp(v - top_v[0]) for v in top_v]
    den = ex[0]
    for e in ex[1:]:
        den = den + e
    gates = jnp.zeros(logits.shape, F32)
    for pick, e in zip(picks, ex):
        gates = gates + jnp.where(pick, e / den, 0.0)
    gates_ref[...] = gates


def _mix(x, o, cv, ga, gb, wgo, wco, wout, norm_ffn, w_router, b_router, tm):
    n = x.shape[0]
    row = lambda w: pl.BlockSpec((tm, w), lambda i: (i, 0))
    return pl.pallas_call(
        _mix_kernel,
        grid=(n // tm,),
        in_specs=[row(D_MODEL), row(V_DIM), row(CONF_CH), row(D_MODEL), row(D_MODEL),
                  _const_spec(wgo.shape), _const_spec(wco.shape), _const_spec(wout.shape),
                  _const_spec((1, D_MODEL)), _const_spec(w_router.shape), _const_spec((1, N_EXPERTS))],
        out_specs=[row(D_MODEL), row(D_MODEL), row(N_EXPERTS)],
        out_shape=[jax.ShapeDtypeStruct((n, D_MODEL), F32), jax.ShapeDtypeStruct((n, D_MODEL), BF16),
                   jax.ShapeDtypeStruct((n, N_EXPERTS), F32)],
        compiler_params=_params(("parallel",)),
        name="mix",
    )(x, o, cv, ga, gb, wgo, wco, wout, norm_ffn, w_router, b_router)


def _moe_kernel(x1_ref, hm_ref, gates_ref, wgu_ref, bgu_ref, wd_ref, bd_ref, x2_ref, acc_ref):
    e = pl.program_id(1)

    @pl.when(e == 0)
    def _():
        acc_ref[...] = jnp.zeros(acc_ref.shape, F32)

    gates = gates_ref[...]
    lane = lax.broadcasted_iota(jnp.int32, gates.shape, 1)
    ge = jnp.sum(jnp.where(lane == e, gates, 0.0), axis=-1, keepdims=True)
    gu = _dot(hm_ref[...], wgu_ref[...]) + bgu_ref[...]
    gate = jnp.minimum(gu[:, :D_FF], SWIGLU_LIMIT)
    up = jnp.clip(gu[:, D_FF:], -SWIGLU_LIMIT, SWIGLU_LIMIT)
    hid = (up + 1.0) * (gate * _sigmoid(SWIGLU_ALPHA * gate))
    ye = _bdot(hid, wd_ref[...]) + bd_ref[...]
    acc_ref[...] += ge * ye

    @pl.when(e == pl.num_programs(1) - 1)
    def _():
        x2_ref[...] = x1_ref[...] + acc_ref[...]


def _moe(x1, hm, gates, wgu, bgu, wd, bd, tm):
    n = x1.shape[0]
    row = lambda w: pl.BlockSpec((tm, w), lambda i, e: (i, 0))
    return pl.pallas_call(
        _moe_kernel,
        grid=(n // tm, N_EXPERTS),
        in_specs=[row(D_MODEL), row(D_MODEL), row(N_EXPERTS),
                  pl.BlockSpec((None, D_MODEL, 2 * D_FF), lambda i, e: (e, 0, 0)),
                  pl.BlockSpec((None, 1, 2 * D_FF), lambda i, e: (e, 0, 0)),
                  pl.BlockSpec((None, D_FF, D_MODEL), lambda i, e: (e, 0, 0)),
                  pl.BlockSpec((None, 1, D_MODEL), lambda i, e: (e, 0, 0))],
        out_specs=row(D_MODEL),
        out_shape=jax.ShapeDtypeStruct((n, D_MODEL), F32),
        scratch_shapes=[pltpu.VMEM((tm, D_MODEL), F32)],
        compiler_params=_params(("parallel", "arbitrary")),
        name="moe",
    )(x1, hm, gates, wgu, bgu, wd, bd)


def _ple_kernel(x2_ref, p_ref, nple_ref, wg_ref, wp_ref, nfin_ref, y_ref):
    x2 = x2_ref[...]
    gate = _sigmoid(_bdot(_rms(x2, nple_ref[...]), wg_ref[...]))
    x3 = x2 + _bdot(p_ref[...], wp_ref[...]) * gate
    y_ref[...] = _rms(x3, nfin_ref[...])


def _ple(x2, p, norm_ple, wg, wp, norm_final, tm):
    n = x2.shape[0]
    row = lambda w: pl.BlockSpec((tm, w), lambda i: (i, 0))
    return pl.pallas_call(
        _ple_kernel,
        grid=(n // tm,),
        in_specs=[row(D_MODEL), row(PLE_DIM), _const_spec((1, D_MODEL)), _const_spec(wg.shape),
                  _const_spec(wp.shape), _const_spec((1, D_MODEL))],
        out_specs=row(D_MODEL),
        out_shape=jax.ShapeDtypeStruct((n, D_MODEL), F32),
        compiler_params=_params(("parallel",)),
        name="ple",
    )(x2, p, norm_ple, wg, wp, norm_final)


def kernel(x_prompt, x_sample, p_prompt, p_sample, state_gdn, state_qkv_conv, state_conf_conv, norm_mix, w_in, conv_qkv_w, a_log, dt_bias, gdn_norm, w_gdn_out, conf_dw_w, conf_dw_b, conf_ln_g, conf_ln_b, w_conf_out, w_out, norm_ffn, w_router, b_router, w_gate_up, b_gate_up, w_down, b_down, norm_ple, w_ple_gate, w_ple_proj, norm_final):
    assert w_in.shape[0] == 1, "single-layer trunk"
    bsz, seq, _ = x_prompt.shape
    n_p = bsz * seq
    n_s = x_sample.shape[0] * x_sample.shape[1]
    assert x_sample.shape[1] == 1

    w_in0 = w_in[0]
    wqkv = w_in0[:, :OFF_Z].astype(BF16)
    wz = w_in0[:, OFF_Z:OFF_BETA].astype(BF16)
    wba = w_in0[:, OFF_BETA:OFF_GLU].astype(BF16)
    wglu = w_in0[:, OFF_GLU:OFF_GATE].astype(BF16)
    wgate = w_in0[:, OFF_GATE:].astype(BF16)
    zeros_h = jnp.zeros((GDN_HEADS,), F32)
    alog16 = jnp.concatenate([zeros_h, a_log[0]]).reshape(1, 2 * GDN_HEADS)
    dtb16 = jnp.concatenate([zeros_h, dt_bias[0]]).reshape(1, 2 * GDN_HEADS)
    wgo = w_gdn_out[0].astype(BF16)
    wco = w_conf_out[0].astype(BF16)
    wout = w_out[0].astype(BF16)
    wpg = w_ple_gate[0].astype(BF16)
    wpp = w_ple_proj[0].astype(BF16)
    wgu = w_gate_up[0].astype(BF16)
    wd = w_down[0].astype(BF16)
    bgu = b_gate_up[0].reshape(N_EXPERTS, 1, 2 * D_FF)
    bd = b_down[0].reshape(N_EXPERTS, 1, D_MODEL)
    row = lambda v: v.reshape(1, -1)
    inproj_w = (row(norm_mix[0]), wqkv, wz, wba, wglu, wgate, alog16, dtb16)
    mix_w = (wgo, wco, wout, row(norm_ffn[0]), w_router[0], row(b_router[0]))
    ple_w = (row(norm_ple[0]), wpg, wpp, row(norm_final))
    conf_w = (conf_dw_w[0], row(conf_dw_b[0]), row(conf_ln_g[0]), row(conf_ln_b[0]))

    def tail(x, o, cv, ga, gb, p, tm):
        x1, hm, gates = _mix(x, o, cv, ga, gb, *mix_w, tm=tm)
        x2 = _moe(x1, hm, gates, wgu, bgu, wd, bd, tm=tm)
        return _ple(x2, p, *ple_w, tm=tm)

    xp = x_prompt.reshape(n_p, D_MODEL)
    qkv_p, z_p, bg_p, u_p, ga_p, gb_p = _inproj(xp, *inproj_w, tm=256)
    qkv_p3 = qkv_p.reshape(bsz, seq, QKV_CH)
    u_p3 = u_p.reshape(bsz, seq, CONF_CH)
    o_p, s_p = _gdn_prompt(qkv_p3, z_p.reshape(bsz, seq, V_DIM), bg_p, conv_qkv_w[0], row(gdn_norm[0]))
    cv_p = _conf_prompt(u_p3, *conf_w)
    y_p = tail(xp, o_p.reshape(n_p, V_DIM), cv_p.reshape(n_p, CONF_CH), ga_p, gb_p,
               p_prompt[0].reshape(n_p, PLE_DIM), tm=512)

    xs = x_sample.reshape(n_s, D_MODEL)
    qkv_s, z_s, bg_s, u_s, ga_s, gb_s = _inproj(xs, *inproj_w, tm=n_s)
    o_s, s_s = _gdn_sample(qkv_s, jnp.swapaxes(state_qkv_conv[0], 0, 1), z_s, bg_s, state_gdn[0],
                           conv_qkv_w[0], row(gdn_norm[0]))
    cv_s = _conf_sample(u_s, jnp.swapaxes(state_conf_conv[0], 0, 1), *conf_w)
    y_s = tail(xs, o_s, cv_s, ga_s, gb_s, p_sample[0].reshape(n_s, PLE_DIM), tm=n_s)

    new_qkv_s = jnp.concatenate([state_qkv_conv[0][:, 1:], qkv_s[:, None, :]], axis=1)
    new_conf_s = jnp.concatenate([state_conf_conv[0][:, 1:], u_s[:, None, :]], axis=1)
    return (y_p.reshape(bsz, seq, D_MODEL), y_s.reshape(n_s, 1, D_MODEL),
            s_p[None], qkv_p3[:, seq - (SHORT_CONV - 1):][None], u_p3[:, seq - (CONF_K - 1):][None],
            s_s[None], new_qkv_s[None], new_conf_s[None])
```

```python
import functools

import jax
import jax.numpy as jnp
from jax import lax
from jax.experimental import pallas as pl
from jax.experimental.pallas import tpu as pltpu

F32 = jnp.float32
BF16 = jnp.bfloat16

D_MODEL = 1024
GDN_HEADS = 8
GDN_DK = 128
GDN_DV = 128
QK_DIM = GDN_HEADS * GDN_DK
V_DIM = GDN_HEADS * GDN_DV
QKV_CH = 2 * QK_DIM + V_DIM
SHORT_CONV = 4
CHUNK = 64
CONF_CH = D_MODEL // 2
CONF_K = 31
N_EXPERTS = 32
TOP_K = 4
D_FF = D_MODEL
SWIGLU_LIMIT = 7.0
SWIGLU_ALPHA = 1.702
PLE_DIM = 256
NORM_EPS = 1e-6
LN_EPS = 1e-5

OFF_Z = QKV_CH
OFF_BETA = OFF_Z + V_DIM
OFF_GLU = OFF_BETA + 2 * GDN_HEADS
OFF_GATE = OFF_GLU + 2 * CONF_CH

VMEM_LIMIT_BYTES = 52 * 1024 * 1024
SUBLANES = 8
LANES = 128
CONF_HALO = 32


def _dot(a, b):
    return jnp.dot(a, b, preferred_element_type=F32)


def _bdot(a, b):
    return jnp.dot(a.astype(BF16), b.astype(BF16), preferred_element_type=F32)


def _hdot(a, b):
    return jnp.dot(a, b, preferred_element_type=F32, precision=lax.Precision.HIGHEST)


def _sigmoid(x):
    return 1.0 / (1.0 + jnp.exp(-x))


def _silu(x):
    return x * _sigmoid(x)


def _rms(x, g):
    return x * lax.rsqrt(jnp.mean(x * x, axis=-1, keepdims=True) + NORM_EPS) * g


def _const_spec(shape):
    nd = len(shape)
    return pl.BlockSpec(shape, lambda *_: (0,) * nd, pipeline_mode=pl.Buffered(1))


def _params(sem):
    return pltpu.CompilerParams(dimension_semantics=sem, vmem_limit_bytes=VMEM_LIMIT_BYTES)


def _inproj_kernel(x_ref, nw_ref, wqkv_ref, wz_ref, wba_ref, wglu_ref, wgate_ref, alog_ref, dtb_ref,
                   qkv_ref, z_ref, bg_ref, u_ref, ga_ref, gb_ref):
    a = _rms(x_ref[...], nw_ref[...]).astype(BF16)
    for c in range(QKV_CH // D_MODEL):
        cols = slice(c * D_MODEL, (c + 1) * D_MODEL)
        qkv_ref[:, cols] = _dot(a, wqkv_ref[:, cols])
    z_ref[...] = _dot(a, wz_ref[...])
    ba = _dot(a, wba_ref[...])
    lane = lax.broadcasted_iota(jnp.int32, ba.shape, 1)
    t = ba + dtb_ref[...]
    softplus = jnp.maximum(t, 0.0) + jnp.log(1.0 + jnp.exp(-jnp.abs(t)))
    bg_ref[...] = jnp.where(lane < GDN_HEADS, _sigmoid(ba), -jnp.exp(alog_ref[...]) * softplus)
    glu = _dot(a, wglu_ref[...])
    u_ref[...] = glu[:, :CONF_CH] * _sigmoid(glu[:, CONF_CH:])
    ga_ref[...] = _sigmoid(_dot(a, wgate_ref[:, :D_MODEL]))
    gb_ref[...] = _sigmoid(_dot(a, wgate_ref[:, D_MODEL:]))


def _inproj(x, norm_w, wqkv, wz, wba, wglu, wgate, alog16, dtb16, tm):
    n = x.shape[0]
    row = lambda w: pl.BlockSpec((tm, w), lambda i: (i, 0))
    return pl.pallas_call(
        _inproj_kernel,
        grid=(n // tm,),
        in_specs=[row(D_MODEL), _const_spec((1, D_MODEL)), _const_spec(wqkv.shape), _const_spec(wz.shape),
                  _const_spec(wba.shape), _const_spec(wglu.shape), _const_spec(wgate.shape),
                  _const_spec((1, 2 * GDN_HEADS)), _const_spec((1, 2 * GDN_HEADS))],
        out_specs=[row(QKV_CH), row(V_DIM), row(2 * GDN_HEADS), row(CONF_CH), row(D_MODEL), row(D_MODEL)],
        out_shape=[jax.ShapeDtypeStruct((n, w), F32)
                   for w in (QKV_CH, V_DIM, 2 * GDN_HEADS, CONF_CH, D_MODEL, D_MODEL)],
        compiler_params=_params(("parallel",)),
        name="inproj",
    )(x, norm_w, wqkv, wz, wba, wglu, wgate, alog16, dtb16)


GDN_GROUP = 4
GROUP_ROWS = GDN_GROUP * CHUNK
N_GROUPS = GDN_HEADS // GDN_GROUP


def _block_unit_lower_inverse(a, eye):
    b = -a
    t = eye + b
    p = _bdot(b, b)
    covered = 2
    while covered < CHUNK:
        covered *= 2
        if covered < CHUNK:
            r = _bdot(jnp.concatenate([t, p], axis=0), p)
            t = t + r[:GROUP_ROWS]
            p = r[GROUP_ROWS:]
        else:
            t = t + _bdot(t, p)
    return t


def _gdn_prep_kernel(qkv_ref, halo_ref, bg_ref, bgt_ref, cw_ref, wqg_ref, kdt_ref, qk_ref, u_ref, gl_ref, ext_ref):
    c = pl.program_id(1)
    ext_ref[0:SUBLANES, :] = jnp.where(c > 0, halo_ref[...], 0.0)
    ext_ref[SUBLANES:SUBLANES + CHUNK, :] = qkv_ref[...]

    def convsilu(c0):
        cols = slice(c0, c0 + LANES)
        acc = cw_ref[SHORT_CONV - 1:SHORT_CONV, cols] * ext_ref[SUBLANES:SUBLANES + CHUNK, cols]
        for j in range(SHORT_CONV - 1):
            r0 = SUBLANES - (SHORT_CONV - 1) + j
            acc = acc + cw_ref[j:j + 1, cols] * ext_ref[r0:r0 + CHUNK, cols]
        return _silu(acc)

    r64 = lax.broadcasted_iota(jnp.int32, (CHUNK, CHUNK), 0)
    c64 = lax.broadcasted_iota(jnp.int32, (CHUNK, CHUNK), 1)
    bg = bg_ref[...]
    gc_all = _hdot((r64 >= c64).astype(F32), bg)
    gct_all = _hdot(bgt_ref[...], (r64 <= c64).astype(F32))
    gl_ref[...] = gc_all[CHUNK - 1:CHUNK, :]

    row = lax.broadcasted_iota(jnp.int32, (GROUP_ROWS, GROUP_ROWS), 0)
    col = lax.broadcasted_iota(jnp.int32, (GROUP_ROWS, GROUP_ROWS), 1)
    blk = row - col + (col & (CHUNK - 1))
    same = (blk >= 0) & (blk < CHUNK)
    tril = same & (row >= col)
    strict = same & (row > col)
    eye = (row == col).astype(F32)
    nt = (((1,), (1,)), ((), ()))

    for g in range(N_GROUPS):
        heads = range(g * GDN_GROUP, (g + 1) * GDN_GROUP)
        stack = lambda f: jnp.concatenate([f(h) for h in heads], axis=0)
        q = stack(lambda h: convsilu(h * GDN_DK))
        k = stack(lambda h: convsilu(QK_DIM + h * GDN_DK))
        v = stack(lambda h: convsilu(2 * QK_DIM + h * GDN_DV))
        q = q * lax.rsqrt(jnp.sum(q * q, axis=-1, keepdims=True) + NORM_EPS) * (GDN_DK ** -0.5)
        k = k * lax.rsqrt(jnp.sum(k * k, axis=-1, keepdims=True) + NORM_EPS)
        beta = stack(lambda h: bg[:, h:h + 1])
        gc = stack(lambda h: gc_all[:, GDN_HEADS + h:GDN_HEADS + h + 1])
        g_last = stack(lambda h: jnp.broadcast_to(
            gc_all[CHUNK - 1:CHUNK, GDN_HEADS + h:GDN_HEADS + h + 1], (CHUNK, 1)))
        gct = jnp.concatenate([gct_all[GDN_HEADS + h:GDN_HEADS + h + 1, :] for h in heads], axis=1)
        decay = jnp.exp(jnp.where(tril, gc - gct, -jnp.inf))
        kb = k * beta
        k16 = k.astype(BF16)
        kk = lax.dot_general(kb.astype(BF16), k16, nt, preferred_element_type=F32)
        t_inv = _block_unit_lower_inverse(jnp.where(strict, kk * decay, 0.0), eye)
        eg = jnp.exp(gc)
        uw = _bdot(t_inv, jnp.concatenate([v * beta, kb * eg], axis=1))
        u_ref[g] = uw[:, :GDN_DV]
        w = uw[:, GDN_DV:].astype(BF16)
        qg = (q * eg).astype(BF16)
        for i, h in enumerate(heads):
            rows = slice(i * CHUNK, (i + 1) * CHUNK)
            wqg_ref[h, 0:CHUNK, :] = w[rows]
            wqg_ref[h, CHUNK:2 * CHUNK, :] = qg[rows]
        qk = lax.dot_general(q.astype(BF16), k16, nt, preferred_element_type=F32) * decay
        qk_ref[g] = qk.astype(BF16)
        kdt_ref[g] = (k * jnp.exp(g_last - gc)).T.astype(BF16)


def _gdn_scan_kernel(wqg_ref, kdt_ref, qk_ref, u_ref, gl_ref, z_ref, gn_ref, o_ref, s_out_ref, s_ref):
    c = pl.program_id(1)

    @pl.when(c == 0)
    def _():
        s_ref[...] = jnp.zeros(s_ref.shape, F32)

    dl_all = jnp.exp(gl_ref[...])
    zeros = jnp.zeros((CHUNK, GDN_DV), BF16)
    for g in range(N_GROUPS):
        heads = list(range(g * GDN_GROUP, (g + 1) * GDN_GROUP))
        v_new, q_s = [], []
        for i, h in enumerate(heads):
            r = _dot(wqg_ref[h], s_ref[h].astype(BF16))
            v_new.append((u_ref[g, i * CHUNK:(i + 1) * CHUNK, :] - r[:CHUNK]).astype(BF16))
            q_s.append(r[CHUNK:])
        o_all = jnp.concatenate(q_s, axis=0) + _dot(qk_ref[g], jnp.concatenate(v_new, axis=0))
        for i, h in enumerate(heads):
            vm = jnp.concatenate([v_new[j] if j == i else zeros for j in range(GDN_GROUP)], axis=0)
            s_ref[h] = s_ref[h] * dl_all[:, GDN_HEADS + h:GDN_HEADS + h + 1] + _dot(kdt_ref[g], vm)
            o = _rms(o_all[i * CHUNK:(i + 1) * CHUNK], gn_ref[...])
            cols = slice(h * GDN_DV, (h + 1) * GDN_DV)
            o_ref[:, cols] = (o * _silu(z_ref[:, cols])).astype(o_ref.dtype)

    @pl.when(c == pl.num_programs(1) - 1)
    def _():
        s_out_ref[...] = s_ref[...]


def _gdn_prompt(qkv, z, bg, conv_w, gdn_norm):
    b, l, _ = qkv.shape
    nc = l // CHUNK
    bg4 = bg.reshape(b, nc, CHUNK, 2 * GDN_HEADS)
    bgt4 = jnp.swapaxes(bg4, 2, 3)
    halo_blocks = CHUNK // SUBLANES
    seq = lambda w: pl.BlockSpec((None, CHUNK, w), lambda i, c: (i, c, 0))
    chunk = lambda *s: pl.BlockSpec((None, None) + s, lambda i, c: (i, c) + (0,) * len(s))
    inter_shapes = [((GDN_HEADS, 2 * CHUNK, GDN_DK), BF16), ((N_GROUPS, GDN_DK, GROUP_ROWS), BF16),
                    ((N_GROUPS, GROUP_ROWS, GROUP_ROWS), BF16), ((N_GROUPS, GROUP_ROWS, GDN_DV), F32),
                    ((1, 2 * GDN_HEADS), F32)]
    inter_specs = [chunk(*s) for s, _ in inter_shapes]
    wqg, kdt, qk, u, gl = pl.pallas_call(
        _gdn_prep_kernel,
        grid=(b, nc),
        in_specs=[seq(QKV_CH),
                  pl.BlockSpec((None, SUBLANES, QKV_CH), lambda i, c: (i, jnp.maximum(c * halo_blocks - 1, 0), 0)),
                  chunk(CHUNK, 2 * GDN_HEADS), chunk(2 * GDN_HEADS, CHUNK), _const_spec(conv_w.shape)],
        out_specs=inter_specs,
        out_shape=[jax.ShapeDtypeStruct((b, nc) + s, d) for s, d in inter_shapes],
        scratch_shapes=[pltpu.VMEM((SUBLANES + CHUNK, QKV_CH), F32)],
        compiler_params=_params(("parallel", "parallel")),
        name="gdn_prep",
    )(qkv, qkv, bg4, bgt4, conv_w)
    return pl.pallas_call(
        _gdn_scan_kernel,
        grid=(b, nc),
        in_specs=inter_specs + [seq(V_DIM), _const_spec((1, GDN_DV))],
        out_specs=[seq(V_DIM),
                   pl.BlockSpec((None, GDN_HEADS, GDN_DK, GDN_DV), lambda i, c: (i, 0, 0, 0))],
        out_shape=[jax.ShapeDtypeStruct((b, l, V_DIM), BF16),
                   jax.ShapeDtypeStruct((b, GDN_HEADS, GDN_DK, GDN_DV), F32)],
        scratch_shapes=[pltpu.VMEM((GDN_HEADS, GDN_DK, GDN_DV), F32)],
        compiler_params=_params(("parallel", "arbitrary")),
        name="gdn_scan",
    )(wqg, kdt, qk, u, gl, z, gdn_norm)


GDN_SAMPLE_ROWS = 8


def _gdn_sample_kernel(raw_ref, cs_ref, z_ref, bg_ref, s_in_ref, cw_ref, gn_ref, o_ref, s_out_ref):
    conv = cw_ref[SHORT_CONV - 1:SHORT_CONV, :] * raw_ref[...]
    for j in range(SHORT_CONV - 1):
        conv = conv + cw_ref[j:j + 1, :] * cs_ref[j]
    qkv = _silu(conv)
    bg = bg_ref[...]
    for h in range(GDN_HEADS):
        q = qkv[:, h * GDN_DK:(h + 1) * GDN_DK]
        k = qkv[:, QK_DIM + h * GDN_DK:QK_DIM + (h + 1) * GDN_DK]
        v = qkv[:, 2 * QK_DIM + h * GDN_DV:2 * QK_DIM + (h + 1) * GDN_DV]
        q = q * lax.rsqrt(jnp.sum(q * q, axis=-1, keepdims=True) + NORM_EPS) * (GDN_DK ** -0.5)
        k = k * lax.rsqrt(jnp.sum(k * k, axis=-1, keepdims=True) + NORM_EPS)
        beta = bg[:, h:h + 1]
        eg = jnp.exp(bg[:, GDN_HEADS + h:GDN_HEADS + h + 1])
        u = v * beta
        qk = jnp.sum(q * k, axis=-1, keepdims=True)
        w_t = (k * (beta * eg)).T
        qg_t = (q * eg).T
        k_t = k.T
        o_rows = []
        for r in range(GDN_SAMPLE_ROWS):
            s = s_in_ref[r, h]
            w_s = jnp.sum(w_t[:, r:r + 1] * s, axis=0, keepdims=True)
            q_s = jnp.sum(qg_t[:, r:r + 1] * s, axis=0, keepdims=True)
            v_new = u[r:r + 1] - w_s
            o_rows.append(q_s + qk[r:r + 1] * v_new)
            s_out_ref[r, h] = s * eg[r:r + 1] + k_t[:, r:r + 1] * v_new
        o = _rms(jnp.concatenate(o_rows, axis=0), gn_ref[...])
        cols = slice(h * GDN_DV, (h + 1) * GDN_DV)
        o_ref[:, cols] = (o * _silu(z_ref[:, cols])).astype(o_ref.dtype)


def _gdn_sample(raw, conv_state_t, z, bg, state, conv_w, gdn_norm):
    n = raw.shape[0]
    rb = GDN_SAMPLE_ROWS
    row = lambda w: pl.BlockSpec((rb, w), lambda i: (i, 0))
    st = pl.BlockSpec((rb, GDN_HEADS, GDN_DK, GDN_DV), lambda i: (i, 0, 0, 0))
    return pl.pallas_call(
        _gdn_sample_kernel,
        grid=(n // rb,),
        in_specs=[row(QKV_CH), pl.BlockSpec((SHORT_CONV - 1, rb, QKV_CH), lambda i: (0, i, 0)),
                  row(V_DIM), row(2 * GDN_HEADS), st, _const_spec(conv_w.shape), _const_spec((1, GDN_DV))],
        out_specs=[row(V_DIM), st],
        out_shape=[jax.ShapeDtypeStruct((n, V_DIM), BF16), jax.ShapeDtypeStruct(state.shape, F32)],
        compiler_params=_params(("parallel",)),
        name="gdn_sample",
    )(raw, conv_state_t, z, bg, state, conv_w, gdn_norm)


CONF_TILE = 256
CONF_ROWS = 64


def _ln_silu(x, g, b):
    mu = jnp.mean(x, axis=-1, keepdims=True)
    xc = x - mu
    var = jnp.mean(xc * xc, axis=-1, keepdims=True)
    return _silu(xc * lax.rsqrt(var + LN_EPS) * g + b)


def _conf_prompt_kernel(u_ref, w_ref, b_ref, lng_ref, lnb_ref, cv_ref, ubuf_ref, acc_ref):
    @pl.when(pl.program_id(1) == 0)
    def _():
        ubuf_ref[0:CONF_HALO, :] = jnp.zeros((CONF_HALO, CONF_CH), F32)

    ubuf_ref[CONF_HALO:CONF_HALO + CONF_TILE, :] = u_ref[...]
    base = CONF_HALO - (CONF_K - 1)
    for rb in range(CONF_TILE // CONF_ROWS):
        for cb in range(CONF_CH // LANES):
            cols = slice(cb * LANES, (cb + 1) * LANES)
            r0 = base + rb * CONF_ROWS
            acc = w_ref[0:1, cols] * ubuf_ref[r0:r0 + CONF_ROWS, cols]
            for j in range(1, CONF_K):
                acc = acc + w_ref[j:j + 1, cols] * ubuf_ref[r0 + j:r0 + j + CONF_ROWS, cols]
            acc_ref[rb * CONF_ROWS:(rb + 1) * CONF_ROWS, cols] = acc + b_ref[:, cols]
    ubuf_ref[0:CONF_HALO, :] = ubuf_ref[CONF_TILE:CONF_TILE + CONF_HALO, :]
    cv_ref[...] = _ln_silu(acc_ref[...], lng_ref[...], lnb_ref[...]).astype(cv_ref.dtype)


def _conf_prompt(u, w, b, ln_g, ln_b):
    bsz, l, _ = u.shape
    seq = pl.BlockSpec((None, CONF_TILE, CONF_CH), lambda i, t: (i, t, 0))
    return pl.pallas_call(
        _conf_prompt_kernel,
        grid=(bsz, l // CONF_TILE),
        in_specs=[seq, _const_spec(w.shape), _const_spec((1, CONF_CH)), _const_spec((1, CONF_CH)),
                  _const_spec((1, CONF_CH))],
        out_specs=seq,
        out_shape=jax.ShapeDtypeStruct((bsz, l, CONF_CH), BF16),
        scratch_shapes=[pltpu.VMEM((CONF_HALO + CONF_TILE, CONF_CH), F32), pltpu.VMEM((CONF_TILE, CONF_CH), F32)],
        compiler_params=_params(("parallel", "arbitrary")),
        name="conf_prompt",
    )(u, w, b, ln_g, ln_b)


def _conf_sample_kernel(u_ref, cs_ref, w_ref, b_ref, lng_ref, lnb_ref, cv_ref):
    acc = w_ref[CONF_K - 1:CONF_K, :] * u_ref[...] + b_ref[...]
    for j in range(CONF_K - 1):
        acc = acc + w_ref[j:j + 1, :] * cs_ref[j]
    cv_ref[...] = _ln_silu(acc, lng_ref[...], lnb_ref[...]).astype(cv_ref.dtype)


def _conf_sample(u, conf_state_t, w, b, ln_g, ln_b):
    n = u.shape[0]
    return pl.pallas_call(
        _conf_sample_kernel,
        out_shape=jax.ShapeDtypeStruct((n, CONF_CH), BF16),
        compiler_params=pltpu.CompilerParams(vmem_limit_bytes=VMEM_LIMIT_BYTES),
        name="conf_sample",
    )(u, conf_state_t, w, b, ln_g, ln_b)


MOE_TILE = 256
SLAB = D_MODEL // LANES


def _mix_kernel(x_ref, o_ref, cv_ref, ga_ref, gb_ref, wgo_ref, wco_ref, wout_ref, nffn_ref, wr_ref, br_ref,
                x1_ref, hm3_ref, topi_ref, topw_ref):
    tm = x_ref.shape[0]
    branch_a = _dot(o_ref[...], wgo_ref[...])
    branch_b = _dot(cv_ref[...], wco_ref[...])
    merged = ga_ref[...] * branch_a + gb_ref[...] * branch_b
    x1 = x_ref[...] + _bdot(merged, wout_ref[...])
    x1_ref[...] = x1
    hm = _rms(x1, nffn_ref[...])
    for s in range(SLAB):
        hm3_ref[pl.ds(s, tm, stride=SLAB), :] = hm[:, s * LANES:(s + 1) * LANES]
    logits = _hdot(hm, wr_ref[...]) + br_ref[...]
    lane = lax.broadcasted_iota(jnp.int32, logits.shape, 1)
    work = logits
    top_v = []
    top_i = jnp.zeros(logits.shape, jnp.int32)
    for k in range(TOP_K):
        m = jnp.max(work, axis=-1, keepdims=True)
        first = jnp.min(jnp.where(work == m, lane, N_EXPERTS), axis=-1, keepdims=True)
        top_v.append(m)
        top_i = jnp.where(lane == k, first, top_i)
        work = jnp.where(lane == first, -jnp.inf, work)
    ex = [jnp.exp(v - top_v[0]) for v in top_v]
    den = ex[0]
    for e in ex[1:]:
        den = den + e
    top_w = jnp.zeros(logits.shape, F32)
    for k, e in enumerate(ex):
        top_w = jnp.where(lane == k, e / den, top_w)
    topi_ref[...] = top_i
    topw_ref[...] = top_w


def _mix(x, o, cv, ga, gb, wgo, wco, wout, norm_ffn, w_router, b_router, tm):
    n = x.shape[0]
    row = lambda w: pl.BlockSpec((tm, w), lambda i: (i, 0))
    return pl.pallas_call(
        _mix_kernel,
        grid=(n // tm,),
        in_specs=[row(D_MODEL), row(V_DIM), row(CONF_CH), row(D_MODEL), row(D_MODEL),
                  _const_spec(wgo.shape), _const_spec(wco.shape), _const_spec(wout.shape),
                  _const_spec((1, D_MODEL)), _const_spec(w_router.shape), _const_spec((1, N_EXPERTS))],
        out_specs=[row(D_MODEL), pl.BlockSpec((tm * SLAB, LANES), lambda i: (i, 0)), row(N_EXPERTS), row(N_EXPERTS)],
        out_shape=[jax.ShapeDtypeStruct((n, D_MODEL), F32), jax.ShapeDtypeStruct((n * SLAB, LANES), F32),
                   jax.ShapeDtypeStruct((n, N_EXPERTS), jnp.int32), jax.ShapeDtypeStruct((n, N_EXPERTS), F32)],
        compiler_params=_params(("parallel",)),
        name="mix",
    )(x, o, cv, ga, gb, wgo, wco, wout, norm_ffn, w_router, b_router)


def _route(top_i):
    n = top_i.shape[0]
    pairs = n * TOP_K
    n_tiles = pairs // MOE_TILE + N_EXPERTS
    e_flat = top_i.reshape(pairs)
    onehot = (e_flat[:, None] == jnp.arange(N_EXPERTS, dtype=jnp.int32)[None, :]).astype(jnp.int32)
    csum = jnp.cumsum(onehot, axis=0)
    rank = jnp.take_along_axis(csum, e_flat[:, None], axis=1)[:, 0] - 1
    tiles_per = (csum[-1] + MOE_TILE - 1) // MOE_TILE
    tile_end = jnp.cumsum(tiles_per)
    n_used = tile_end[-1]
    pos = (tile_end - tiles_per)[e_flat] * MOE_TILE + rank
    pair_id = jnp.arange(pairs, dtype=jnp.int32)
    rows = jnp.arange(n_tiles * MOE_TILE, dtype=jnp.int32)
    trash = pairs + ((rows // MOE_TILE) % 2) * MOE_TILE + rows % MOE_TILE
    row_token = jnp.zeros((n_tiles * MOE_TILE,), jnp.int32).at[pos].set(pair_id // TOP_K)
    row_dest = trash.at[pos].set(pair_id) * SLAB
    t = jnp.minimum(jnp.arange(n_tiles, dtype=jnp.int32), n_used - 1)
    tile_expert = jnp.sum((tile_end[None, :] <= t[:, None]).astype(jnp.int32), axis=1)
    shape3 = (n_tiles, 1, MOE_TILE)
    return tile_expert, n_used.reshape(1), row_token.reshape(shape3), row_dest.reshape(shape3)


def _experts_kernel(te_ref, nu_ref, tok0_ref, tokn_ref, dst_ref, x3_hbm, wgu_ref, bgu_ref, wd_ref, bd_ref,
                    y_hbm, xbuf, ybuf, wgu16, wd16, sem_in, sem_out):
    t = pl.program_id(0)
    n_used = nu_ref[0]
    slot = t % 2
    groups = MOE_TILE // SUBLANES
    tile_rows = MOE_TILE * SLAB
    pair_rows = y_hbm.shape[0] - 2 * tile_rows

    @pl.when(t == 0)
    def _():
        ybuf[...] = jnp.zeros(ybuf.shape, F32)
        for s in range(2):
            fill = pltpu.make_async_copy(ybuf.at[s], y_hbm.at[pl.ds(pair_rows + s * tile_rows, tile_rows), :],
                                         sem_out.at[s])
            fill.start()
            fill.wait()

    def gather(tok_ref, s):
        def body(i8, carry):
            for j in range(SUBLANES):
                i = i8 * SUBLANES + j
                pltpu.make_async_copy(x3_hbm.at[tok_ref[0, i]],
                                      xbuf.at[s, pl.ds(pl.multiple_of(i * SLAB, SLAB), SLAB), :],
                                      sem_in.at[s]).start()
            return carry
        lax.fori_loop(0, groups, body, 0)

    def scatter(s):
        def body(i8, carry):
            for j in range(SUBLANES):
                i = i8 * SUBLANES + j
                pltpu.make_async_copy(ybuf.at[s, pl.ds(pl.multiple_of(i * SLAB, SLAB), SLAB), :],
                                      y_hbm.at[pl.ds(pl.multiple_of(dst_ref[0, i], SLAB), SLAB), :],
                                      sem_out.at[s]).start()
            return carry
        lax.fori_loop(0, groups, body, 0)

    def wait_all(buf, sem, s):
        pltpu.make_async_copy(buf.at[s], buf.at[s], sem.at[s]).wait()

    @pl.when(t < n_used)
    def _():
        @pl.when(t == 0)
        def _():
            gather(tok0_ref, 0)

        @pl.when(t + 1 < n_used)
        def _():
            gather(tokn_ref, 1 - slot)

        @pl.when((t == 0) | (te_ref[t] != te_ref[jnp.maximum(t - 1, 0)]))
        def _():
            wgu16[...] = wgu_ref[...].astype(BF16)
            wd16[...] = wd_ref[...].astype(BF16)

        wait_all(xbuf, sem_in, slot)
        x = jnp.concatenate([xbuf[slot, pl.ds(s, MOE_TILE, stride=SLAB), :] for s in range(SLAB)], axis=1)
        gu = _dot(x.astype(BF16), wgu16[...]) + bgu_ref[...]
        gate = jnp.minimum(gu[:, :D_FF], SWIGLU_LIMIT)
        up = jnp.clip(gu[:, D_FF:], -SWIGLU_LIMIT, SWIGLU_LIMIT)
        hid = (up + 1.0) * (gate * _sigmoid(SWIGLU_ALPHA * gate))
        ye = _dot(hid.astype(BF16), wd16[...]) + bd_ref[...]

        @pl.when(t >= 2)
        def _():
            wait_all(ybuf, sem_out, slot)

        for s in range(SLAB):
            ybuf[slot, pl.ds(s, MOE_TILE, stride=SLAB), :] = ye[:, s * LANES:(s + 1) * LANES]
        scatter(slot)

        @pl.when(t == n_used - 1)
        def _():
            wait_all(ybuf, sem_out, slot)

            @pl.when(t >= 1)
            def _():
                wait_all(ybuf, sem_out, 1 - slot)


def _experts(hm3, tile_expert, n_used, row_token, row_dest, wgu, bgu, wd, bd):
    n_tiles = row_token.shape[0]
    pairs = hm3.shape[0] // SLAB * TOP_K
    x3 = hm3.reshape(-1, SLAB, LANES)
    idx = lambda f: pl.BlockSpec((None, 1, MOE_TILE), f, memory_space=pltpu.SMEM)
    grid_spec = pltpu.PrefetchScalarGridSpec(
        num_scalar_prefetch=2,
        grid=(n_tiles,),
        in_specs=[idx(lambda t, te, nu: (0, 0, 0)),
                  idx(lambda t, te, nu: (jnp.minimum(t + 1, n_tiles - 1), 0, 0)),
                  idx(lambda t, te, nu: (t, 0, 0)),
                  pl.BlockSpec(memory_space=pl.ANY),
                  pl.BlockSpec((None, D_MODEL, 2 * D_FF), lambda t, te, nu: (te[t], 0, 0)),
                  pl.BlockSpec((None, 1, 2 * D_FF), lambda t, te, nu: (te[t], 0, 0)),
                  pl.BlockSpec((None, D_FF, D_MODEL), lambda t, te, nu: (te[t], 0, 0)),
                  pl.BlockSpec((None, 1, D_MODEL), lambda t, te, nu: (te[t], 0, 0))],
        out_specs=pl.BlockSpec(memory_space=pl.ANY),
        scratch_shapes=[pltpu.VMEM((2, MOE_TILE * SLAB, LANES), F32), pltpu.VMEM((2, MOE_TILE * SLAB, LANES), F32),
                        pltpu.VMEM((D_MODEL, 2 * D_FF), BF16), pltpu.VMEM((D_FF, D_MODEL), BF16),
                        pltpu.SemaphoreType.DMA((2,)), pltpu.SemaphoreType.DMA((2,))])
    return pl.pallas_call(
        _experts_kernel,
        grid_spec=grid_spec,
        out_shape=jax.ShapeDtypeStruct(((pairs + 2 * MOE_TILE) * SLAB, LANES), F32),
        compiler_params=_params(("arbitrary",)),
        name="experts",
    )(tile_expert, n_used, row_token, row_token, row_dest, x3, wgu, bgu, wd, bd)


def _ple_kernel(x1_ref, y_ref, topw_ref, p_ref, nple_ref, wg_ref, wp_ref, nfin_ref, out_ref):
    tm = x1_ref.shape[0]
    topw = topw_ref[...]
    moe = []
    for s in range(SLAB):
        acc = topw[:, 0:1] * y_ref[pl.ds(s, tm, stride=TOP_K * SLAB), :]
        for k in range(1, TOP_K):
            acc = acc + topw[:, k:k + 1] * y_ref[pl.ds(k * SLAB + s, tm, stride=TOP_K * SLAB), :]
        moe.append(acc)
    x2 = x1_ref[...] + jnp.concatenate(moe, axis=1)
    gate = _sigmoid(_bdot(_rms(x2, nple_ref[...]), wg_ref[...]))
    x3 = x2 + _bdot(p_ref[...], wp_ref[...]) * gate
    out_ref[...] = _rms(x3, nfin_ref[...])


def _ple(x1, y2d, topw, p, norm_ple, wg, wp, norm_final, tm, row_offset):
    n = x1.shape[0]
    off = row_offset // tm
    row = lambda w: pl.BlockSpec((tm, w), lambda i: (i, 0))
    return pl.pallas_call(
        _ple_kernel,
        grid=(n // tm,),
        in_specs=[row(D_MODEL), pl.BlockSpec((tm * TOP_K * SLAB, LANES), lambda i: (i + off, 0)), row(N_EXPERTS),
                  row(PLE_DIM), _const_spec((1, D_MODEL)), _const_spec(wg.shape), _const_spec(wp.shape),
                  _const_spec((1, D_MODEL))],
        out_specs=row(D_MODEL),
        out_shape=jax.ShapeDtypeStruct((n, D_MODEL), F32),
        compiler_params=_params(("parallel",)),
        name="ple",
    )(x1, y2d, topw, p, norm_ple, wg, wp, norm_final)


def kernel(x_prompt, x_sample, p_prompt, p_sample, state_gdn, state_qkv_conv, state_conf_conv, norm_mix, w_in, conv_qkv_w, a_log, dt_bias, gdn_norm, w_gdn_out, conf_dw_w, conf_dw_b, conf_ln_g, conf_ln_b, w_conf_out, w_out, norm_ffn, w_router, b_router, w_gate_up, b_gate_up, w_down, b_down, norm_ple, w_ple_gate, w_ple_proj, norm_final):
    assert w_in.shape[0] == 1, "single-layer trunk"
    bsz, seq, _ = x_prompt.shape
    n_p = bsz * seq
    n_s = x_sample.shape[0] * x_sample.shape[1]
    assert x_sample.shape[1] == 1

    w_in0 = w_in[0]
    wqkv = w_in0[:, :OFF_Z].astype(BF16)
    wz = w_in0[:, OFF_Z:OFF_BETA].astype(BF16)
    wba = w_in0[:, OFF_BETA:OFF_GLU].astype(BF16)
    wglu = w_in0[:, OFF_GLU:OFF_GATE].astype(BF16)
    wgate = w_in0[:, OFF_GATE:].astype(BF16)
    zeros_h = jnp.zeros((GDN_HEADS,), F32)
    alog16 = jnp.concatenate([zeros_h, a_log[0]]).reshape(1, 2 * GDN_HEADS)
    dtb16 = jnp.concatenate([zeros_h, dt_bias[0]]).reshape(1, 2 * GDN_HEADS)
    wgo = w_gdn_out[0].astype(BF16)
    wco = w_conf_out[0].astype(BF16)
    wout = w_out[0].astype(BF16)
    wpg = w_ple_gate[0].astype(BF16)
    wpp = w_ple_proj[0].astype(BF16)
    bgu = b_gate_up[0].reshape(N_EXPERTS, 1, 2 * D_FF)
    bd = b_down[0].reshape(N_EXPERTS, 1, D_MODEL)
    row = lambda v: v.reshape(1, -1)
    inproj_w = (row(norm_mix[0]), wqkv, wz, wba, wglu, wgate, alog16, dtb16)
    mix_w = (wgo, wco, wout, row(norm_ffn[0]), w_router[0], row(b_router[0]))
    ple_w = (row(norm_ple[0]), wpg, wpp, row(norm_final))
    conf_w = (conf_dw_w[0], row(conf_dw_b[0]), row(conf_ln_g[0]), row(conf_ln_b[0]))

    xp = x_prompt.reshape(n_p, D_MODEL)
    qkv_p, z_p, bg_p, u_p, ga_p, gb_p = _inproj(xp, *inproj_w, tm=256)
    qkv_p3 = qkv_p.reshape(bsz, seq, QKV_CH)
    u_p3 = u_p.reshape(bsz, seq, CONF_CH)
    o_p, s_p = _gdn_prompt(qkv_p3, z_p.reshape(bsz, seq, V_DIM), bg_p, conv_qkv_w[0], row(gdn_norm[0]))
    cv_p = _conf_prompt(u_p3, *conf_w)

    xs = x_sample.reshape(n_s, D_MODEL)
    qkv_s, z_s, bg_s, u_s, ga_s, gb_s = _inproj(xs, *inproj_w, tm=n_s)
    o_s, s_s = _gdn_sample(qkv_s, jnp.swapaxes(state_qkv_conv[0], 0, 1), z_s, bg_s, state_gdn[0],
                           conv_qkv_w[0], row(gdn_norm[0]))
    cv_s = _conf_sample(u_s, jnp.swapaxes(state_conf_conv[0], 0, 1), *conf_w)

    tm_p = 512
    x1_p, hm3_p, ti_p, tw_p = _mix(xp, o_p.reshape(n_p, V_DIM), cv_p.reshape(n_p, CONF_CH), ga_p, gb_p, *mix_w,
                                   tm=tm_p)
    x1_s, hm3_s, ti_s, tw_s = _mix(xs, o_s, cv_s, ga_s, gb_s, *mix_w, tm=n_s)
    routing = _route(jnp.concatenate([ti_p, ti_s], axis=0)[:, :TOP_K])
    y2d = _experts(jnp.concatenate([hm3_p, hm3_s], axis=0), *routing, w_gate_up[0], bgu, w_down[0], bd)
    y_p = _ple(x1_p, y2d, tw_p, p_prompt[0].reshape(n_p, PLE_DIM), *ple_w, tm=tm_p, row_offset=0)
    y_s = _ple(x1_s, y2d, tw_s, p_sample[0].reshape(n_s, PLE_DIM), *ple_w, tm=n_s, row_offset=n_p)

    new_qkv_s = jnp.concatenate([state_qkv_conv[0][:, 1:], qkv_s[:, None, :]], axis=1)
    new_conf_s = jnp.concatenate([state_conf_conv[0][:, 1:], u_s[:, None, :]], axis=1)
    return (y_p.reshape(bsz, seq, D_MODEL), y_s.reshape(n_s, 1, D_MODEL),
            s_p[None], qkv_p3[:, seq - (SHORT_CONV - 1):][None], u_p3[:, seq - (CONF_K - 1):][None],
            s_s[None], new_qkv_s[None], new_conf_s[None])
```

```python
import functools

import jax
import jax.numpy as jnp
from jax import lax
from jax.experimental import pallas as pl
from jax.experimental.pallas import tpu as pltpu

F32 = jnp.float32
BF16 = jnp.bfloat16

D_MODEL = 1024
GDN_HEADS = 8
GDN_DK = 128
GDN_DV = 128
QK_DIM = GDN_HEADS * GDN_DK
V_DIM = GDN_HEADS * GDN_DV
QKV_CH = 2 * QK_DIM + V_DIM
SHORT_CONV = 4
CHUNK = 64
CONF_CH = D_MODEL // 2
CONF_K = 31
N_EXPERTS = 32
TOP_K = 4
D_FF = D_MODEL
SWIGLU_LIMIT = 7.0
SWIGLU_ALPHA = 1.702
PLE_DIM = 256
NORM_EPS = 1e-6
LN_EPS = 1e-5

OFF_Z = QKV_CH
OFF_BETA = OFF_Z + V_DIM
OFF_GLU = OFF_BETA + 2 * GDN_HEADS
OFF_GATE = OFF_GLU + 2 * CONF_CH

VMEM_LIMIT_BYTES = 52 * 1024 * 1024
SUBLANES = 8
LANES = 128
CONF_HALO = 32


def _dot(a, b):
    return jnp.dot(a, b, preferred_element_type=F32)


def _bdot(a, b):
    return jnp.dot(a.astype(BF16), b.astype(BF16), preferred_element_type=F32)


def _hdot(a, b):
    return jnp.dot(a, b, preferred_element_type=F32, precision=lax.Precision.HIGHEST)


def _sigmoid(x):
    return 1.0 / (1.0 + jnp.exp(-x))


def _silu(x):
    return x * _sigmoid(x)


def _rms(x, g):
    return x * lax.rsqrt(jnp.mean(x * x, axis=-1, keepdims=True) + NORM_EPS) * g


def _const_spec(shape):
    nd = len(shape)
    return pl.BlockSpec(shape, lambda *_: (0,) * nd, pipeline_mode=pl.Buffered(1))


def _params(sem):
    return pltpu.CompilerParams(dimension_semantics=sem, vmem_limit_bytes=VMEM_LIMIT_BYTES)


def _inproj_kernel(x_ref, nw_ref, wqkv_ref, wz_ref, wba_ref, wglu_ref, wgate_ref, alog_ref, dtb_ref,
                   qkv_ref, z_ref, bg_ref, u_ref, ga_ref, gb_ref):
    a = _rms(x_ref[...], nw_ref[...]).astype(BF16)
    for c in range(QKV_CH // D_MODEL):
        cols = slice(c * D_MODEL, (c + 1) * D_MODEL)
        qkv_ref[:, cols] = _dot(a, wqkv_ref[:, cols])
    z_ref[...] = _dot(a, wz_ref[...])
    ba = _dot(a, wba_ref[...])
    lane = lax.broadcasted_iota(jnp.int32, ba.shape, 1)
    t = ba + dtb_ref[...]
    softplus = jnp.maximum(t, 0.0) + jnp.log(1.0 + jnp.exp(-jnp.abs(t)))
    bg_ref[...] = jnp.where(lane < GDN_HEADS, _sigmoid(ba), -jnp.exp(alog_ref[...]) * softplus)
    glu = _dot(a, wglu_ref[...])
    u_ref[...] = glu[:, :CONF_CH] * _sigmoid(glu[:, CONF_CH:])
    ga_ref[...] = _sigmoid(_dot(a, wgate_ref[:, :D_MODEL]))
    gb_ref[...] = _sigmoid(_dot(a, wgate_ref[:, D_MODEL:]))


def _inproj(x, norm_w, wqkv, wz, wba, wglu, wgate, alog16, dtb16, tm):
    n = x.shape[0]
    row = lambda w: pl.BlockSpec((tm, w), lambda i: (i, 0))
    return pl.pallas_call(
        _inproj_kernel,
        grid=(n // tm,),
        in_specs=[row(D_MODEL), _const_spec((1, D_MODEL)), _const_spec(wqkv.shape), _const_spec(wz.shape),
                  _const_spec(wba.shape), _const_spec(wglu.shape), _const_spec(wgate.shape),
                  _const_spec((1, 2 * GDN_HEADS)), _const_spec((1, 2 * GDN_HEADS))],
        out_specs=[row(QKV_CH), row(V_DIM), row(2 * GDN_HEADS), row(CONF_CH), row(D_MODEL), row(D_MODEL)],
        out_shape=[jax.ShapeDtypeStruct((n, w), F32)
                   for w in (QKV_CH, V_DIM, 2 * GDN_HEADS, CONF_CH, D_MODEL, D_MODEL)],
        compiler_params=_params(("parallel",)),
        name="inproj",
    )(x, norm_w, wqkv, wz, wba, wglu, wgate, alog16, dtb16)


GDN_GROUP = 4
GROUP_ROWS = GDN_GROUP * CHUNK
N_GROUPS = GDN_HEADS // GDN_GROUP


def _block_unit_lower_inverse(a, eye):
    b = -a
    t = eye + b
    p = _bdot(b, b)
    covered = 2
    while covered < CHUNK:
        covered *= 2
        if covered < CHUNK:
            r = _bdot(jnp.concatenate([t, p], axis=0), p)
            t = t + r[:GROUP_ROWS]
            p = r[GROUP_ROWS:]
        else:
            t = t + _bdot(t, p)
    return t


def _gdn_prep_kernel(qkv_ref, halo_ref, bg_ref, bgt_ref, cw_ref, wqg_ref, kdt_ref, qk_ref, u_ref, gl_ref, ext_ref):
    c = pl.program_id(1)
    ext_ref[0:SUBLANES, :] = jnp.where(c > 0, halo_ref[...], 0.0)
    ext_ref[SUBLANES:SUBLANES + CHUNK, :] = qkv_ref[...]

    def convsilu(c0):
        cols = slice(c0, c0 + LANES)
        acc = cw_ref[SHORT_CONV - 1:SHORT_CONV, cols] * ext_ref[SUBLANES:SUBLANES + CHUNK, cols]
        for j in range(SHORT_CONV - 1):
            r0 = SUBLANES - (SHORT_CONV - 1) + j
            acc = acc + cw_ref[j:j + 1, cols] * ext_ref[r0:r0 + CHUNK, cols]
        return _silu(acc)

    r64 = lax.broadcasted_iota(jnp.int32, (CHUNK, CHUNK), 0)
    c64 = lax.broadcasted_iota(jnp.int32, (CHUNK, CHUNK), 1)
    bg = bg_ref[...]
    gc_all = _hdot((r64 >= c64).astype(F32), bg)
    gct_all = _hdot(bgt_ref[...], (r64 <= c64).astype(F32))
    gl_ref[...] = gc_all[CHUNK - 1:CHUNK, :]

    row = lax.broadcasted_iota(jnp.int32, (GROUP_ROWS, GROUP_ROWS), 0)
    col = lax.broadcasted_iota(jnp.int32, (GROUP_ROWS, GROUP_ROWS), 1)
    blk = row - col + (col & (CHUNK - 1))
    same = (blk >= 0) & (blk < CHUNK)
    tril = same & (row >= col)
    strict = same & (row > col)
    eye = (row == col).astype(F32)
    nt = (((1,), (1,)), ((), ()))

    for g in range(N_GROUPS):
        heads = range(g * GDN_GROUP, (g + 1) * GDN_GROUP)
        stack = lambda f: jnp.concatenate([f(h) for h in heads], axis=0)
        q = stack(lambda h: convsilu(h * GDN_DK))
        k = stack(lambda h: convsilu(QK_DIM + h * GDN_DK))
        v = stack(lambda h: convsilu(2 * QK_DIM + h * GDN_DV))
        q = q * lax.rsqrt(jnp.sum(q * q, axis=-1, keepdims=True) + NORM_EPS) * (GDN_DK ** -0.5)
        k = k * lax.rsqrt(jnp.sum(k * k, axis=-1, keepdims=True) + NORM_EPS)
        beta = stack(lambda h: bg[:, h:h + 1])
        gc = stack(lambda h: gc_all[:, GDN_HEADS + h:GDN_HEADS + h + 1])
        g_last = stack(lambda h: jnp.broadcast_to(
            gc_all[CHUNK - 1:CHUNK, GDN_HEADS + h:GDN_HEADS + h + 1], (CHUNK, 1)))
        gct = jnp.concatenate([gct_all[GDN_HEADS + h:GDN_HEADS + h + 1, :] for h in heads], axis=1)
        decay = jnp.exp(jnp.where(tril, gc - gct, -jnp.inf))
        kb = k * beta
        k16 = k.astype(BF16)
        kk = lax.dot_general(kb.astype(BF16), k16, nt, preferred_element_type=F32)
        t_inv = _block_unit_lower_inverse(jnp.where(strict, kk * decay, 0.0), eye)
        eg = jnp.exp(gc)
        uw = _bdot(t_inv, jnp.concatenate([v * beta, kb * eg], axis=1))
        u_ref[g] = uw[:, :GDN_DV]
        w = uw[:, GDN_DV:].astype(BF16)
        qg = (q * eg).astype(BF16)
        for i, h in enumerate(heads):
            rows = slice(i * CHUNK, (i + 1) * CHUNK)
            wqg_ref[h, 0:CHUNK, :] = w[rows]
            wqg_ref[h, CHUNK:2 * CHUNK, :] = qg[rows]
        qk = lax.dot_general(q.astype(BF16), k16, nt, preferred_element_type=F32) * decay
        qk_ref[g] = qk.astype(BF16)
        kdt_ref[g] = (k * jnp.exp(g_last - gc)).T.astype(BF16)


def _gdn_scan_kernel(wqg_ref, kdt_ref, qk_ref, u_ref, gl_ref, z_ref, gn_ref, o_ref, s_out_ref, s_ref):
    c = pl.program_id(1)

    @pl.when(c == 0)
    def _():
        s_ref[...] = jnp.zeros(s_ref.shape, F32)

    dl_all = jnp.exp(gl_ref[...])
    zeros = jnp.zeros((CHUNK, GDN_DV), BF16)
    for g in range(N_GROUPS):
        heads = list(range(g * GDN_GROUP, (g + 1) * GDN_GROUP))
        v_new, q_s = [], []
        for i, h in enumerate(heads):
            r = _dot(wqg_ref[h], s_ref[h].astype(BF16))
            v_new.append((u_ref[g, i * CHUNK:(i + 1) * CHUNK, :] - r[:CHUNK]).astype(BF16))
            q_s.append(r[CHUNK:])
        o_all = jnp.concatenate(q_s, axis=0) + _dot(qk_ref[g], jnp.concatenate(v_new, axis=0))
        for i, h in enumerate(heads):
            vm = jnp.concatenate([v_new[j] if j == i else zeros for j in range(GDN_GROUP)], axis=0)
            s_ref[h] = s_ref[h] * dl_all[:, GDN_HEADS + h:GDN_HEADS + h + 1] + _dot(kdt_ref[g], vm)
            o = _rms(o_all[i * CHUNK:(i + 1) * CHUNK], gn_ref[...])
            cols = slice(h * GDN_DV, (h + 1) * GDN_DV)
            o_ref[:, cols] = (o * _silu(z_ref[:, cols])).astype(o_ref.dtype)

    @pl.when(c == pl.num_programs(1) - 1)
    def _():
        s_out_ref[...] = s_ref[...]


def _gdn_prompt(qkv, z, bg, conv_w, gdn_norm):
    b, l, _ = qkv.shape
    nc = l // CHUNK
    bg4 = bg.reshape(b, nc, CHUNK, 2 * GDN_HEADS)
    bgt4 = jnp.swapaxes(bg4, 2, 3)
    halo_blocks = CHUNK // SUBLANES
    seq = lambda w: pl.BlockSpec((None, CHUNK, w), lambda i, c: (i, c, 0))
    chunk = lambda *s: pl.BlockSpec((None, None) + s, lambda i, c: (i, c) + (0,) * len(s))
    inter_shapes = [((GDN_HEADS, 2 * CHUNK, GDN_DK), BF16), ((N_GROUPS, GDN_DK, GROUP_ROWS), BF16),
                    ((N_GROUPS, GROUP_ROWS, GROUP_ROWS), BF16), ((N_GROUPS, GROUP_ROWS, GDN_DV), F32),
                    ((1, 2 * GDN_HEADS), F32)]
    inter_specs = [chunk(*s) for s, _ in inter_shapes]
    wqg, kdt, qk, u, gl = pl.pallas_call(
        _gdn_prep_kernel,
        grid=(b, nc),
        in_specs=[seq(QKV_CH),
                  pl.BlockSpec((None, SUBLANES, QKV_CH), lambda i, c: (i, jnp.maximum(c * halo_blocks - 1, 0), 0)),
                  chunk(CHUNK, 2 * GDN_HEADS), chunk(2 * GDN_HEADS, CHUNK), _const_spec(conv_w.shape)],
        out_specs=inter_specs,
        out_shape=[jax.ShapeDtypeStruct((b, nc) + s, d) for s, d in inter_shapes],
        scratch_shapes=[pltpu.VMEM((SUBLANES + CHUNK, QKV_CH), F32)],
        compiler_params=_params(("parallel", "parallel")),
        name="gdn_prep",
    )(qkv, qkv, bg4, bgt4, conv_w)
    return pl.pallas_call(
        _gdn_scan_kernel,
        grid=(b, nc),
        in_specs=inter_specs + [seq(V_DIM), _const_spec((1, GDN_DV))],
        out_specs=[seq(V_DIM),
                   pl.BlockSpec((None, GDN_HEADS, GDN_DK, GDN_DV), lambda i, c: (i, 0, 0, 0))],
        out_shape=[jax.ShapeDtypeStruct((b, l, V_DIM), BF16),
                   jax.ShapeDtypeStruct((b, GDN_HEADS, GDN_DK, GDN_DV), F32)],
        scratch_shapes=[pltpu.VMEM((GDN_HEADS, GDN_DK, GDN_DV), F32)],
        compiler_params=_params(("parallel", "arbitrary")),
        name="gdn_scan",
    )(wqg, kdt, qk, u, gl, z, gdn_norm)


GDN_SAMPLE_ROWS = 8


def _gdn_sample_kernel(raw_ref, cs_ref, z_ref, bg_ref, s_in_ref, cw_ref, gn_ref, o_ref, s_out_ref):
    conv = cw_ref[SHORT_CONV - 1:SHORT_CONV, :] * raw_ref[...]
    for j in range(SHORT_CONV - 1):
        conv = conv + cw_ref[j:j + 1, :] * cs_ref[j]
    qkv = _silu(conv)
    bg = bg_ref[...]
    for h in range(GDN_HEADS):
        q = qkv[:, h * GDN_DK:(h + 1) * GDN_DK]
        k = qkv[:, QK_DIM + h * GDN_DK:QK_DIM + (h + 1) * GDN_DK]
        v = qkv[:, 2 * QK_DIM + h * GDN_DV:2 * QK_DIM + (h + 1) * GDN_DV]
        q = q * lax.rsqrt(jnp.sum(q * q, axis=-1, keepdims=True) + NORM_EPS) * (GDN_DK ** -0.5)
        k = k * lax.rsqrt(jnp.sum(k * k, axis=-1, keepdims=True) + NORM_EPS)
        beta = bg[:, h:h + 1]
        eg = jnp.exp(bg[:, GDN_HEADS + h:GDN_HEADS + h + 1])
        u = v * beta
        qk = jnp.sum(q * k, axis=-1, keepdims=True)
        w_t = (k * (beta * eg)).T
        qg_t = (q * eg).T
        k_t = k.T
        o_rows = []
        for r in range(GDN_SAMPLE_ROWS):
            s = s_in_ref[r, h]
            w_s = jnp.sum(w_t[:, r:r + 1] * s, axis=0, keepdims=True)
            q_s = jnp.sum(qg_t[:, r:r + 1] * s, axis=0, keepdims=True)
            v_new = u[r:r + 1] - w_s
            o_rows.append(q_s + qk[r:r + 1] * v_new)
            s_out_ref[r, h] = s * eg[r:r + 1] + k_t[:, r:r + 1] * v_new
        o = _rms(jnp.concatenate(o_rows, axis=0), gn_ref[...])
        cols = slice(h * GDN_DV, (h + 1) * GDN_DV)
        o_ref[:, cols] = (o * _silu(z_ref[:, cols])).astype(o_ref.dtype)


def _gdn_sample(raw, conv_state_t, z, bg, state, conv_w, gdn_norm):
    n = raw.shape[0]
    rb = GDN_SAMPLE_ROWS
    row = lambda w: pl.BlockSpec((rb, w), lambda i: (i, 0))
    st = pl.BlockSpec((rb, GDN_HEADS, GDN_DK, GDN_DV), lambda i: (i, 0, 0, 0))
    return pl.pallas_call(
        _gdn_sample_kernel,
        grid=(n // rb,),
        in_specs=[row(QKV_CH), pl.BlockSpec((SHORT_CONV - 1, rb, QKV_CH), lambda i: (0, i, 0)),
                  row(V_DIM), row(2 * GDN_HEADS), st, _const_spec(conv_w.shape), _const_spec((1, GDN_DV))],
        out_specs=[row(V_DIM), st],
        out_shape=[jax.ShapeDtypeStruct((n, V_DIM), BF16), jax.ShapeDtypeStruct(state.shape, F32)],
        compiler_params=_params(("parallel",)),
        name="gdn_sample",
    )(raw, conv_state_t, z, bg, state, conv_w, gdn_norm)


CONF_TILE = 256
CONF_ROWS = 64


def _ln_silu(x, g, b):
    mu = jnp.mean(x, axis=-1, keepdims=True)
    xc = x - mu
    var = jnp.mean(xc * xc, axis=-1, keepdims=True)
    return _silu(xc * lax.rsqrt(var + LN_EPS) * g + b)


def _conf_prompt_kernel(u_ref, w_ref, b_ref, lng_ref, lnb_ref, cv_ref, ubuf_ref, acc_ref):
    @pl.when(pl.program_id(1) == 0)
    def _():
        ubuf_ref[0:CONF_HALO, :] = jnp.zeros((CONF_HALO, CONF_CH), F32)

    ubuf_ref[CONF_HALO:CONF_HALO + CONF_TILE, :] = u_ref[...]
    base = CONF_HALO - (CONF_K - 1)
    for rb in range(CONF_TILE // CONF_ROWS):
        for cb in range(CONF_CH // LANES):
            cols = slice(cb * LANES, (cb + 1) * LANES)
            r0 = base + rb * CONF_ROWS
            acc = w_ref[0:1, cols] * ubuf_ref[r0:r0 + CONF_ROWS, cols]
            for j in range(1, CONF_K):
                acc = acc + w_ref[j:j + 1, cols] * ubuf_ref[r0 + j:r0 + j + CONF_ROWS, cols]
            acc_ref[rb * CONF_ROWS:(rb + 1) * CONF_ROWS, cols] = acc + b_ref[:, cols]
    ubuf_ref[0:CONF_HALO, :] = ubuf_ref[CONF_TILE:CONF_TILE + CONF_HALO, :]
    cv_ref[...] = _ln_silu(acc_ref[...], lng_ref[...], lnb_ref[...]).astype(cv_ref.dtype)


def _conf_prompt(u, w, b, ln_g, ln_b):
    bsz, l, _ = u.shape
    seq = pl.BlockSpec((None, CONF_TILE, CONF_CH), lambda i, t: (i, t, 0))
    return pl.pallas_call(
        _conf_prompt_kernel,
        grid=(bsz, l // CONF_TILE),
        in_specs=[seq, _const_spec(w.shape), _const_spec((1, CONF_CH)), _const_spec((1, CONF_CH)),
                  _const_spec((1, CONF_CH))],
        out_specs=seq,
        out_shape=jax.ShapeDtypeStruct((bsz, l, CONF_CH), BF16),
        scratch_shapes=[pltpu.VMEM((CONF_HALO + CONF_TILE, CONF_CH), F32), pltpu.VMEM((CONF_TILE, CONF_CH), F32)],
        compiler_params=_params(("parallel", "arbitrary")),
        name="conf_prompt",
    )(u, w, b, ln_g, ln_b)


def _conf_sample_kernel(u_ref, cs_ref, w_ref, b_ref, lng_ref, lnb_ref, cv_ref):
    acc = w_ref[CONF_K - 1:CONF_K, :] * u_ref[...] + b_ref[...]
    for j in range(CONF_K - 1):
        acc = acc + w_ref[j:j + 1, :] * cs_ref[j]
    cv_ref[...] = _ln_silu(acc, lng_ref[...], lnb_ref[...]).astype(cv_ref.dtype)


def _conf_sample(u, conf_state_t, w, b, ln_g, ln_b):
    n = u.shape[0]
    return pl.pallas_call(
        _conf_sample_kernel,
        out_shape=jax.ShapeDtypeStruct((n, CONF_CH), BF16),
        compiler_params=pltpu.CompilerParams(vmem_limit_bytes=VMEM_LIMIT_BYTES),
        name="conf_sample",
    )(u, conf_state_t, w, b, ln_g, ln_b)


MOE_TILE = 256
SLAB = D_MODEL // LANES


def _mix_kernel(x_ref, o_ref, cv_ref, ga_ref, gb_ref, wgo_ref, wco_ref, wout_ref, nffn_ref, wr_ref, br_ref,
                x1_ref, hm3_ref, topi_ref, topw_ref):
    tm = x_ref.shape[0]
    branch_a = _dot(o_ref[...], wgo_ref[...])
    branch_b = _dot(cv_ref[...], wco_ref[...])
    merged = ga_ref[...] * branch_a + gb_ref[...] * branch_b
    x1 = x_ref[...] + _bdot(merged, wout_ref[...])
    x1_ref[...] = x1
    hm = _rms(x1, nffn_ref[...])
    for s in range(SLAB):
        hm3_ref[pl.ds(s, tm, stride=SLAB), :] = hm[:, s * LANES:(s + 1) * LANES]
    logits = _hdot(hm, wr_ref[...]) + br_ref[...]
    lane = lax.broadcasted_iota(jnp.int32, logits.shape, 1)
    work = logits
    top_v = []
    top_i = jnp.zeros(logits.shape, jnp.int32)
    for k in range(TOP_K):
        m = jnp.max(work, axis=-1, keepdims=True)
        first = jnp.min(jnp.where(work == m, lane, N_EXPERTS), axis=-1, keepdims=True)
        top_v.append(m)
        top_i = jnp.where(lane == k, first, top_i)
        work = jnp.where(lane == first, -jnp.inf, work)
    ex = [jnp.exp(v - top_v[0]) for v in top_v]
    den = ex[0]
    for e in ex[1:]:
        den = den + e
    top_w = jnp.zeros(logits.shape, F32)
    for k, e in enumerate(ex):
        top_w = jnp.where(lane == k, e / den, top_w)
    topi_ref[...] = top_i
    topw_ref[...] = top_w


def _mix(x, o, cv, ga, gb, wgo, wco, wout, norm_ffn, w_router, b_router, tm):
    n = x.shape[0]
    row = lambda w: pl.BlockSpec((tm, w), lambda i: (i, 0))
    return pl.pallas_call(
        _mix_kernel,
        grid=(n // tm,),
        in_specs=[row(D_MODEL), row(V_DIM), row(CONF_CH), row(D_MODEL), row(D_MODEL),
                  _const_spec(wgo.shape), _const_spec(wco.shape), _const_spec(wout.shape),
                  _const_spec((1, D_MODEL)), _const_spec(w_router.shape), _const_spec((1, N_EXPERTS))],
        out_specs=[row(D_MODEL), pl.BlockSpec((tm * SLAB, LANES), lambda i: (i, 0)), row(N_EXPERTS), row(N_EXPERTS)],
        out_shape=[jax.ShapeDtypeStruct((n, D_MODEL), F32), jax.ShapeDtypeStruct((n * SLAB, LANES), F32),
                   jax.ShapeDtypeStruct((n, N_EXPERTS), jnp.int32), jax.ShapeDtypeStruct((n, N_EXPERTS), F32)],
        compiler_params=_params(("parallel",)),
        name="mix",
    )(x, o, cv, ga, gb, wgo, wco, wout, norm_ffn, w_router, b_router)


ROUTE_BLOCK_MAX = 16384


def _equal_blocks(n):
    for d in range(1, n // LANES + 1):
        if n % d == 0 and n // d <= ROUTE_BLOCK_MAX and (n // d) % LANES == 0:
            return d
    raise ValueError(f"cannot split {n} elements into lane-aligned blocks")


def _route_rows_kernel(pos_ref, rp_ref, *, n_init):
    s = pl.program_id(0)
    rows_per = rp_ref.shape[0] // n_init
    pairs_per = pos_ref.shape[-1]

    @pl.when(s < n_init)
    def _():
        base = s * rows_per

        def init(i, carry):
            for j in range(SUBLANES):
                rp_ref[base + i * SUBLANES + j] = -1
            return carry
        lax.fori_loop(0, rows_per // SUBLANES, init, 0)

    @pl.when(s >= n_init)
    def _():
        base = (s - n_init) * pairs_per

        def place(i, carry):
            for j in range(SUBLANES):
                p = i * SUBLANES + j
                rp_ref[pos_ref[0, p]] = base + p
            return carry
        lax.fori_loop(0, pairs_per // SUBLANES, place, 0)


def _route(top_i):
    n = top_i.shape[0]
    pairs = n * TOP_K
    n_tiles = pairs // MOE_TILE + N_EXPERTS
    e_flat = top_i.reshape(pairs)
    onehot = (e_flat[:, None] == jnp.arange(N_EXPERTS, dtype=jnp.int32)[None, :]).astype(jnp.int32)
    csum = jnp.cumsum(onehot, axis=0)
    rank = jnp.take_along_axis(csum, e_flat[:, None], axis=1)[:, 0] - 1
    tiles_per = (csum[-1] + MOE_TILE - 1) // MOE_TILE
    tile_end = jnp.cumsum(tiles_per)
    n_used = tile_end[-1]
    pos = (tile_end - tiles_per)[e_flat] * MOE_TILE + rank
    n_rows = n_tiles * MOE_TILE
    n_init, n_place = _equal_blocks(n_rows), _equal_blocks(pairs)
    row_pair = pl.pallas_call(
        functools.partial(_route_rows_kernel, n_init=n_init),
        grid=(n_init + n_place,),
        in_specs=[pl.BlockSpec((None, 1, pairs // n_place), lambda s: (jnp.maximum(s - n_init, 0), 0, 0),
                               memory_space=pltpu.SMEM)],
        out_specs=pl.BlockSpec(memory_space=pltpu.SMEM),
        out_shape=jax.ShapeDtypeStruct((n_rows,), jnp.int32),
        compiler_params=_params(("arbitrary",)),
        name="route_rows",
    )(pos.reshape(n_place, 1, pairs // n_place))
    row_pair = jnp.concatenate([jnp.full((MOE_TILE,), -1, jnp.int32), row_pair]).reshape(n_tiles + 1, 1, MOE_TILE)
    t = jnp.minimum(jnp.arange(n_tiles + 1, dtype=jnp.int32), n_used - 1)
    tile_expert = jnp.sum((tile_end[None, :] <= t[:, None]).astype(jnp.int32), axis=1)
    return tile_expert, n_used.reshape(1), row_pair


def _experts_kernel(te_ref, nu_ref, idx0_ref, idxn_ref, idxp_ref, x3_hbm, wgu_ref, bgu_ref, wd_ref, bd_ref,
                    y_hbm, x0, x1, y0, y1, wgu16, wd16, sem_in, sem_out):
    t = pl.program_id(0)
    n_used = nu_ref[0]
    xs, ys = (x0, x1), (y0, y1)
    tile_rows = MOE_TILE * SLAB
    pair_rows = y_hbm.shape[0] - 2 * tile_rows

    def start_gather(idx_ref, c):
        for i in range(MOE_TILE):
            tok = jnp.maximum(idx_ref[0, i], 0) // TOP_K
            pltpu.make_async_copy(x3_hbm.at[tok], xs[c].at[pl.ds(i * SLAB, SLAB), :], sem_in.at[c]).start()

    def start_scatter(idx_ref, c):
        for i in range(MOE_TILE):
            pair = idx_ref[0, i]
            row = jnp.where(pair >= 0, pair * SLAB, pair_rows + c * tile_rows + i * SLAB)
            pltpu.make_async_copy(ys[c].at[pl.ds(i * SLAB, SLAB), :],
                                  y_hbm.at[pl.ds(pl.multiple_of(row, SLAB), SLAB), :], sem_out.at[c]).start()

    def wait_tile(buf, sem):
        pltpu.make_async_copy(buf, buf, sem).wait()

    @pl.when(t == 0)
    def _():
        y0[...] = jnp.zeros(y0.shape, F32)
        y1[...] = jnp.zeros(y1.shape, F32)
        pltpu.make_async_copy(y0, y_hbm.at[pl.ds(pair_rows, tile_rows), :], sem_out.at[0]).start()
        start_gather(idx0_ref, 0)

    @pl.when((t < n_used) & ((t == 0) | (te_ref[t] != te_ref[jnp.maximum(t - 1, 0)])))
    def _():
        wgu16[...] = wgu_ref[...].astype(BF16)
        wd16[...] = wd_ref[...].astype(BF16)

    def step(c):
        o = 1 - c
        wait_tile(xs[c], sem_in.at[c])
        start_gather(idxn_ref, o)
        start_scatter(idxp_ref, o)
        x = jnp.concatenate([xs[c][pl.ds(s, MOE_TILE, stride=SLAB), :] for s in range(SLAB)], axis=1)
        gu = _dot(x.astype(BF16), wgu16[...]) + bgu_ref[...]
        gate = jnp.minimum(gu[:, :D_FF], SWIGLU_LIMIT)
        up = jnp.clip(gu[:, D_FF:], -SWIGLU_LIMIT, SWIGLU_LIMIT)
        hid = (up + 1.0) * (gate * _sigmoid(SWIGLU_ALPHA * gate))
        ye = _dot(hid.astype(BF16), wd16[...]) + bd_ref[...]
        wait_tile(ys[c], sem_out.at[c])
        for s in range(SLAB):
            ys[c][pl.ds(s, MOE_TILE, stride=SLAB), :] = ye[:, s * LANES:(s + 1) * LANES]

    def drain(c):
        o = 1 - c
        start_scatter(idxp_ref, o)
        wait_tile(ys[o], sem_out.at[o])
        wait_tile(ys[c], sem_out.at[c])
        wait_tile(xs[c], sem_in.at[c])

    for c in range(2):
        pl.when((t < n_used) & (t % 2 == c))(functools.partial(step, c))
        pl.when((t == n_used) & (t % 2 == c))(functools.partial(drain, c))


def _experts(hm3, tile_expert, n_used, row_pair, wgu, bgu, wd, bd):
    n_steps = row_pair.shape[0]
    n_tiles = n_steps - 1
    pairs = hm3.shape[0] // SLAB * TOP_K
    x3 = hm3.reshape(-1, SLAB, LANES)
    idx = lambda f: pl.BlockSpec((None, 1, MOE_TILE), f, memory_space=pltpu.SMEM)
    tile_buf = pltpu.VMEM((MOE_TILE * SLAB, LANES), F32)
    grid_spec = pltpu.PrefetchScalarGridSpec(
        num_scalar_prefetch=2,
        grid=(n_steps,),
        in_specs=[idx(lambda t, te, nu: (1, 0, 0)),
                  idx(lambda t, te, nu: (jnp.minimum(t + 2, n_tiles), 0, 0)),
                  idx(lambda t, te, nu: (t, 0, 0)),
                  pl.BlockSpec(memory_space=pl.ANY),
                  pl.BlockSpec((None, D_MODEL, 2 * D_FF), lambda t, te, nu: (te[t], 0, 0)),
                  pl.BlockSpec((None, 1, 2 * D_FF), lambda t, te, nu: (te[t], 0, 0)),
                  pl.BlockSpec((None, D_FF, D_MODEL), lambda t, te, nu: (te[t], 0, 0)),
                  pl.BlockSpec((None, 1, D_MODEL), lambda t, te, nu: (te[t], 0, 0))],
        out_specs=pl.BlockSpec(memory_space=pl.ANY),
        scratch_shapes=[tile_buf, tile_buf, tile_buf, tile_buf,
                        pltpu.VMEM((D_MODEL, 2 * D_FF), BF16), pltpu.VMEM((D_FF, D_MODEL), BF16),
                        pltpu.SemaphoreType.DMA((2,)), pltpu.SemaphoreType.DMA((2,))])
    return pl.pallas_call(
        _experts_kernel,
        grid_spec=grid_spec,
        out_shape=jax.ShapeDtypeStruct(((pairs + 2 * MOE_TILE) * SLAB, LANES), F32),
        compiler_params=_params(("arbitrary",)),
        name="experts",
    )(tile_expert, n_used, row_pair, row_pair, row_pair, x3, wgu, bgu, wd, bd)


def _ple_kernel(x1_ref, y_ref, topw_ref, p_ref, nple_ref, wg_ref, wp_ref, nfin_ref, out_ref):
    tm = x1_ref.shape[0]
    topw = topw_ref[...]
    moe = []
    for s in range(SLAB):
        acc = topw[:, 0:1] * y_ref[pl.ds(s, tm, stride=TOP_K * SLAB), :]
        for k in range(1, TOP_K):
            acc = acc + topw[:, k:k + 1] * y_ref[pl.ds(k * SLAB + s, tm, stride=TOP_K * SLAB), :]
        moe.append(acc)
    x2 = x1_ref[...] + jnp.concatenate(moe, axis=1)
    gate = _sigmoid(_bdot(_rms(x2, nple_ref[...]), wg_ref[...]))
    x3 = x2 + _bdot(p_ref[...], wp_ref[...]) * gate
    out_ref[...] = _rms(x3, nfin_ref[...])


def _ple(x1, y2d, topw, p, norm_ple, wg, wp, norm_final, tm, row_offset):
    n = x1.shape[0]
    off = row_offset // tm
    row = lambda w: pl.BlockSpec((tm, w), lambda i: (i, 0))
    return pl.pallas_call(
        _ple_kernel,
        grid=(n // tm,),
        in_specs=[row(D_MODEL), pl.BlockSpec((tm * TOP_K * SLAB, LANES), lambda i: (i + off, 0)), row(N_EXPERTS),
                  row(PLE_DIM), _const_spec((1, D_MODEL)), _const_spec(wg.shape), _const_spec(wp.shape),
                  _const_spec((1, D_MODEL))],
        out_specs=row(D_MODEL),
        out_shape=jax.ShapeDtypeStruct((n, D_MODEL), F32),
        compiler_params=_params(("parallel",)),
        name="ple",
    )(x1, y2d, topw, p, norm_ple, wg, wp, norm_final)


def kernel(x_prompt, x_sample, p_prompt, p_sample, state_gdn, state_qkv_conv, state_conf_conv, norm_mix, w_in, conv_qkv_w, a_log, dt_bias, gdn_norm, w_gdn_out, conf_dw_w, conf_dw_b, conf_ln_g, conf_ln_b, w_conf_out, w_out, norm_ffn, w_router, b_router, w_gate_up, b_gate_up, w_down, b_down, norm_ple, w_ple_gate, w_ple_proj, norm_final):
    assert w_in.shape[0] == 1, "single-layer trunk"
    bsz, seq, _ = x_prompt.shape
    n_p = bsz * seq
    n_s = x_sample.shape[0] * x_sample.shape[1]
    assert x_sample.shape[1] == 1

    w_in0 = w_in[0]
    wqkv = w_in0[:, :OFF_Z].astype(BF16)
    wz = w_in0[:, OFF_Z:OFF_BETA].astype(BF16)
    wba = w_in0[:, OFF_BETA:OFF_GLU].astype(BF16)
    wglu = w_in0[:, OFF_GLU:OFF_GATE].astype(BF16)
    wgate = w_in0[:, OFF_GATE:].astype(BF16)
    zeros_h = jnp.zeros((GDN_HEADS,), F32)
    alog16 = jnp.concatenate([zeros_h, a_log[0]]).reshape(1, 2 * GDN_HEADS)
    dtb16 = jnp.concatenate([zeros_h, dt_bias[0]]).reshape(1, 2 * GDN_HEADS)
    wgo = w_gdn_out[0].astype(BF16)
    wco = w_conf_out[0].astype(BF16)
    wout = w_out[0].astype(BF16)
    wpg = w_ple_gate[0].astype(BF16)
    wpp = w_ple_proj[0].astype(BF16)
    bgu = b_gate_up[0].reshape(N_EXPERTS, 1, 2 * D_FF)
    bd = b_down[0].reshape(N_EXPERTS, 1, D_MODEL)
    row = lambda v: v.reshape(1, -1)
    inproj_w = (row(norm_mix[0]), wqkv, wz, wba, wglu, wgate, alog16, dtb16)
    mix_w = (wgo, wco, wout, row(norm_ffn[0]), w_router[0], row(b_router[0]))
    ple_w = (row(norm_ple[0]), wpg, wpp, row(norm_final))
    conf_w = (conf_dw_w[0], row(conf_dw_b[0]), row(conf_ln_g[0]), row(conf_ln_b[0]))

    xp = x_prompt.reshape(n_p, D_MODEL)
    qkv_p, z_p, bg_p, u_p, ga_p, gb_p = _inproj(xp, *inproj_w, tm=256)
    qkv_p3 = qkv_p.reshape(bsz, seq, QKV_CH)
    u_p3 = u_p.reshape(bsz, seq, CONF_CH)
    o_p, s_p = _gdn_prompt(qkv_p3, z_p.reshape(bsz, seq, V_DIM), bg_p, conv_qkv_w[0], row(gdn_norm[0]))
    cv_p = _conf_prompt(u_p3, *conf_w)

    xs = x_sample.reshape(n_s, D_MODEL)
    qkv_s, z_s, bg_s, u_s, ga_s, gb_s = _inproj(xs, *inproj_w, tm=n_s)
    o_s, s_s = _gdn_sample(qkv_s, jnp.swapaxes(state_qkv_conv[0], 0, 1), z_s, bg_s, state_gdn[0],
                           conv_qkv_w[0], row(gdn_norm[0]))
    cv_s = _conf_sample(u_s, jnp.swapaxes(state_conf_conv[0], 0, 1), *conf_w)

    tm_p = 512
    x1_p, hm3_p, ti_p, tw_p = _mix(xp, o_p.reshape(n_p, V_DIM), cv_p.reshape(n_p, CONF_CH), ga_p, gb_p, *mix_w,
                                   tm=tm_p)
    x1_s, hm3_s, ti_s, tw_s = _mix(xs, o_s, cv_s, ga_s, gb_s, *mix_w, tm=n_s)
    routing = _route(jnp.concatenate([ti_p, ti_s], axis=0)[:, :TOP_K])
    y2d = _experts(jnp.concatenate([hm3_p, hm3_s], axis=0), *routing, w_gate_up[0], bgu, w_down[0], bd)
    y_p = _ple(x1_p, y2d, tw_p, p_prompt[0].reshape(n_p, PLE_DIM), *ple_w, tm=tm_p, row_offset=0)
    y_s = _ple(x1_s, y2d, tw_s, p_sample[0].reshape(n_s, PLE_DIM), *ple_w, tm=n_s, row_offset=n_p)

    new_qkv_s = jnp.concatenate([state_qkv_conv[0][:, 1:], qkv_s[:, None, :]], axis=1)
    new_conf_s = jnp.concatenate([state_conf_conv[0][:, 1:], u_s[:, None, :]], axis=1)
    return (y_p.reshape(bsz, seq, D_MODEL), y_s.reshape(n_s, 1, D_MODEL),
            s_p[None], qkv_p3[:, seq - (SHORT_CONV - 1):][None], u_p3[:, seq - (CONF_K - 1):][None],
            s_s[None], new_qkv_s[None], new_conf_s[None])
```

```python
import functools

import jax
import jax.numpy as jnp
from jax import lax
from jax.experimental import pallas as pl
from jax.experimental.pallas import tpu as pltpu

F32 = jnp.float32
BF16 = jnp.bfloat16

D_MODEL = 1024
GDN_HEADS = 8
GDN_DK = 128
GDN_DV = 128
QK_DIM = GDN_HEADS * GDN_DK
V_DIM = GDN_HEADS * GDN_DV
QKV_CH = 2 * QK_DIM + V_DIM
SHORT_CONV = 4
CHUNK = 64
CONF_CH = D_MODEL // 2
CONF_K = 31
N_EXPERTS = 32
TOP_K = 4
D_FF = D_MODEL
SWIGLU_LIMIT = 7.0
SWIGLU_ALPHA = 1.702
PLE_DIM = 256
NORM_EPS = 1e-6
LN_EPS = 1e-5

OFF_Z = QKV_CH
OFF_BETA = OFF_Z + V_DIM
OFF_GLU = OFF_BETA + 2 * GDN_HEADS
OFF_GATE = OFF_GLU + 2 * CONF_CH

VMEM_LIMIT_BYTES = 52 * 1024 * 1024
SUBLANES = 8
LANES = 128
CONF_HALO = 32


def _dot(a, b):
    return jnp.dot(a, b, preferred_element_type=F32)


def _bdot(a, b):
    return jnp.dot(a.astype(BF16), b.astype(BF16), preferred_element_type=F32)


def _hdot(a, b):
    return jnp.dot(a, b, preferred_element_type=F32, precision=lax.Precision.HIGHEST)


def _sigmoid(x):
    return 1.0 / (1.0 + jnp.exp(-x))


def _silu(x):
    return x * _sigmoid(x)


def _rms(x, g):
    return x * lax.rsqrt(jnp.mean(x * x, axis=-1, keepdims=True) + NORM_EPS) * g


def _const_spec(shape):
    nd = len(shape)
    return pl.BlockSpec(shape, lambda *_: (0,) * nd, pipeline_mode=pl.Buffered(1))


def _params(sem):
    return pltpu.CompilerParams(dimension_semantics=sem, vmem_limit_bytes=VMEM_LIMIT_BYTES)


def _inproj_kernel(x_ref, nw_ref, wqkv_ref, wz_ref, wba_ref, wglu_ref, wgate_ref, alog_ref, dtb_ref,
                   qkv_ref, z_ref, bg_ref, u_ref, ga_ref, gb_ref):
    a = _rms(x_ref[...], nw_ref[...]).astype(BF16)
    for c in range(QKV_CH // D_MODEL):
        cols = slice(c * D_MODEL, (c + 1) * D_MODEL)
        qkv_ref[:, cols] = _dot(a, wqkv_ref[:, cols])
    z_ref[...] = _dot(a, wz_ref[...])
    ba = _dot(a, wba_ref[...])
    lane = lax.broadcasted_iota(jnp.int32, ba.shape, 1)
    t = ba + dtb_ref[...]
    softplus = jnp.maximum(t, 0.0) + jnp.log(1.0 + jnp.exp(-jnp.abs(t)))
    bg_ref[...] = jnp.where(lane < GDN_HEADS, _sigmoid(ba), -jnp.exp(alog_ref[...]) * softplus)
    glu = _dot(a, wglu_ref[...])
    u_ref[...] = glu[:, :CONF_CH] * _sigmoid(glu[:, CONF_CH:])
    ga_ref[...] = _sigmoid(_dot(a, wgate_ref[:, :D_MODEL]))
    gb_ref[...] = _sigmoid(_dot(a, wgate_ref[:, D_MODEL:]))


def _inproj(x, norm_w, wqkv, wz, wba, wglu, wgate, alog16, dtb16, tm):
    n = x.shape[0]
    row = lambda w: pl.BlockSpec((tm, w), lambda i: (i, 0))
    return pl.pallas_call(
        _inproj_kernel,
        grid=(n // tm,),
        in_specs=[row(D_MODEL), _const_spec((1, D_MODEL)), _const_spec(wqkv.shape), _const_spec(wz.shape),
                  _const_spec(wba.shape), _const_spec(wglu.shape), _const_spec(wgate.shape),
                  _const_spec((1, 2 * GDN_HEADS)), _const_spec((1, 2 * GDN_HEADS))],
        out_specs=[row(QKV_CH), row(V_DIM), row(2 * GDN_HEADS), row(CONF_CH), row(D_MODEL), row(D_MODEL)],
        out_shape=[jax.ShapeDtypeStruct((n, w), F32)
                   for w in (QKV_CH, V_DIM, 2 * GDN_HEADS, CONF_CH, D_MODEL, D_MODEL)],
        compiler_params=_params(("parallel",)),
        name="inproj",
    )(x, norm_w, wqkv, wz, wba, wglu, wgate, alog16, dtb16)


GDN_GROUP = 4
PREP_CHUNKS = 2
GROUP_ROWS = GDN_GROUP * CHUNK
N_GROUPS = GDN_HEADS // GDN_GROUP


def _block_unit_lower_inverse(a, eye):
    b = -a
    t = eye + b
    p = _bdot(b, b)
    covered = 2
    while covered < CHUNK:
        covered *= 2
        if covered < CHUNK:
            r = _bdot(jnp.concatenate([t, p], axis=0), p)
            t = t + r[:GROUP_ROWS]
            p = r[GROUP_ROWS:]
        else:
            t = t + _bdot(t, p)
    return t


def _gdn_prep_kernel(qkv_ref, halo_ref, bg_ref, bgt_ref, cw_ref, wqg_ref, kdt_ref, qk_ref, u_ref, gl_ref, ext_ref):
    c = pl.program_id(1)
    ext_ref[0:SUBLANES, :] = jnp.where(c > 0, halo_ref[...], 0.0)
    ext_ref[SUBLANES:SUBLANES + PREP_CHUNKS * CHUNK, :] = qkv_ref[...]
    for n in range(PREP_CHUNKS):
        _gdn_prep_chunk(n, bg_ref, bgt_ref, cw_ref, wqg_ref, kdt_ref, qk_ref, u_ref, gl_ref, ext_ref)


def _gdn_prep_chunk(n, bg_ref, bgt_ref, cw_ref, wqg_ref, kdt_ref, qk_ref, u_ref, gl_ref, ext_ref):
    base = SUBLANES + n * CHUNK

    def convsilu(c0):
        cols = slice(c0, c0 + LANES)
        acc = cw_ref[SHORT_CONV - 1:SHORT_CONV, cols] * ext_ref[base:base + CHUNK, cols]
        for j in range(SHORT_CONV - 1):
            r0 = base - (SHORT_CONV - 1) + j
            acc = acc + cw_ref[j:j + 1, cols] * ext_ref[r0:r0 + CHUNK, cols]
        return _silu(acc)

    r64 = lax.broadcasted_iota(jnp.int32, (CHUNK, CHUNK), 0)
    c64 = lax.broadcasted_iota(jnp.int32, (CHUNK, CHUNK), 1)
    bg = bg_ref[n]
    gc_all = _hdot((r64 >= c64).astype(F32), bg)
    gct_all = _hdot(bgt_ref[n], (r64 <= c64).astype(F32))
    gl_ref[n] = gc_all[CHUNK - 1:CHUNK, :]

    row = lax.broadcasted_iota(jnp.int32, (GROUP_ROWS, GROUP_ROWS), 0)
    col = lax.broadcasted_iota(jnp.int32, (GROUP_ROWS, GROUP_ROWS), 1)
    blk = row - col + (col & (CHUNK - 1))
    same = (blk >= 0) & (blk < CHUNK)
    tril = same & (row >= col)
    strict = same & (row > col)
    eye = (row == col).astype(F32)
    nt = (((1,), (1,)), ((), ()))

    for g in range(N_GROUPS):
        heads = range(g * GDN_GROUP, (g + 1) * GDN_GROUP)
        stack = lambda f: jnp.concatenate([f(h) for h in heads], axis=0)
        q = stack(lambda h: convsilu(h * GDN_DK))
        k = stack(lambda h: convsilu(QK_DIM + h * GDN_DK))
        v = stack(lambda h: convsilu(2 * QK_DIM + h * GDN_DV))
        q = q * lax.rsqrt(jnp.sum(q * q, axis=-1, keepdims=True) + NORM_EPS) * (GDN_DK ** -0.5)
        k = k * lax.rsqrt(jnp.sum(k * k, axis=-1, keepdims=True) + NORM_EPS)
        beta = stack(lambda h: bg[:, h:h + 1])
        gc = stack(lambda h: gc_all[:, GDN_HEADS + h:GDN_HEADS + h + 1])
        g_last = stack(lambda h: jnp.broadcast_to(
            gc_all[CHUNK - 1:CHUNK, GDN_HEADS + h:GDN_HEADS + h + 1], (CHUNK, 1)))
        gct = jnp.concatenate([gct_all[GDN_HEADS + h:GDN_HEADS + h + 1, :] for h in heads], axis=1)
        decay = jnp.exp(jnp.where(tril, gc - gct, -jnp.inf))
        kb = k * beta
        k16 = k.astype(BF16)
        kk = lax.dot_general(kb.astype(BF16), k16, nt, preferred_element_type=F32)
        t_inv = _block_unit_lower_inverse(jnp.where(strict, kk * decay, 0.0), eye)
        eg = jnp.exp(gc)
        uw = _bdot(t_inv, jnp.concatenate([v * beta, kb * eg], axis=1))
        u_ref[n, g] = uw[:, :GDN_DV]
        w = uw[:, GDN_DV:].astype(BF16)
        qg = (q * eg).astype(BF16)
        for i, h in enumerate(heads):
            rows = slice(i * CHUNK, (i + 1) * CHUNK)
            wqg_ref[n, h, 0:CHUNK, :] = w[rows]
            wqg_ref[n, h, CHUNK:2 * CHUNK, :] = qg[rows]
        qk = lax.dot_general(q.astype(BF16), k16, nt, preferred_element_type=F32) * decay
        qk_ref[n, g] = qk.astype(BF16)
        kdt_ref[n, g] = (k * jnp.exp(g_last - gc)).T.astype(BF16)


def _gdn_scan_kernel(wqg_ref, kdt_ref, qk_ref, u_ref, gl_ref, z_ref, gn_ref, o_ref, s_out_ref, s_ref):
    c = pl.program_id(1)

    @pl.when(c == 0)
    def _():
        s_ref[...] = jnp.zeros(s_ref.shape, F32)

    dl_all = jnp.exp(gl_ref[...])
    zeros = jnp.zeros((CHUNK, GDN_DV), BF16)
    for g in range(N_GROUPS):
        heads = list(range(g * GDN_GROUP, (g + 1) * GDN_GROUP))
        v_new, q_s = [], []
        for i, h in enumerate(heads):
            r = _dot(wqg_ref[h], s_ref[h].astype(BF16))
            v_new.append((u_ref[g, i * CHUNK:(i + 1) * CHUNK, :] - r[:CHUNK]).astype(BF16))
            q_s.append(r[CHUNK:])
        o_all = jnp.concatenate(q_s, axis=0) + _dot(qk_ref[g], jnp.concatenate(v_new, axis=0))
        for i, h in enumerate(heads):
            vm = jnp.concatenate([v_new[j] if j == i else zeros for j in range(GDN_GROUP)], axis=0)
            s_ref[h] = s_ref[h] * dl_all[:, GDN_HEADS + h:GDN_HEADS + h + 1] + _dot(kdt_ref[g], vm)
            o = _rms(o_all[i * CHUNK:(i + 1) * CHUNK], gn_ref[...])
            cols = slice(h * GDN_DV, (h + 1) * GDN_DV)
            o_ref[:, cols] = (o * _silu(z_ref[:, cols])).astype(o_ref.dtype)

    @pl.when(c == pl.num_programs(1) - 1)
    def _():
        s_out_ref[...] = s_ref[...]


def _gdn_prompt(qkv, z, bg, conv_w, gdn_norm):
    b, l, _ = qkv.shape
    nc = l // CHUNK
    bg4 = bg.reshape(b, nc, CHUNK, 2 * GDN_HEADS)
    bgt4 = jnp.swapaxes(bg4, 2, 3)
    halo_blocks = CHUNK // SUBLANES
    seq = lambda w: pl.BlockSpec((None, CHUNK, w), lambda i, c: (i, c, 0))
    chunk = lambda *s: pl.BlockSpec((None, None) + s, lambda i, c: (i, c) + (0,) * len(s))
    inter_shapes = [((GDN_HEADS, 2 * CHUNK, GDN_DK), BF16), ((N_GROUPS, GDN_DK, GROUP_ROWS), BF16),
                    ((N_GROUPS, GROUP_ROWS, GROUP_ROWS), BF16), ((N_GROUPS, GROUP_ROWS, GDN_DV), F32),
                    ((1, 2 * GDN_HEADS), F32)]
    inter_specs = [chunk(*s) for s, _ in inter_shapes]
    chunks = lambda *s: pl.BlockSpec((None, PREP_CHUNKS) + s, lambda i, c: (i, c) + (0,) * len(s))
    wqg, kdt, qk, u, gl = pl.pallas_call(
        _gdn_prep_kernel,
        grid=(b, nc // PREP_CHUNKS),
        in_specs=[pl.BlockSpec((None, PREP_CHUNKS * CHUNK, QKV_CH), lambda i, c: (i, c, 0)),
                  pl.BlockSpec((None, SUBLANES, QKV_CH),
                               lambda i, c: (i, jnp.maximum(c * PREP_CHUNKS * halo_blocks - 1, 0), 0)),
                  chunks(CHUNK, 2 * GDN_HEADS), chunks(2 * GDN_HEADS, CHUNK), _const_spec(conv_w.shape)],
        out_specs=[chunks(*s) for s, _ in inter_shapes],
        out_shape=[jax.ShapeDtypeStruct((b, nc) + s, d) for s, d in inter_shapes],
        scratch_shapes=[pltpu.VMEM((SUBLANES + PREP_CHUNKS * CHUNK, QKV_CH), F32)],
        compiler_params=_params(("parallel", "parallel")),
        name="gdn_prep",
    )(qkv, qkv, bg4, bgt4, conv_w)
    return pl.pallas_call(
        _gdn_scan_kernel,
        grid=(b, nc),
        in_specs=inter_specs + [seq(V_DIM), _const_spec((1, GDN_DV))],
        out_specs=[seq(V_DIM),
                   pl.BlockSpec((None, GDN_HEADS, GDN_DK, GDN_DV), lambda i, c: (i, 0, 0, 0))],
        out_shape=[jax.ShapeDtypeStruct((b, l, V_DIM), BF16),
                   jax.ShapeDtypeStruct((b, GDN_HEADS, GDN_DK, GDN_DV), F32)],
        scratch_shapes=[pltpu.VMEM((GDN_HEADS, GDN_DK, GDN_DV), F32)],
        compiler_params=_params(("parallel", "arbitrary")),
        name="gdn_scan",
    )(wqg, kdt, qk, u, gl, z, gdn_norm)


GDN_SAMPLE_ROWS = 8


def _gdn_sample_kernel(raw_ref, cs_ref, z_ref, bg_ref, s_in_ref, cw_ref, gn_ref, o_ref, s_out_ref):
    conv = cw_ref[SHORT_CONV - 1:SHORT_CONV, :] * raw_ref[...]
    for j in range(SHORT_CONV - 1):
        conv = conv + cw_ref[j:j + 1, :] * cs_ref[j]
    qkv = _silu(conv)
    bg = bg_ref[...]
    for h in range(GDN_HEADS):
        q = qkv[:, h * GDN_DK:(h + 1) * GDN_DK]
        k = qkv[:, QK_DIM + h * GDN_DK:QK_DIM + (h + 1) * GDN_DK]
        v = qkv[:, 2 * QK_DIM + h * GDN_DV:2 * QK_DIM + (h + 1) * GDN_DV]
        q = q * lax.rsqrt(jnp.sum(q * q, axis=-1, keepdims=True) + NORM_EPS) * (GDN_DK ** -0.5)
        k = k * lax.rsqrt(jnp.sum(k * k, axis=-1, keepdims=True) + NORM_EPS)
        beta = bg[:, h:h + 1]
        eg = jnp.exp(bg[:, GDN_HEADS + h:GDN_HEADS + h + 1])
        u = v * beta
        qk = jnp.sum(q * k, axis=-1, keepdims=True)
        w_t = (k * (beta * eg)).T
        qg_t = (q * eg).T
        k_t = k.T
        o_rows = []
        for r in range(GDN_SAMPLE_ROWS):
            s = s_in_ref[r, h]
            w_s = jnp.sum(w_t[:, r:r + 1] * s, axis=0, keepdims=True)
            q_s = jnp.sum(qg_t[:, r:r + 1] * s, axis=0, keepdims=True)
            v_new = u[r:r + 1] - w_s
            o_rows.append(q_s + qk[r:r + 1] * v_new)
            s_out_ref[r, h] = s * eg[r:r + 1] + k_t[:, r:r + 1] * v_new
        o = _rms(jnp.concatenate(o_rows, axis=0), gn_ref[...])
        cols = slice(h * GDN_DV, (h + 1) * GDN_DV)
        o_ref[:, cols] = (o * _silu(z_ref[:, cols])).astype(o_ref.dtype)


def _gdn_sample(raw, conv_state_t, z, bg, state, conv_w, gdn_norm):
    n = raw.shape[0]
    rb = GDN_SAMPLE_ROWS
    row = lambda w: pl.BlockSpec((rb, w), lambda i: (i, 0))
    st = pl.BlockSpec((rb, GDN_HEADS, GDN_DK, GDN_DV), lambda i: (i, 0, 0, 0))
    return pl.pallas_call(
        _gdn_sample_kernel,
        grid=(n // rb,),
        in_specs=[row(QKV_CH), pl.BlockSpec((SHORT_CONV - 1, rb, QKV_CH), lambda i: (0, i, 0)),
                  row(V_DIM), row(2 * GDN_HEADS), st, _const_spec(conv_w.shape), _const_spec((1, GDN_DV))],
        out_specs=[row(V_DIM), st],
        out_shape=[jax.ShapeDtypeStruct((n, V_DIM), BF16), jax.ShapeDtypeStruct(state.shape, F32)],
        compiler_params=_params(("parallel",)),
        name="gdn_sample",
    )(raw, conv_state_t, z, bg, state, conv_w, gdn_norm)


CONF_TILE = 256
CONF_ROWS = 64


def _ln_silu(x, g, b):
    mu = jnp.mean(x, axis=-1, keepdims=True)
    xc = x - mu
    var = jnp.mean(xc * xc, axis=-1, keepdims=True)
    return _silu(xc * lax.rsqrt(var + LN_EPS) * g + b)


def _conf_prompt_kernel(u_ref, w_ref, b_ref, lng_ref, lnb_ref, cv_ref, ubuf_ref, acc_ref):
    @pl.when(pl.program_id(1) == 0)
    def _():
        ubuf_ref[0:CONF_HALO, :] = jnp.zeros((CONF_HALO, CONF_CH), F32)

    ubuf_ref[CONF_HALO:CONF_HALO + CONF_TILE, :] = u_ref[...]
    base = CONF_HALO - (CONF_K - 1)
    for rb in range(CONF_TILE // CONF_ROWS):
        for cb in range(CONF_CH // LANES):
            cols = slice(cb * LANES, (cb + 1) * LANES)
            r0 = base + rb * CONF_ROWS
            acc = w_ref[0:1, cols] * ubuf_ref[r0:r0 + CONF_ROWS, cols]
            for j in range(1, CONF_K):
                acc = acc + w_ref[j:j + 1, cols] * ubuf_ref[r0 + j:r0 + j + CONF_ROWS, cols]
            acc_ref[rb * CONF_ROWS:(rb + 1) * CONF_ROWS, cols] = acc + b_ref[:, cols]
    ubuf_ref[0:CONF_HALO, :] = ubuf_ref[CONF_TILE:CONF_TILE + CONF_HALO, :]
    cv_ref[...] = _ln_silu(acc_ref[...], lng_ref[...], lnb_ref[...]).astype(cv_ref.dtype)


def _conf_prompt(u, w, b, ln_g, ln_b):
    bsz, l, _ = u.shape
    seq = pl.BlockSpec((None, CONF_TILE, CONF_CH), lambda i, t: (i, t, 0))
    return pl.pallas_call(
        _conf_prompt_kernel,
        grid=(bsz, l // CONF_TILE),
        in_specs=[seq, _const_spec(w.shape), _const_spec((1, CONF_CH)), _const_spec((1, CONF_CH)),
                  _const_spec((1, CONF_CH))],
        out_specs=seq,
        out_shape=jax.ShapeDtypeStruct((bsz, l, CONF_CH), BF16),
        scratch_shapes=[pltpu.VMEM((CONF_HALO + CONF_TILE, CONF_CH), F32), pltpu.VMEM((CONF_TILE, CONF_CH), F32)],
        compiler_params=_params(("parallel", "arbitrary")),
        name="conf_prompt",
    )(u, w, b, ln_g, ln_b)


def _conf_sample_kernel(u_ref, cs_ref, w_ref, b_ref, lng_ref, lnb_ref, cv_ref):
    acc = w_ref[CONF_K - 1:CONF_K, :] * u_ref[...] + b_ref[...]
    for j in range(CONF_K - 1):
        acc = acc + w_ref[j:j + 1, :] * cs_ref[j]
    cv_ref[...] = _ln_silu(acc, lng_ref[...], lnb_ref[...]).astype(cv_ref.dtype)


def _conf_sample(u, conf_state_t, w, b, ln_g, ln_b):
    n = u.shape[0]
    return pl.pallas_call(
        _conf_sample_kernel,
        out_shape=jax.ShapeDtypeStruct((n, CONF_CH), BF16),
        compiler_params=pltpu.CompilerParams(vmem_limit_bytes=VMEM_LIMIT_BYTES),
        name="conf_sample",
    )(u, conf_state_t, w, b, ln_g, ln_b)


MOE_TILE = 256
SLAB = D_MODEL // LANES


def _mix_kernel(x_ref, o_ref, cv_ref, ga_ref, gb_ref, wgo_ref, wco_ref, wout_ref, nffn_ref, wr_ref, br_ref,
                x1_ref, hm3_ref, topi_ref, topw_ref):
    tm = x_ref.shape[0]
    branch_a = _dot(o_ref[...], wgo_ref[...])
    branch_b = _dot(cv_ref[...], wco_ref[...])
    merged = ga_ref[...] * branch_a + gb_ref[...] * branch_b
    x1 = x_ref[...] + _bdot(merged, wout_ref[...])
    x1_ref[...] = x1
    hm = _rms(x1, nffn_ref[...])
    for s in range(SLAB):
        hm3_ref[pl.ds(s, tm, stride=SLAB), :] = hm[:, s * LANES:(s + 1) * LANES]
    logits = _hdot(hm, wr_ref[...]) + br_ref[...]
    lane = lax.broadcasted_iota(jnp.int32, logits.shape, 1)
    work = logits
    top_v = []
    top_i = jnp.zeros(logits.shape, jnp.int32)
    for k in range(TOP_K):
        m = jnp.max(work, axis=-1, keepdims=True)
        first = jnp.min(jnp.where(work == m, lane, N_EXPERTS), axis=-1, keepdims=True)
        top_v.append(m)
        top_i = jnp.where(lane == k, first, top_i)
        work = jnp.where(lane == first, -jnp.inf, work)
    ex = [jnp.exp(v - top_v[0]) for v in top_v]
    den = ex[0]
    for e in ex[1:]:
        den = den + e
    top_w = jnp.zeros(logits.shape, F32)
    for k, e in enumerate(ex):
        top_w = jnp.where(lane == k, e / den, top_w)
    topi_ref[...] = top_i
    topw_ref[...] = top_w


def _mix(x, o, cv, ga, gb, wgo, wco, wout, norm_ffn, w_router, b_router, tm):
    n = x.shape[0]
    row = lambda w: pl.BlockSpec((tm, w), lambda i: (i, 0))
    return pl.pallas_call(
        _mix_kernel,
        grid=(n // tm,),
        in_specs=[row(D_MODEL), row(V_DIM), row(CONF_CH), row(D_MODEL), row(D_MODEL),
                  _const_spec(wgo.shape), _const_spec(wco.shape), _const_spec(wout.shape),
                  _const_spec((1, D_MODEL)), _const_spec(w_router.shape), _const_spec((1, N_EXPERTS))],
        out_specs=[row(D_MODEL), pl.BlockSpec((tm * SLAB, LANES), lambda i: (i, 0)), row(N_EXPERTS), row(N_EXPERTS)],
        out_shape=[jax.ShapeDtypeStruct((n, D_MODEL), F32), jax.ShapeDtypeStruct((n * SLAB, LANES), F32),
                   jax.ShapeDtypeStruct((n, N_EXPERTS), jnp.int32), jax.ShapeDtypeStruct((n, N_EXPERTS), F32)],
        compiler_params=_params(("parallel",)),
        name="mix",
    )(x, o, cv, ga, gb, wgo, wco, wout, norm_ffn, w_router, b_router)


ROUTE_BLOCK_MAX = 16384


def _equal_blocks(n):
    for d in range(1, n // LANES + 1):
        if n % d == 0 and n // d <= ROUTE_BLOCK_MAX and (n // d) % LANES == 0:
            return d
    raise ValueError(f"cannot split {n} elements into lane-aligned blocks")


def _route_rows_kernel(pos_ref, rp_ref, *, n_init):
    s = pl.program_id(0)
    rows_per = rp_ref.shape[0] // n_init
    pairs_per = pos_ref.shape[-1]

    @pl.when(s < n_init)
    def _():
        base = s * rows_per

        def init(i, carry):
            for j in range(SUBLANES):
                rp_ref[base + i * SUBLANES + j] = -1
            return carry
        lax.fori_loop(0, rows_per // SUBLANES, init, 0)

    @pl.when(s >= n_init)
    def _():
        base = (s - n_init) * pairs_per

        def place(i, carry):
            for j in range(SUBLANES):
                p = i * SUBLANES + j
                rp_ref[pos_ref[0, p]] = base + p
            return carry
        lax.fori_loop(0, pairs_per // SUBLANES, place, 0)


def _route(top_i):
    n = top_i.shape[0]
    pairs = n * TOP_K
    n_tiles = pairs // MOE_TILE + N_EXPERTS
    e_flat = top_i.reshape(pairs)
    onehot = (e_flat[:, None] == jnp.arange(N_EXPERTS, dtype=jnp.int32)[None, :]).astype(jnp.int32)
    csum = jnp.cumsum(onehot, axis=0)
    rank = jnp.take_along_axis(csum, e_flat[:, None], axis=1)[:, 0] - 1
    tiles_per = (csum[-1] + MOE_TILE - 1) // MOE_TILE
    tile_end = jnp.cumsum(tiles_per)
    n_used = tile_end[-1]
    pos = (tile_end - tiles_per)[e_flat] * MOE_TILE + rank
    n_rows = n_tiles * MOE_TILE
    n_init, n_place = _equal_blocks(n_rows), _equal_blocks(pairs)
    row_pair = pl.pallas_call(
        functools.partial(_route_rows_kernel, n_init=n_init),
        grid=(n_init + n_place,),
        in_specs=[pl.BlockSpec((None, 1, pairs // n_place), lambda s: (jnp.maximum(s - n_init, 0), 0, 0),
                               memory_space=pltpu.SMEM)],
        out_specs=pl.BlockSpec(memory_space=pltpu.SMEM),
        out_shape=jax.ShapeDtypeStruct((n_rows,), jnp.int32),
        compiler_params=_params(("arbitrary",)),
        name="route_rows",
    )(pos.reshape(n_place, 1, pairs // n_place))
    row_pair = jnp.concatenate([jnp.full((MOE_TILE,), -1, jnp.int32), row_pair]).reshape(n_tiles + 1, 1, MOE_TILE)
    t = jnp.minimum(jnp.arange(n_tiles + 1, dtype=jnp.int32), n_used - 1)
    tile_expert = jnp.sum((tile_end[None, :] <= t[:, None]).astype(jnp.int32), axis=1)
    return tile_expert, n_used.reshape(1), row_pair


def _experts_kernel(te_ref, nu_ref, idx0_ref, idxn_ref, idxp_ref, x3_hbm, wgu_ref, bgu_ref, wd_ref, bd_ref,
                    y_hbm, x0, x1, y0, y1, wgu16, wd16, sem_in, sem_out):
    t = pl.program_id(0)
    n_used = nu_ref[0]
    xs, ys = (x0, x1), (y0, y1)
    tile_rows = MOE_TILE * SLAB
    pair_rows = y_hbm.shape[0] - 2 * tile_rows

    def start_gather(idx_ref, c):
        for i in range(MOE_TILE):
            tok = jnp.maximum(idx_ref[0, i], 0) // TOP_K
            pltpu.make_async_copy(x3_hbm.at[tok], xs[c].at[pl.ds(i * SLAB, SLAB), :],
                                  sem_in.at[c]).start(priority=i % 2)

    def start_scatter(idx_ref, c):
        for i in range(MOE_TILE):
            pair = idx_ref[0, i]
            row = jnp.where(pair >= 0, pair * SLAB, pair_rows + c * tile_rows + i * SLAB)
            pltpu.make_async_copy(ys[c].at[pl.ds(i * SLAB, SLAB), :],
                                  y_hbm.at[pl.ds(pl.multiple_of(row, SLAB), SLAB), :],
                                  sem_out.at[c]).start(priority=i % 2)

    def wait_tile(buf, sem):
        pltpu.make_async_copy(buf, buf, sem).wait()

    @pl.when(t == 0)
    def _():
        y0[...] = jnp.zeros(y0.shape, F32)
        y1[...] = jnp.zeros(y1.shape, F32)
        pltpu.make_async_copy(y0, y_hbm.at[pl.ds(pair_rows, tile_rows), :], sem_out.at[0]).start()
        start_gather(idx0_ref, 0)

    @pl.when((t < n_used) & ((t == 0) | (te_ref[t] != te_ref[jnp.maximum(t - 1, 0)])))
    def _():
        wgu16[...] = wgu_ref[...].astype(BF16)
        wd16[...] = wd_ref[...].astype(BF16)

    def step(c):
        o = 1 - c
        wait_tile(xs[c], sem_in.at[c])
        start_gather(idxn_ref, o)
        start_scatter(idxp_ref, o)
        x = jnp.concatenate([xs[c][pl.ds(s, MOE_TILE, stride=SLAB), :] for s in range(SLAB)], axis=1)
        gu = _dot(x.astype(BF16), wgu16[...]) + bgu_ref[...]
        gate = jnp.minimum(gu[:, :D_FF], SWIGLU_LIMIT)
        up = jnp.clip(gu[:, D_FF:], -SWIGLU_LIMIT, SWIGLU_LIMIT)
        hid = (up + 1.0) * (gate * _sigmoid(SWIGLU_ALPHA * gate))
        ye = _dot(hid.astype(BF16), wd16[...]) + bd_ref[...]
        wait_tile(ys[c], sem_out.at[c])
        for s in range(SLAB):
            ys[c][pl.ds(s, MOE_TILE, stride=SLAB), :] = ye[:, s * LANES:(s + 1) * LANES]

    def drain(c):
        o = 1 - c
        start_scatter(idxp_ref, o)
        wait_tile(ys[o], sem_out.at[o])
        wait_tile(ys[c], sem_out.at[c])
        wait_tile(xs[c], sem_in.at[c])

    for c in range(2):
        pl.when((t < n_used) & (t % 2 == c))(functools.partial(step, c))
        pl.when((t == n_used) & (t % 2 == c))(functools.partial(drain, c))


def _experts(hm3, tile_expert, n_used, row_pair, wgu, bgu, wd, bd):
    n_steps = row_pair.shape[0]
    n_tiles = n_steps - 1
    pairs = hm3.shape[0] // SLAB * TOP_K
    x3 = hm3.reshape(-1, SLAB, LANES)
    idx = lambda f: pl.BlockSpec((None, 1, MOE_TILE), f, memory_space=pltpu.SMEM)
    tile_buf = pltpu.VMEM((MOE_TILE * SLAB, LANES), F32)
    grid_spec = pltpu.PrefetchScalarGridSpec(
        num_scalar_prefetch=2,
        grid=(n_steps,),
        in_specs=[idx(lambda t, te, nu: (1, 0, 0)),
                  idx(lambda t, te, nu: (jnp.minimum(t + 2, n_tiles), 0, 0)),
                  idx(lambda t, te, nu: (t, 0, 0)),
                  pl.BlockSpec(memory_space=pl.ANY),
                  pl.BlockSpec((None, D_MODEL, 2 * D_FF), lambda t, te, nu: (te[t], 0, 0)),
                  pl.BlockSpec((None, 1, 2 * D_FF), lambda t, te, nu: (te[t], 0, 0)),
                  pl.BlockSpec((None, D_FF, D_MODEL), lambda t, te, nu: (te[t], 0, 0)),
                  pl.BlockSpec((None, 1, D_MODEL), lambda t, te, nu: (te[t], 0, 0))],
        out_specs=pl.BlockSpec(memory_space=pl.ANY),
        scratch_shapes=[tile_buf, tile_buf, tile_buf, tile_buf,
                        pltpu.VMEM((D_MODEL, 2 * D_FF), BF16), pltpu.VMEM((D_FF, D_MODEL), BF16),
                        pltpu.SemaphoreType.DMA((2,)), pltpu.SemaphoreType.DMA((2,))])
    return pl.pallas_call(
        _experts_kernel,
        grid_spec=grid_spec,
        out_shape=jax.ShapeDtypeStruct(((pairs + 2 * MOE_TILE) * SLAB, LANES), F32),
        compiler_params=_params(("arbitrary",)),
        name="experts",
    )(tile_expert, n_used, row_pair, row_pair, row_pair, x3, wgu, bgu, wd, bd)


def _ple_kernel(x1_ref, y_ref, topw_ref, p_ref, nple_ref, wg_ref, wp_ref, nfin_ref, out_ref):
    tm = x1_ref.shape[0]
    topw = topw_ref[...]
    moe = []
    for s in range(SLAB):
        acc = topw[:, 0:1] * y_ref[pl.ds(s, tm, stride=TOP_K * SLAB), :]
        for k in range(1, TOP_K):
            acc = acc + topw[:, k:k + 1] * y_ref[pl.ds(k * SLAB + s, tm, stride=TOP_K * SLAB), :]
        moe.append(acc)
    x2 = x1_ref[...] + jnp.concatenate(moe, axis=1)
    gate = _sigmoid(_bdot(_rms(x2, nple_ref[...]), wg_ref[...]))
    x3 = x2 + _bdot(p_ref[...], wp_ref[...]) * gate
    out_ref[...] = _rms(x3, nfin_ref[...])


def _ple(x1, y2d, topw, p, norm_ple, wg, wp, norm_final, tm, row_offset):
    n = x1.shape[0]
    off = row_offset // tm
    row = lambda w: pl.BlockSpec((tm, w), lambda i: (i, 0))
    return pl.pallas_call(
        _ple_kernel,
        grid=(n // tm,),
        in_specs=[row(D_MODEL), pl.BlockSpec((tm * TOP_K * SLAB, LANES), lambda i: (i + off, 0)), row(N_EXPERTS),
                  row(PLE_DIM), _const_spec((1, D_MODEL)), _const_spec(wg.shape), _const_spec(wp.shape),
                  _const_spec((1, D_MODEL))],
        out_specs=row(D_MODEL),
        out_shape=jax.ShapeDtypeStruct((n, D_MODEL), F32),
        compiler_params=_params(("parallel",)),
        name="ple",
    )(x1, y2d, topw, p, norm_ple, wg, wp, norm_final)


def kernel(x_prompt, x_sample, p_prompt, p_sample, state_gdn, state_qkv_conv, state_conf_conv, norm_mix, w_in, conv_qkv_w, a_log, dt_bias, gdn_norm, w_gdn_out, conf_dw_w, conf_dw_b, conf_ln_g, conf_ln_b, w_conf_out, w_out, norm_ffn, w_router, b_router, w_gate_up, b_gate_up, w_down, b_down, norm_ple, w_ple_gate, w_ple_proj, norm_final):
    assert w_in.shape[0] == 1, "single-layer trunk"
    bsz, seq, _ = x_prompt.shape
    n_p = bsz * seq
    n_s = x_sample.shape[0] * x_sample.shape[1]
    assert x_sample.shape[1] == 1

    w_in0 = w_in[0]
    wqkv = w_in0[:, :OFF_Z].astype(BF16)
    wz = w_in0[:, OFF_Z:OFF_BETA].astype(BF16)
    wba = w_in0[:, OFF_BETA:OFF_GLU].astype(BF16)
    wglu = w_in0[:, OFF_GLU:OFF_GATE].astype(BF16)
    wgate = w_in0[:, OFF_GATE:].astype(BF16)
    zeros_h = jnp.zeros((GDN_HEADS,), F32)
    alog16 = jnp.concatenate([zeros_h, a_log[0]]).reshape(1, 2 * GDN_HEADS)
    dtb16 = jnp.concatenate([zeros_h, dt_bias[0]]).reshape(1, 2 * GDN_HEADS)
    wgo = w_gdn_out[0].astype(BF16)
    wco = w_conf_out[0].astype(BF16)
    wout = w_out[0].astype(BF16)
    wpg = w_ple_gate[0].astype(BF16)
    wpp = w_ple_proj[0].astype(BF16)
    bgu = b_gate_up[0].reshape(N_EXPERTS, 1, 2 * D_FF)
    bd = b_down[0].reshape(N_EXPERTS, 1, D_MODEL)
    row = lambda v: v.reshape(1, -1)
    inproj_w = (row(norm_mix[0]), wqkv, wz, wba, wglu, wgate, alog16, dtb16)
    mix_w = (wgo, wco, wout, row(norm_ffn[0]), w_router[0], row(b_router[0]))
    ple_w = (row(norm_ple[0]), wpg, wpp, row(norm_final))
    conf_w = (conf_dw_w[0], row(conf_dw_b[0]), row(conf_ln_g[0]), row(conf_ln_b[0]))

    xp = x_prompt.reshape(n_p, D_MODEL)
    qkv_p, z_p, bg_p, u_p, ga_p, gb_p = _inproj(xp, *inproj_w, tm=256)
    qkv_p3 = qkv_p.reshape(bsz, seq, QKV_CH)
    u_p3 = u_p.reshape(bsz, seq, CONF_CH)
    o_p, s_p = _gdn_prompt(qkv_p3, z_p.reshape(bsz, seq, V_DIM), bg_p, conv_qkv_w[0], row(gdn_norm[0]))
    cv_p = _conf_prompt(u_p3, *conf_w)

    xs = x_sample.reshape(n_s, D_MODEL)
    qkv_s, z_s, bg_s, u_s, ga_s, gb_s = _inproj(xs, *inproj_w, tm=n_s)
    o_s, s_s = _gdn_sample(qkv_s, jnp.swapaxes(state_qkv_conv[0], 0, 1), z_s, bg_s, state_gdn[0],
                           conv_qkv_w[0], row(gdn_norm[0]))
    cv_s = _conf_sample(u_s, jnp.swapaxes(state_conf_conv[0], 0, 1), *conf_w)

    tm_p = 512
    x1_p, hm3_p, ti_p, tw_p = _mix(xp, o_p.reshape(n_p, V_DIM), cv_p.reshape(n_p, CONF_CH), ga_p, gb_p, *mix_w,
                                   tm=tm_p)
    x1_s, hm3_s, ti_s, tw_s = _mix(xs, o_s, cv_s, ga_s, gb_s, *mix_w, tm=n_s)
    routing = _route(jnp.concatenate([ti_p, ti_s], axis=0)[:, :TOP_K])
    y2d = _experts(jnp.concatenate([hm3_p, hm3_s], axis=0), *routing, w_gate_up[0], bgu, w_down[0], bd)
    y_p = _ple(x1_p, y2d, tw_p, p_prompt[0].reshape(n_p, PLE_DIM), *ple_w, tm=tm_p, row_offset=0)
    y_s = _ple(x1_s, y2d, tw_s, p_sample[0].reshape(n_s, PLE_DIM), *ple_w, tm=n_s, row_offset=n_p)

    new_qkv_s = jnp.concatenate([state_qkv_conv[0][:, 1:], qkv_s[:, None, :]], axis=1)
    new_conf_s = jnp.concatenate([state_conf_conv[0][:, 1:], u_s[:, None, :]], axis=1)
    return (y_p.reshape(bsz, seq, D_MODEL), y_s.reshape(n_s, 1, D_MODEL),
            s_p[None], qkv_p3[:, seq - (SHORT_CONV - 1):][None], u_p3[:, seq - (CONF_K - 1):][None],
            s_s[None], new_qkv_s[None], new_conf_s[None])
```

```python
import functools

import jax
import jax.numpy as jnp
from jax import lax
from jax.experimental import pallas as pl
from jax.experimental.pallas import tpu as pltpu

F32 = jnp.float32
BF16 = jnp.bfloat16

D_MODEL = 1024
GDN_HEADS = 8
GDN_DK = 128
GDN_DV = 128
QK_DIM = GDN_HEADS * GDN_DK
V_DIM = GDN_HEADS * GDN_DV
QKV_CH = 2 * QK_DIM + V_DIM
SHORT_CONV = 4
CHUNK = 64
CONF_CH = D_MODEL // 2
CONF_K = 31
N_EXPERTS = 32
TOP_K = 4
D_FF = D_MODEL
SWIGLU_LIMIT = 7.0
SWIGLU_ALPHA = 1.702
PLE_DIM = 256
NORM_EPS = 1e-6
LN_EPS = 1e-5

OFF_Z = QKV_CH
OFF_BETA = OFF_Z + V_DIM
OFF_GLU = OFF_BETA + 2 * GDN_HEADS
OFF_GATE = OFF_GLU + 2 * CONF_CH

VMEM_LIMIT_BYTES = 52 * 1024 * 1024
SUBLANES = 8
LANES = 128
CONF_HALO = 32


def _dot(a, b):
    return jnp.dot(a, b, preferred_element_type=F32)


def _bdot(a, b):
    return jnp.dot(a.astype(BF16), b.astype(BF16), preferred_element_type=F32)


def _split3(x):
    hi = x.astype(BF16)
    r1 = x - hi.astype(F32)
    mid = r1.astype(BF16)
    lo = (r1 - mid.astype(F32)).astype(BF16)
    return hi, mid, lo


def _mask_dot(mask, x, mask_first):
    m = jnp.where(mask, 1.0, 0.0).astype(BF16)
    parts = [_dot(m, p) if mask_first else _dot(p, m) for p in _split3(x)]
    return parts[0] + (parts[1] + parts[2])


def _sigmoid(x):
    return 1.0 / (1.0 + jnp.exp(-x))


def _silu(x):
    return x * _sigmoid(x)


def _rms(x, g):
    return x * lax.rsqrt(jnp.mean(x * x, axis=-1, keepdims=True) + NORM_EPS) * g


def _const_spec(shape):
    nd = len(shape)
    return pl.BlockSpec(shape, lambda *_: (0,) * nd, pipeline_mode=pl.Buffered(1))


def _params(sem):
    return pltpu.CompilerParams(dimension_semantics=sem, vmem_limit_bytes=VMEM_LIMIT_BYTES)


def _inproj_kernel(x_ref, nw_ref, wqkv_ref, wz_ref, wba_ref, wglu_ref, wgate_ref, alog_ref, dtb_ref,
                   qkv_ref, z_ref, bg_ref, u_ref, ga_ref, gb_ref):
    a = _rms(x_ref[...], nw_ref[...]).astype(BF16)
    for c in range(QKV_CH // D_MODEL):
        cols = slice(c * D_MODEL, (c + 1) * D_MODEL)
        qkv_ref[:, cols] = _dot(a, wqkv_ref[:, cols])
    z_ref[...] = _dot(a, wz_ref[...])
    ba = _dot(a, wba_ref[...])
    lane = lax.broadcasted_iota(jnp.int32, ba.shape, 1)
    t = ba + dtb_ref[...]
    softplus = jnp.maximum(t, 0.0) + jnp.log(1.0 + jnp.exp(-jnp.abs(t)))
    bg_ref[...] = jnp.where(lane < GDN_HEADS, _sigmoid(ba), -jnp.exp(alog_ref[...]) * softplus)
    glu = _dot(a, wglu_ref[...])
    u_ref[...] = glu[:, :CONF_CH] * _sigmoid(glu[:, CONF_CH:])
    ga_ref[...] = _sigmoid(_dot(a, wgate_ref[:, :D_MODEL]))
    gb_ref[...] = _sigmoid(_dot(a, wgate_ref[:, D_MODEL:]))


def _inproj(x, norm_w, wqkv, wz, wba, wglu, wgate, alog16, dtb16, tm):
    n = x.shape[0]
    row = lambda w: pl.BlockSpec((tm, w), lambda i: (i, 0))
    return pl.pallas_call(
        _inproj_kernel,
        grid=(n // tm,),
        in_specs=[row(D_MODEL), _const_spec((1, D_MODEL)), _const_spec(wqkv.shape), _const_spec(wz.shape),
                  _const_spec(wba.shape), _const_spec(wglu.shape), _const_spec(wgate.shape),
                  _const_spec((1, 2 * GDN_HEADS)), _const_spec((1, 2 * GDN_HEADS))],
        out_specs=[row(QKV_CH), row(V_DIM), row(2 * GDN_HEADS), row(CONF_CH), row(D_MODEL), row(D_MODEL)],
        out_shape=[jax.ShapeDtypeStruct((n, w), F32)
                   for w in (QKV_CH, V_DIM, 2 * GDN_HEADS, CONF_CH, D_MODEL, D_MODEL)],
        compiler_params=_params(("parallel",)),
        name="inproj",
    )(x, norm_w, wqkv, wz, wba, wglu, wgate, alog16, dtb16)


GDN_GROUP = 4
PREP_CHUNKS = 2
GROUP_ROWS = GDN_GROUP * CHUNK
N_GROUPS = GDN_HEADS // GDN_GROUP


def _block_unit_lower_inverse(a, eye):
    b = -a
    t = eye + b
    p = _bdot(b, b)
    covered = 2
    while covered < CHUNK:
        covered *= 2
        if covered < CHUNK:
            r = _bdot(jnp.concatenate([t, p], axis=0), p)
            t = t + r[:GROUP_ROWS]
            p = r[GROUP_ROWS:]
        else:
            t = t + _bdot(t, p)
    return t


def _gdn_prep_kernel(qkv_ref, halo_ref, bg_ref, bgt_ref, cw_ref, wqg_ref, kdt_ref, qk_ref, u_ref, gl_ref, ext_ref):
    c = pl.program_id(1)
    ext_ref[0:SUBLANES, :] = jnp.where(c > 0, halo_ref[...], 0.0)
    ext_ref[SUBLANES:SUBLANES + PREP_CHUNKS * CHUNK, :] = qkv_ref[...]
    for n in range(PREP_CHUNKS):
        _gdn_prep_chunk(n, bg_ref, bgt_ref, cw_ref, wqg_ref, kdt_ref, qk_ref, u_ref, gl_ref, ext_ref)


def _gdn_prep_chunk(n, bg_ref, bgt_ref, cw_ref, wqg_ref, kdt_ref, qk_ref, u_ref, gl_ref, ext_ref):
    base = SUBLANES + n * CHUNK

    def convsilu(c0):
        cols = slice(c0, c0 + LANES)
        acc = cw_ref[SHORT_CONV - 1:SHORT_CONV, cols] * ext_ref[base:base + CHUNK, cols]
        for j in range(SHORT_CONV - 1):
            r0 = base - (SHORT_CONV - 1) + j
            acc = acc + cw_ref[j:j + 1, cols] * ext_ref[r0:r0 + CHUNK, cols]
        return _silu(acc)

    r64 = lax.broadcasted_iota(jnp.int32, (CHUNK, CHUNK), 0)
    c64 = lax.broadcasted_iota(jnp.int32, (CHUNK, CHUNK), 1)
    bg = bg_ref[n]
    gc_all = _mask_dot(r64 >= c64, bg, mask_first=True)
    gct_all = _mask_dot(r64 <= c64, bgt_ref[n], mask_first=False)
    gl_ref[n] = gc_all[CHUNK - 1:CHUNK, :]

    row = lax.broadcasted_iota(jnp.int32, (GROUP_ROWS, GROUP_ROWS), 0)
    col = lax.broadcasted_iota(jnp.int32, (GROUP_ROWS, GROUP_ROWS), 1)
    blk = row - col + (col & (CHUNK - 1))
    same = (blk >= 0) & (blk < CHUNK)
    tril = same & (row >= col)
    strict = same & (row > col)
    eye = (row == col).astype(F32)
    nt = (((1,), (1,)), ((), ()))

    for g in range(N_GROUPS):
        heads = range(g * GDN_GROUP, (g + 1) * GDN_GROUP)
        stack = lambda f: jnp.concatenate([f(h) for h in heads], axis=0)
        q = stack(lambda h: convsilu(h * GDN_DK))
        k = stack(lambda h: convsilu(QK_DIM + h * GDN_DK))
        v = stack(lambda h: convsilu(2 * QK_DIM + h * GDN_DV))
        q = q * lax.rsqrt(jnp.sum(q * q, axis=-1, keepdims=True) + NORM_EPS) * (GDN_DK ** -0.5)
        k = k * lax.rsqrt(jnp.sum(k * k, axis=-1, keepdims=True) + NORM_EPS)
        beta = stack(lambda h: bg[:, h:h + 1])
        gc = stack(lambda h: gc_all[:, GDN_HEADS + h:GDN_HEADS + h + 1])
        g_last = stack(lambda h: jnp.broadcast_to(
            gc_all[CHUNK - 1:CHUNK, GDN_HEADS + h:GDN_HEADS + h + 1], (CHUNK, 1)))
        gct = jnp.concatenate([gct_all[GDN_HEADS + h:GDN_HEADS + h + 1, :] for h in heads], axis=1)
        decay = jnp.exp(jnp.where(tril, gc - gct, -jnp.inf))
        kb = k * beta
        kq_k = lax.dot_general(jnp.concatenate([kb.astype(BF16), q.astype(BF16)], axis=0), k.astype(BF16), nt,
                               preferred_element_type=F32)
        kk = kq_k[:GROUP_ROWS]
        t_inv = _block_unit_lower_inverse(jnp.where(strict, kk * decay, 0.0), eye)
        eg = jnp.exp(gc)
        uw = _bdot(t_inv, jnp.concatenate([v * beta, kb * eg], axis=1))
        u_ref[n, g] = uw[:, :GDN_DV]
        w = uw[:, GDN_DV:].astype(BF16)
        qg = (q * eg).astype(BF16)
        for i, h in enumerate(heads):
            rows = slice(i * CHUNK, (i + 1) * CHUNK)
            wqg_ref[n, h, 0:CHUNK, :] = w[rows]
            wqg_ref[n, h, CHUNK:2 * CHUNK, :] = qg[rows]
        qk_ref[n, g] = (kq_k[GROUP_ROWS:] * decay).astype(BF16)
        kdt_ref[n, g] = (k * jnp.exp(g_last - gc)).T.astype(BF16)


def _gdn_scan_kernel(wqg_ref, kdt_ref, qk_ref, u_ref, gl_ref, z_ref, gn_ref, o_ref, s_out_ref, s_ref):
    c = pl.program_id(1)

    @pl.when(c == 0)
    def _():
        s_ref[...] = jnp.zeros(s_ref.shape, F32)

    zeros = jnp.zeros((CHUNK, GDN_DV), BF16)
    for n in range(PREP_CHUNKS):
        dl_all = jnp.exp(gl_ref[n])
        trows = slice(n * CHUNK, (n + 1) * CHUNK)
        for g in range(N_GROUPS):
            heads = list(range(g * GDN_GROUP, (g + 1) * GDN_GROUP))
            v_new, q_s = [], []
            for i, h in enumerate(heads):
                r = _dot(wqg_ref[n, h], s_ref[h].astype(BF16))
                v_new.append((u_ref[n, g, i * CHUNK:(i + 1) * CHUNK, :] - r[:CHUNK]).astype(BF16))
                q_s.append(r[CHUNK:])
            o_all = jnp.concatenate(q_s, axis=0) + _dot(qk_ref[n, g], jnp.concatenate(v_new, axis=0))
            for i, h in enumerate(heads):
                vm = jnp.concatenate([v_new[j] if j == i else zeros for j in range(GDN_GROUP)], axis=0)
                s_ref[h] = s_ref[h] * dl_all[:, GDN_HEADS + h:GDN_HEADS + h + 1] + _dot(kdt_ref[n, g], vm)
                o = _rms(o_all[i * CHUNK:(i + 1) * CHUNK], gn_ref[...])
                cols = slice(h * GDN_DV, (h + 1) * GDN_DV)
                o_ref[trows, cols] = (o * _silu(z_ref[trows, cols])).astype(o_ref.dtype)

    @pl.when(c == pl.num_programs(1) - 1)
    def _():
        s_out_ref[...] = s_ref[...]


def _gdn_prompt(qkv, z, bg, conv_w, gdn_norm):
    b, l, _ = qkv.shape
    nc = l // CHUNK
    bg4 = bg.reshape(b, nc, CHUNK, 2 * GDN_HEADS)
    bgt4 = jnp.swapaxes(bg4, 2, 3)
    halo_blocks = CHUNK // SUBLANES
    grid = (b, nc // PREP_CHUNKS)
    seq = lambda w: pl.BlockSpec((None, PREP_CHUNKS * CHUNK, w), lambda i, c: (i, c, 0))
    chunks = lambda *s: pl.BlockSpec((None, PREP_CHUNKS) + s, lambda i, c: (i, c) + (0,) * len(s))
    inter_shapes = [((GDN_HEADS, 2 * CHUNK, GDN_DK), BF16), ((N_GROUPS, GDN_DK, GROUP_ROWS), BF16),
                    ((N_GROUPS, GROUP_ROWS, GROUP_ROWS), BF16), ((N_GROUPS, GROUP_ROWS, GDN_DV), F32),
                    ((1, 2 * GDN_HEADS), F32)]
    inter_specs = [chunks(*s) for s, _ in inter_shapes]
    wqg, kdt, qk, u, gl = pl.pallas_call(
        _gdn_prep_kernel,
        grid=grid,
        in_specs=[seq(QKV_CH),
                  pl.BlockSpec((None, SUBLANES, QKV_CH),
                               lambda i, c: (i, jnp.maximum(c * PREP_CHUNKS * halo_blocks - 1, 0), 0)),
                  chunks(CHUNK, 2 * GDN_HEADS), chunks(2 * GDN_HEADS, CHUNK), _const_spec(conv_w.shape)],
        out_specs=inter_specs,
        out_shape=[jax.ShapeDtypeStruct((b, nc) + s, d) for s, d in inter_shapes],
        scratch_shapes=[pltpu.VMEM((SUBLANES + PREP_CHUNKS * CHUNK, QKV_CH), F32)],
        compiler_params=_params(("parallel", "parallel")),
        name="gdn_prep",
    )(qkv, qkv, bg4, bgt4, conv_w)
    return pl.pallas_call(
        _gdn_scan_kernel,
        grid=grid,
        in_specs=inter_specs + [seq(V_DIM), _const_spec((1, GDN_DV))],
        out_specs=[seq(V_DIM),
                   pl.BlockSpec((None, GDN_HEADS, GDN_DK, GDN_DV), lambda i, c: (i, 0, 0, 0))],
        out_shape=[jax.ShapeDtypeStruct((b, l, V_DIM), BF16),
                   jax.ShapeDtypeStruct((b, GDN_HEADS, GDN_DK, GDN_DV), F32)],
        scratch_shapes=[pltpu.VMEM((GDN_HEADS, GDN_DK, GDN_DV), F32)],
        compiler_params=_params(("parallel", "arbitrary")),
        name="gdn_scan",
    )(wqg, kdt, qk, u, gl, z, gdn_norm)


GDN_SAMPLE_ROWS = 8


def _gdn_sample_kernel(raw_ref, cs_ref, z_ref, bg_ref, s_in_ref, cw_ref, gn_ref, o_ref, s_out_ref):
    conv = cw_ref[SHORT_CONV - 1:SHORT_CONV, :] * raw_ref[...]
    for j in range(SHORT_CONV - 1):
        conv = conv + cw_ref[j:j + 1, :] * cs_ref[j]
    qkv = _silu(conv)
    bg = bg_ref[...]
    for h in range(GDN_HEADS):
        q = qkv[:, h * GDN_DK:(h + 1) * GDN_DK]
        k = qkv[:, QK_DIM + h * GDN_DK:QK_DIM + (h + 1) * GDN_DK]
        v = qkv[:, 2 * QK_DIM + h * GDN_DV:2 * QK_DIM + (h + 1) * GDN_DV]
        q = q * lax.rsqrt(jnp.sum(q * q, axis=-1, keepdims=True) + NORM_EPS) * (GDN_DK ** -0.5)
        k = k * lax.rsqrt(jnp.sum(k * k, axis=-1, keepdims=True) + NORM_EPS)
        beta = bg[:, h:h + 1]
        eg = jnp.exp(bg[:, GDN_HEADS + h:GDN_HEADS + h + 1])
        u = v * beta
        qk = jnp.sum(q * k, axis=-1, keepdims=True)
        w_t = (k * (beta * eg)).T
        qg_t = (q * eg).T
        k_t = k.T
        o_rows = []
        for r in range(GDN_SAMPLE_ROWS):
            s = s_in_ref[r, h]
            w_s = jnp.sum(w_t[:, r:r + 1] * s, axis=0, keepdims=True)
            q_s = jnp.sum(qg_t[:, r:r + 1] * s, axis=0, keepdims=True)
            v_new = u[r:r + 1] - w_s
            o_rows.append(q_s + qk[r:r + 1] * v_new)
            s_out_ref[r, h] = s * eg[r:r + 1] + k_t[:, r:r + 1] * v_new
        o = _rms(jnp.concatenate(o_rows, axis=0), gn_ref[...])
        cols = slice(h * GDN_DV, (h + 1) * GDN_DV)
        o_ref[:, cols] = (o * _silu(z_ref[:, cols])).astype(o_ref.dtype)


def _gdn_sample(raw, conv_state_t, z, bg, state, conv_w, gdn_norm):
    n = raw.shape[0]
    rb = GDN_SAMPLE_ROWS
    row = lambda w: pl.BlockSpec((rb, w), lambda i: (i, 0))
    st = pl.BlockSpec((rb, GDN_HEADS, GDN_DK, GDN_DV), lambda i: (i, 0, 0, 0))
    return pl.pallas_call(
        _gdn_sample_kernel,
        grid=(n // rb,),
        in_specs=[row(QKV_CH), pl.BlockSpec((SHORT_CONV - 1, rb, QKV_CH), lambda i: (0, i, 0)),
                  row(V_DIM), row(2 * GDN_HEADS), st, _const_spec(conv_w.shape), _const_spec((1, GDN_DV))],
        out_specs=[row(V_DIM), st],
        out_shape=[jax.ShapeDtypeStruct((n, V_DIM), BF16), jax.ShapeDtypeStruct(state.shape, F32)],
        compiler_params=_params(("parallel",)),
        name="gdn_sample",
    )(raw, conv_state_t, z, bg, state, conv_w, gdn_norm)


CONF_TILE = 256
CONF_ROWS = 64


def _ln_silu(x, g, b):
    mu = jnp.mean(x, axis=-1, keepdims=True)
    xc = x - mu
    var = jnp.mean(xc * xc, axis=-1, keepdims=True)
    return _silu(xc * lax.rsqrt(var + LN_EPS) * g + b)


def _conf_prompt_kernel(u_ref, w_ref, b_ref, lng_ref, lnb_ref, cv_ref, ubuf_ref, sh_ref, acc_ref):
    @pl.when(pl.program_id(1) == 0)
    def _():
        ubuf_ref[0:CONF_HALO, :] = jnp.zeros((CONF_HALO, CONF_CH), F32)

    ubuf_ref[CONF_HALO:CONF_HALO + CONF_TILE, :] = u_ref[...]
    span = CONF_HALO + CONF_TILE - SUBLANES
    for k in range(1, SUBLANES):
        sh_ref[k - 1, 0:span, :] = ubuf_ref[k:k + span, :]
    base = CONF_HALO - (CONF_K - 1)
    for rb in range(CONF_TILE // CONF_ROWS):
        for cb in range(CONF_CH // LANES):
            cols = slice(cb * LANES, (cb + 1) * LANES)
            acc = b_ref[:, cols]
            for j in range(CONF_K):
                k = (base + j) % SUBLANES
                r0 = base + j - k + rb * CONF_ROWS
                rows = slice(r0, r0 + CONF_ROWS)
                src = ubuf_ref[rows, cols] if k == 0 else sh_ref[k - 1, rows, cols]
                acc = acc + w_ref[j:j + 1, cols] * src
            acc_ref[rb * CONF_ROWS:(rb + 1) * CONF_ROWS, cols] = acc
    ubuf_ref[0:CONF_HALO, :] = ubuf_ref[CONF_TILE:CONF_TILE + CONF_HALO, :]
    cv_ref[...] = _ln_silu(acc_ref[...], lng_ref[...], lnb_ref[...]).astype(cv_ref.dtype)


def _conf_prompt(u, w, b, ln_g, ln_b):
    bsz, l, _ = u.shape
    seq = pl.BlockSpec((None, CONF_TILE, CONF_CH), lambda i, t: (i, t, 0))
    return pl.pallas_call(
        _conf_prompt_kernel,
        grid=(bsz, l // CONF_TILE),
        in_specs=[seq, _const_spec(w.shape), _const_spec((1, CONF_CH)), _const_spec((1, CONF_CH)),
                  _const_spec((1, CONF_CH))],
        out_specs=seq,
        out_shape=jax.ShapeDtypeStruct((bsz, l, CONF_CH), BF16),
        scratch_shapes=[pltpu.VMEM((CONF_HALO + CONF_TILE, CONF_CH), F32),
                        pltpu.VMEM((SUBLANES - 1, CONF_HALO + CONF_TILE - SUBLANES, CONF_CH), F32),
                        pltpu.VMEM((CONF_TILE, CONF_CH), F32)],
        compiler_params=_params(("parallel", "arbitrary")),
        name="conf_prompt",
    )(u, w, b, ln_g, ln_b)


def _conf_sample_kernel(u_ref, cs_ref, w_ref, b_ref, lng_ref, lnb_ref, cv_ref):
    acc = w_ref[CONF_K - 1:CONF_K, :] * u_ref[...] + b_ref[...]
    for j in range(CONF_K - 1):
        acc = acc + w_ref[j:j + 1, :] * cs_ref[j]
    cv_ref[...] = _ln_silu(acc, lng_ref[...], lnb_ref[...]).astype(cv_ref.dtype)


def _conf_sample(u, conf_state_t, w, b, ln_g, ln_b):
    n = u.shape[0]
    return pl.pallas_call(
        _conf_sample_kernel,
        out_shape=jax.ShapeDtypeStruct((n, CONF_CH), BF16),
        compiler_params=pltpu.CompilerParams(vmem_limit_bytes=VMEM_LIMIT_BYTES),
        name="conf_sample",
    )(u, conf_state_t, w, b, ln_g, ln_b)


MOE_TILE = 256
SLAB = D_MODEL // LANES


def _mix_kernel(x_ref, o_ref, cv_ref, ga_ref, gb_ref, wgo_ref, wco_ref, wout_ref, nffn_ref, wr_ref, br_ref,
                x1_ref, hm3_ref, topi_ref, topw_ref):
    tm = x_ref.shape[0]
    branch_a = _dot(o_ref[...], wgo_ref[...])
    branch_b = _dot(cv_ref[...], wco_ref[...])
    merged = ga_ref[...] * branch_a + gb_ref[...] * branch_b
    x1 = x_ref[...] + _bdot(merged, wout_ref[...])
    x1_ref[...] = x1
    hm = _rms(x1, nffn_ref[...])
    for s in range(SLAB):
        hm3_ref[pl.ds(s, tm, stride=SLAB), :] = hm[:, s * LANES:(s + 1) * LANES]
    hm_hi = hm.astype(BF16)
    hm_lo = (hm - hm_hi.astype(F32)).astype(BF16)
    by_hi = _dot(hm_hi, wr_ref[...])
    logits = (by_hi[:, :N_EXPERTS] + (by_hi[:, N_EXPERTS:] + _dot(hm_lo, wr_ref[:, :N_EXPERTS]))) + br_ref[...]
    lane = lax.broadcasted_iota(jnp.int32, logits.shape, 1)
    work = logits
    top_v = []
    top_i = jnp.zeros(logits.shape, jnp.int32)
    for k in range(TOP_K):
        m = jnp.max(work, axis=-1, keepdims=True)
        first = jnp.min(jnp.where(work == m, lane, N_EXPERTS), axis=-1, keepdims=True)
        top_v.append(m)
        top_i = jnp.where(lane == k, first, top_i)
        work = jnp.where(lane == first, -jnp.inf, work)
    ex = [jnp.exp(v - top_v[0]) for v in top_v]
    den = ex[0]
    for e in ex[1:]:
        den = den + e
    top_w = jnp.zeros(logits.shape, F32)
    for k, e in enumerate(ex):
        top_w = jnp.where(lane == k, e / den, top_w)
    topi_ref[...] = top_i
    topw_ref[...] = top_w


def _mix(x, o, cv, ga, gb, wgo, wco, wout, norm_ffn, w_router, b_router, tm):
    n = x.shape[0]
    row = lambda w: pl.BlockSpec((tm, w), lambda i: (i, 0))
    return pl.pallas_call(
        _mix_kernel,
        grid=(n // tm,),
        in_specs=[row(D_MODEL), row(V_DIM), row(CONF_CH), row(D_MODEL), row(D_MODEL),
                  _const_spec(wgo.shape), _const_spec(wco.shape), _const_spec(wout.shape),
                  _const_spec((1, D_MODEL)), _const_spec(w_router.shape), _const_spec((1, N_EXPERTS))],
        out_specs=[row(D_MODEL), pl.BlockSpec((tm * SLAB, LANES), lambda i: (i, 0)), row(N_EXPERTS), row(N_EXPERTS)],
        out_shape=[jax.ShapeDtypeStruct((n, D_MODEL), F32), jax.ShapeDtypeStruct((n * SLAB, LANES), F32),
                   jax.ShapeDtypeStruct((n, N_EXPERTS), jnp.int32), jax.ShapeDtypeStruct((n, N_EXPERTS), F32)],
        compiler_params=_params(("parallel",)),
        name="mix",
    )(x, o, cv, ga, gb, wgo, wco, wout, norm_ffn, w_router, b_router)


ROUTE_BLOCK_MAX = 16384


def _equal_blocks(n):
    for d in range(1, n // LANES + 1):
        if n % d == 0 and n // d <= ROUTE_BLOCK_MAX and (n // d) % LANES == 0:
            return d
    raise ValueError(f"cannot split {n} elements into lane-aligned blocks")


def _route_rows_kernel(pos_ref, rp_ref, *, n_init):
    s = pl.program_id(0)
    rows_per = rp_ref.shape[0] // n_init
    pairs_per = pos_ref.shape[-1]

    @pl.when(s < n_init)
    def _():
        base = s * rows_per

        def init(i, carry):
            for j in range(SUBLANES):
                rp_ref[base + i * SUBLANES + j] = -1
            return carry
        lax.fori_loop(0, rows_per // SUBLANES, init, 0)

    @pl.when(s >= n_init)
    def _():
        base = (s - n_init) * pairs_per

        def place(i, carry):
            for j in range(SUBLANES):
                p = i * SUBLANES + j
                rp_ref[pos_ref[0, p]] = base + p
            return carry
        lax.fori_loop(0, pairs_per // SUBLANES, place, 0)


def _route(top_i):
    n = top_i.shape[0]
    pairs = n * TOP_K
    n_tiles = pairs // MOE_TILE + N_EXPERTS
    e_flat = top_i.reshape(pairs)
    onehot = (e_flat[:, None] == jnp.arange(N_EXPERTS, dtype=jnp.int32)[None, :]).astype(jnp.int32)
    csum = jnp.cumsum(onehot, axis=0)
    rank = jnp.take_along_axis(csum, e_flat[:, None], axis=1)[:, 0] - 1
    tiles_per = (csum[-1] + MOE_TILE - 1) // MOE_TILE
    tile_end = jnp.cumsum(tiles_per)
    n_used = tile_end[-1]
    pos = (tile_end - tiles_per)[e_flat] * MOE_TILE + rank
    n_rows = n_tiles * MOE_TILE
    n_init, n_place = _equal_blocks(n_rows), _equal_blocks(pairs)
    row_pair = pl.pallas_call(
        functools.partial(_route_rows_kernel, n_init=n_init),
        grid=(n_init + n_place,),
        in_specs=[pl.BlockSpec((None, 1, pairs // n_place), lambda s: (jnp.maximum(s - n_init, 0), 0, 0),
                               memory_space=pltpu.SMEM)],
        out_specs=pl.BlockSpec(memory_space=pltpu.SMEM),
        out_shape=jax.ShapeDtypeStruct((n_rows,), jnp.int32),
        compiler_params=_params(("arbitrary",)),
        name="route_rows",
    )(pos.reshape(n_place, 1, pairs // n_place))
    row_pair = jnp.concatenate([jnp.full((MOE_TILE,), -1, jnp.int32), row_pair]).reshape(n_tiles + 1, 1, MOE_TILE)
    t = jnp.minimum(jnp.arange(n_tiles + 1, dtype=jnp.int32), n_used - 1)
    tile_expert = jnp.sum((tile_end[None, :] <= t[:, None]).astype(jnp.int32), axis=1)
    return tile_expert, n_used.reshape(1), row_pair


def _experts_kernel(te_ref, nu_ref, idx0_ref, idxn_ref, idxp_ref, x3_hbm, wgu_ref, bgu_ref, wd_ref, bd_ref,
                    y_hbm, x0, x1, y0, y1, wgu16, wd16, sem_in, sem_out):
    t = pl.program_id(0)
    n_used = nu_ref[0]
    xs, ys = (x0, x1), (y0, y1)
    tile_rows = MOE_TILE * SLAB
    pair_rows = y_hbm.shape[0] - 2 * tile_rows

    def start_gather(idx_ref, c):
        for i in range(MOE_TILE):
            tok = jnp.maximum(idx_ref[0, i], 0) // TOP_K
            pltpu.make_async_copy(x3_hbm.at[tok], xs[c].at[pl.ds(i * SLAB, SLAB), :],
                                  sem_in.at[c]).start(priority=i % 2)

    def start_scatter(idx_ref, c):
        for i in range(MOE_TILE):
            pair = idx_ref[0, i]
            row = jnp.where(pair >= 0, pair * SLAB, pair_rows + c * tile_rows + i * SLAB)
            pltpu.make_async_copy(ys[c].at[pl.ds(i * SLAB, SLAB), :],
                                  y_hbm.at[pl.ds(pl.multiple_of(row, SLAB), SLAB), :],
                                  sem_out.at[c]).start(priority=i % 2)

    def wait_tile(buf, sem):
        pltpu.make_async_copy(buf, buf, sem).wait()

    @pl.when(t == 0)
    def _():
        y0[...] = jnp.zeros(y0.shape, F32)
        y1[...] = jnp.zeros(y1.shape, F32)
        pltpu.make_async_copy(y0, y_hbm.at[pl.ds(pair_rows, tile_rows), :], sem_out.at[0]).start()
        start_gather(idx0_ref, 0)

    @pl.when((t < n_used) & ((t == 0) | (te_ref[t] != te_ref[jnp.maximum(t - 1, 0)])))
    def _():
        wgu16[...] = wgu_ref[...].astype(BF16)
        wd16[...] = wd_ref[...].astype(BF16)

    def step(c):
        o = 1 - c
        wait_tile(xs[c], sem_in.at[c])
        start_gather(idxn_ref, o)
        start_scatter(idxp_ref, o)
        x = jnp.concatenate([xs[c][pl.ds(s, MOE_TILE, stride=SLAB), :] for s in range(SLAB)], axis=1)
        gu = _dot(x.astype(BF16), wgu16[...]) + bgu_ref[...]
        gate = jnp.minimum(gu[:, :D_FF], SWIGLU_LIMIT)
        up = jnp.clip(gu[:, D_FF:], -SWIGLU_LIMIT, SWIGLU_LIMIT)
        hid = (up + 1.0) * (gate * _sigmoid(SWIGLU_ALPHA * gate))
        ye = _dot(hid.astype(BF16), wd16[...]) + bd_ref[...]
        wait_tile(ys[c], sem_out.at[c])
        for s in range(SLAB):
            ys[c][pl.ds(s, MOE_TILE, stride=SLAB), :] = ye[:, s * LANES:(s + 1) * LANES]

    def drain(c):
        o = 1 - c
        start_scatter(idxp_ref, o)
        wait_tile(ys[o], sem_out.at[o])
        wait_tile(ys[c], sem_out.at[c])
        wait_tile(xs[c], sem_in.at[c])

    for c in range(2):
        pl.when((t < n_used) & (t % 2 == c))(functools.partial(step, c))
        pl.when((t == n_used) & (t % 2 == c))(functools.partial(drain, c))


def _experts(hm3, tile_expert, n_used, row_pair, wgu, bgu, wd, bd):
    n_steps = row_pair.shape[0]
    n_tiles = n_steps - 1
    pairs = hm3.shape[0] // SLAB * TOP_K
    x3 = hm3.reshape(-1, SLAB, LANES)
    idx = lambda f: pl.BlockSpec((None, 1, MOE_TILE), f, memory_space=pltpu.SMEM)
    tile_buf = pltpu.VMEM((MOE_TILE * SLAB, LANES), F32)
    grid_spec = pltpu.PrefetchScalarGridSpec(
        num_scalar_prefetch=2,
        grid=(n_steps,),
        in_specs=[idx(lambda t, te, nu: (1, 0, 0)),
                  idx(lambda t, te, nu: (jnp.minimum(t + 2, n_tiles), 0, 0)),
                  idx(lambda t, te, nu: (t, 0, 0)),
                  pl.BlockSpec(memory_space=pl.ANY),
                  pl.BlockSpec((None, D_MODEL, 2 * D_FF), lambda t, te, nu: (te[t], 0, 0)),
                  pl.BlockSpec((None, 1, 2 * D_FF), lambda t, te, nu: (te[t], 0, 0)),
                  pl.BlockSpec((None, D_FF, D_MODEL), lambda t, te, nu: (te[t], 0, 0)),
                  pl.BlockSpec((None, 1, D_MODEL), lambda t, te, nu: (te[t], 0, 0))],
        out_specs=pl.BlockSpec(memory_space=pl.ANY),
        scratch_shapes=[tile_buf, tile_buf, tile_buf, tile_buf,
                        pltpu.VMEM((D_MODEL, 2 * D_FF), BF16), pltpu.VMEM((D_FF, D_MODEL), BF16),
                        pltpu.SemaphoreType.DMA((2,)), pltpu.SemaphoreType.DMA((2,))])
    return pl.pallas_call(
        _experts_kernel,
        grid_spec=grid_spec,
        out_shape=jax.ShapeDtypeStruct(((pairs + 2 * MOE_TILE) * SLAB, LANES), F32),
        compiler_params=_params(("arbitrary",)),
        name="experts",
    )(tile_expert, n_used, row_pair, row_pair, row_pair, x3, wgu, bgu, wd, bd)


def _ple_kernel(x1_ref, y_ref, topw_ref, p_ref, nple_ref, wg_ref, wp_ref, nfin_ref, out_ref):
    tm = x1_ref.shape[0]
    topw = topw_ref[...]
    moe = []
    for s in range(SLAB):
        acc = topw[:, 0:1] * y_ref[pl.ds(s, tm, stride=TOP_K * SLAB), :]
        for k in range(1, TOP_K):
            acc = acc + topw[:, k:k + 1] * y_ref[pl.ds(k * SLAB + s, tm, stride=TOP_K * SLAB), :]
        moe.append(acc)
    x2 = x1_ref[...] + jnp.concatenate(moe, axis=1)
    gate = _sigmoid(_bdot(_rms(x2, nple_ref[...]), wg_ref[...]))
    x3 = x2 + _bdot(p_ref[...], wp_ref[...]) * gate
    out_ref[...] = _rms(x3, nfin_ref[...])


def _ple(x1, y2d, topw, p, norm_ple, wg, wp, norm_final, tm, row_offset):
    n = x1.shape[0]
    off = row_offset // tm
    row = lambda w: pl.BlockSpec((tm, w), lambda i: (i, 0))
    return pl.pallas_call(
        _ple_kernel,
        grid=(n // tm,),
        in_specs=[row(D_MODEL), pl.BlockSpec((tm * TOP_K * SLAB, LANES), lambda i: (i + off, 0)), row(N_EXPERTS),
                  row(PLE_DIM), _const_spec((1, D_MODEL)), _const_spec(wg.shape), _const_spec(wp.shape),
                  _const_spec((1, D_MODEL))],
        out_specs=row(D_MODEL),
        out_shape=jax.ShapeDtypeStruct((n, D_MODEL), F32),
        compiler_params=_params(("parallel",)),
        name="ple",
    )(x1, y2d, topw, p, norm_ple, wg, wp, norm_final)


def kernel(x_prompt, x_sample, p_prompt, p_sample, state_gdn, state_qkv_conv, state_conf_conv, norm_mix, w_in, conv_qkv_w, a_log, dt_bias, gdn_norm, w_gdn_out, conf_dw_w, conf_dw_b, conf_ln_g, conf_ln_b, w_conf_out, w_out, norm_ffn, w_router, b_router, w_gate_up, b_gate_up, w_down, b_down, norm_ple, w_ple_gate, w_ple_proj, norm_final):
    assert w_in.shape[0] == 1, "single-layer trunk"
    bsz, seq, _ = x_prompt.shape
    n_p = bsz * seq
    n_s = x_sample.shape[0] * x_sample.shape[1]
    assert x_sample.shape[1] == 1

    w_in0 = w_in[0]
    wqkv = w_in0[:, :OFF_Z].astype(BF16)
    wz = w_in0[:, OFF_Z:OFF_BETA].astype(BF16)
    wba = w_in0[:, OFF_BETA:OFF_GLU].astype(BF16)
    wglu = w_in0[:, OFF_GLU:OFF_GATE].astype(BF16)
    wgate = w_in0[:, OFF_GATE:].astype(BF16)
    zeros_h = jnp.zeros((GDN_HEADS,), F32)
    alog16 = jnp.concatenate([zeros_h, a_log[0]]).reshape(1, 2 * GDN_HEADS)
    dtb16 = jnp.concatenate([zeros_h, dt_bias[0]]).reshape(1, 2 * GDN_HEADS)
    wgo = w_gdn_out[0].astype(BF16)
    wco = w_conf_out[0].astype(BF16)
    wout = w_out[0].astype(BF16)
    wpg = w_ple_gate[0].astype(BF16)
    wpp = w_ple_proj[0].astype(BF16)
    bgu = b_gate_up[0].reshape(N_EXPERTS, 1, 2 * D_FF)
    bd = b_down[0].reshape(N_EXPERTS, 1, D_MODEL)
    row = lambda v: v.reshape(1, -1)
    inproj_w = (row(norm_mix[0]), wqkv, wz, wba, wglu, wgate, alog16, dtb16)
    wr_hi = w_router[0].astype(BF16)
    wr_lo = (w_router[0] - wr_hi.astype(F32)).astype(BF16)
    mix_w = (wgo, wco, wout, row(norm_ffn[0]), jnp.concatenate([wr_hi, wr_lo], axis=1), row(b_router[0]))
    ple_w = (row(norm_ple[0]), wpg, wpp, row(norm_final))
    conf_w = (conf_dw_w[0], row(conf_dw_b[0]), row(conf_ln_g[0]), row(conf_ln_b[0]))

    xp = x_prompt.reshape(n_p, D_MODEL)
    qkv_p, z_p, bg_p, u_p, ga_p, gb_p = _inproj(xp, *inproj_w, tm=256)
    qkv_p3 = qkv_p.reshape(bsz, seq, QKV_CH)
    u_p3 = u_p.reshape(bsz, seq, CONF_CH)
    o_p, s_p = _gdn_prompt(qkv_p3, z_p.reshape(bsz, seq, V_DIM), bg_p, conv_qkv_w[0], row(gdn_norm[0]))
    cv_p = _conf_prompt(u_p3, *conf_w)

    xs = x_sample.reshape(n_s, D_MODEL)
    qkv_s, z_s, bg_s, u_s, ga_s, gb_s = _inproj(xs, *inproj_w, tm=n_s)
    o_s, s_s = _gdn_sample(qkv_s, jnp.swapaxes(state_qkv_conv[0], 0, 1), z_s, bg_s, state_gdn[0],
                           conv_qkv_w[0], row(gdn_norm[0]))
    cv_s = _conf_sample(u_s, jnp.swapaxes(state_conf_conv[0], 0, 1), *conf_w)

    tm_p = 512
    x1_p, hm3_p, ti_p, tw_p = _mix(xp, o_p.reshape(n_p, V_DIM), cv_p.reshape(n_p, CONF_CH), ga_p, gb_p, *mix_w,
                                   tm=tm_p)
    x1_s, hm3_s, ti_s, tw_s = _mix(xs, o_s, cv_s, ga_s, gb_s, *mix_w, tm=n_s)
    routing = _route(jnp.concatenate([ti_p, ti_s], axis=0)[:, :TOP_K])
    y2d = _experts(jnp.concatenate([hm3_p, hm3_s], axis=0), *routing, w_gate_up[0], bgu, w_down[0], bd)
    y_p = _ple(x1_p, y2d, tw_p, p_prompt[0].reshape(n_p, PLE_DIM), *ple_w, tm=tm_p, row_offset=0)
    y_s = _ple(x1_s, y2d, tw_s, p_sample[0].reshape(n_s, PLE_DIM), *ple_w, tm=n_s, row_offset=n_p)

    new_qkv_s = jnp.concatenate([state_qkv_conv[0][:, 1:], qkv_s[:, None, :]], axis=1)
    new_conf_s = jnp.concatenate([state_conf_conv[0][:, 1:], u_s[:, None, :]], axis=1)
    return (y_p.reshape(bsz, seq, D_MODEL), y_s.reshape(n_s, 1, D_MODEL),
            s_p[None], qkv_p3[:, seq - (SHORT_CONV - 1):][None], u_p3[:, seq - (CONF_K - 1):][None],
            s_s[None], new_qkv_s[None], new_conf_s[None])
```

```python
import functools

import jax
import jax.numpy as jnp
from jax import lax
from jax.experimental import pallas as pl
from jax.experimental.pallas import tpu as pltpu

F32 = jnp.float32
BF16 = jnp.bfloat16

D_MODEL = 1024
GDN_HEADS = 8
GDN_DK = 128
GDN_DV = 128
QK_DIM = GDN_HEADS * GDN_DK
V_DIM = GDN_HEADS * GDN_DV
QKV_CH = 2 * QK_DIM + V_DIM
SHORT_CONV = 4
CHUNK = 64
CONF_CH = D_MODEL // 2
CONF_K = 31
N_EXPERTS = 32
TOP_K = 4
D_FF = D_MODEL
SWIGLU_LIMIT = 7.0
SWIGLU_ALPHA = 1.702
PLE_DIM = 256
NORM_EPS = 1e-6
LN_EPS = 1e-5

OFF_Z = QKV_CH
OFF_BETA = OFF_Z + V_DIM
OFF_GLU = OFF_BETA + 2 * GDN_HEADS
OFF_GATE = OFF_GLU + 2 * CONF_CH

VMEM_LIMIT_BYTES = 52 * 1024 * 1024
SUBLANES = 8
LANES = 128
CONF_HALO = 32


def _dot(a, b):
    return jnp.dot(a, b, preferred_element_type=F32)


def _bdot(a, b):
    return jnp.dot(a.astype(BF16), b.astype(BF16), preferred_element_type=F32)


def _split3(x):
    hi = x.astype(BF16)
    r1 = x - hi.astype(F32)
    mid = r1.astype(BF16)
    lo = (r1 - mid.astype(F32)).astype(BF16)
    return hi, mid, lo


def _mask_dot(mask, x, mask_first):
    m = jnp.where(mask, 1.0, 0.0).astype(BF16)
    parts = [_dot(m, p) if mask_first else _dot(p, m) for p in _split3(x)]
    return parts[0] + (parts[1] + parts[2])


def _sigmoid(x):
    return 1.0 / (1.0 + jnp.exp(-x))


def _silu(x):
    return x * _sigmoid(x)


def _rms(x, g):
    return x * lax.rsqrt(jnp.mean(x * x, axis=-1, keepdims=True) + NORM_EPS) * g


def _const_spec(shape):
    nd = len(shape)
    return pl.BlockSpec(shape, lambda *_: (0,) * nd, pipeline_mode=pl.Buffered(1))


def _params(sem):
    return pltpu.CompilerParams(dimension_semantics=sem, vmem_limit_bytes=VMEM_LIMIT_BYTES)


def _inproj_kernel(x_ref, nw_ref, wqkv_ref, wz_ref, wba_ref, wglu_ref, wgate_ref, alog_ref, dtb_ref,
                   qkv_ref, z_ref, bg_ref, u_ref, ga_ref, gb_ref):
    a = _rms(x_ref[...], nw_ref[...]).astype(BF16)
    for c in range(QKV_CH // D_MODEL):
        cols = slice(c * D_MODEL, (c + 1) * D_MODEL)
        qkv_ref[:, cols] = _dot(a, wqkv_ref[:, cols])
    z_ref[...] = _dot(a, wz_ref[...])
    ba = _dot(a, wba_ref[...])
    lane = lax.broadcasted_iota(jnp.int32, ba.shape, 1)
    t = ba + dtb_ref[...]
    softplus = jnp.maximum(t, 0.0) + jnp.log(1.0 + jnp.exp(-jnp.abs(t)))
    bg_ref[...] = jnp.where(lane < GDN_HEADS, _sigmoid(ba), -jnp.exp(alog_ref[...]) * softplus)
    glu = _dot(a, wglu_ref[...])
    u_ref[...] = glu[:, :CONF_CH] * _sigmoid(glu[:, CONF_CH:])
    ga_ref[...] = _sigmoid(_dot(a, wgate_ref[:, :D_MODEL]))
    gb_ref[...] = _sigmoid(_dot(a, wgate_ref[:, D_MODEL:]))


def _inproj(x, norm_w, wqkv, wz, wba, wglu, wgate, alog16, dtb16, tm):
    n = x.shape[0]
    row = lambda w: pl.BlockSpec((tm, w), lambda i: (i, 0))
    return pl.pallas_call(
        _inproj_kernel,
        grid=(n // tm,),
        in_specs=[row(D_MODEL), _const_spec((1, D_MODEL)), _const_spec(wqkv.shape), _const_spec(wz.shape),
                  _const_spec(wba.shape), _const_spec(wglu.shape), _const_spec(wgate.shape),
                  _const_spec((1, 2 * GDN_HEADS)), _const_spec((1, 2 * GDN_HEADS))],
        out_specs=[row(QKV_CH), row(V_DIM), row(2 * GDN_HEADS), row(CONF_CH), row(D_MODEL), row(D_MODEL)],
        out_shape=[jax.ShapeDtypeStruct((n, w), F32)
                   for w in (QKV_CH, V_DIM, 2 * GDN_HEADS, CONF_CH, D_MODEL, D_MODEL)],
        compiler_params=_params(("parallel",)),
        name="inproj",
    )(x, norm_w, wqkv, wz, wba, wglu, wgate, alog16, dtb16)


GDN_GROUP = 4
PREP_CHUNKS = 4
GROUP_ROWS = GDN_GROUP * CHUNK
N_GROUPS = GDN_HEADS // GDN_GROUP


def _block_unit_lower_inverse(a, eye):
    b = -a
    t = eye + b
    p = _bdot(b, b)
    covered = 2
    while covered < CHUNK:
        covered *= 2
        if covered < CHUNK:
            r = _bdot(jnp.concatenate([t, p], axis=0), p)
            t = t + r[:GROUP_ROWS]
            p = r[GROUP_ROWS:]
        else:
            t = t + _bdot(t, p)
    return t


def _gdn_prep_kernel(qkv_ref, halo_ref, bg_ref, bgt_ref, cw_ref, wqg_ref, kdt_ref, qk_ref, u_ref, gl_ref, ext_ref):
    c = pl.program_id(1)
    ext_ref[0:SUBLANES, :] = jnp.where(c > 0, halo_ref[...], 0.0)
    ext_ref[SUBLANES:SUBLANES + PREP_CHUNKS * CHUNK, :] = qkv_ref[...]
    for n in range(PREP_CHUNKS):
        _gdn_prep_chunk(n, bg_ref, bgt_ref, cw_ref, wqg_ref, kdt_ref, qk_ref, u_ref, gl_ref, ext_ref)


def _gdn_prep_chunk(n, bg_ref, bgt_ref, cw_ref, wqg_ref, kdt_ref, qk_ref, u_ref, gl_ref, ext_ref):
    base = SUBLANES + n * CHUNK

    def convsilu(c0):
        cols = slice(c0, c0 + LANES)
        acc = cw_ref[SHORT_CONV - 1:SHORT_CONV, cols] * ext_ref[base:base + CHUNK, cols]
        for j in range(SHORT_CONV - 1):
            r0 = base - (SHORT_CONV - 1) + j
            acc = acc + cw_ref[j:j + 1, cols] * ext_ref[r0:r0 + CHUNK, cols]
        return _silu(acc)

    r64 = lax.broadcasted_iota(jnp.int32, (CHUNK, CHUNK), 0)
    c64 = lax.broadcasted_iota(jnp.int32, (CHUNK, CHUNK), 1)
    bg = bg_ref[n]
    gc_all = _mask_dot(r64 >= c64, bg, mask_first=True)
    gct_all = _mask_dot(r64 <= c64, bgt_ref[n], mask_first=False)
    gl_ref[n] = gc_all[CHUNK - 1:CHUNK, :]

    row = lax.broadcasted_iota(jnp.int32, (GROUP_ROWS, GROUP_ROWS), 0)
    col = lax.broadcasted_iota(jnp.int32, (GROUP_ROWS, GROUP_ROWS), 1)
    blk = row - col + (col & (CHUNK - 1))
    same = (blk >= 0) & (blk < CHUNK)
    tril = same & (row >= col)
    strict = same & (row > col)
    eye = (row == col).astype(F32)
    nt = (((1,), (1,)), ((), ()))

    for g in range(N_GROUPS):
        heads = range(g * GDN_GROUP, (g + 1) * GDN_GROUP)
        stack = lambda f: jnp.concatenate([f(h) for h in heads], axis=0)
        q = stack(lambda h: convsilu(h * GDN_DK))
        k = stack(lambda h: convsilu(QK_DIM + h * GDN_DK))
        v = stack(lambda h: convsilu(2 * QK_DIM + h * GDN_DV))
        q = q * lax.rsqrt(jnp.sum(q * q, axis=-1, keepdims=True) + NORM_EPS) * (GDN_DK ** -0.5)
        k = k * lax.rsqrt(jnp.sum(k * k, axis=-1, keepdims=True) + NORM_EPS)
        beta = stack(lambda h: bg[:, h:h + 1])
        gc = stack(lambda h: gc_all[:, GDN_HEADS + h:GDN_HEADS + h + 1])
        g_last = stack(lambda h: jnp.broadcast_to(
            gc_all[CHUNK - 1:CHUNK, GDN_HEADS + h:GDN_HEADS + h + 1], (CHUNK, 1)))
        gct = jnp.concatenate([gct_all[GDN_HEADS + h:GDN_HEADS + h + 1, :] for h in heads], axis=1)
        decay = jnp.exp(jnp.where(tril, gc - gct, -jnp.inf))
        kb = k * beta
        kq_k = lax.dot_general(jnp.concatenate([kb.astype(BF16), q.astype(BF16)], axis=0), k.astype(BF16), nt,
                               preferred_element_type=F32)
        kk = kq_k[:GROUP_ROWS]
        t_inv = _block_unit_lower_inverse(jnp.where(strict, kk * decay, 0.0), eye)
        eg = jnp.exp(gc)
        uw = _bdot(t_inv, jnp.concatenate([v * beta, kb * eg], axis=1))
        u_ref[n, g] = uw[:, :GDN_DV]
        w = uw[:, GDN_DV:].astype(BF16)
        qg = (q * eg).astype(BF16)
        for i, h in enumerate(heads):
            rows = slice(i * CHUNK, (i + 1) * CHUNK)
            wqg_ref[n, h, 0:CHUNK, :] = w[rows]
            wqg_ref[n, h, CHUNK:2 * CHUNK, :] = qg[rows]
        qk_ref[n, g] = (kq_k[GROUP_ROWS:] * decay).astype(BF16)
        kdt_ref[n, g] = (k * jnp.exp(g_last - gc)).T.astype(BF16)


def _gdn_scan_kernel(wqg_ref, kdt_ref, qk_ref, u_ref, gl_ref, z_ref, gn_ref, o_ref, s_out_ref, s_ref):
    c = pl.program_id(1)

    @pl.when(c == 0)
    def _():
        s_ref[...] = jnp.zeros(s_ref.shape, F32)

    zeros = jnp.zeros((CHUNK, GDN_DV), BF16)
    for n in range(PREP_CHUNKS):
        dl_all = jnp.exp(gl_ref[n])
        trows = slice(n * CHUNK, (n + 1) * CHUNK)
        for g in range(N_GROUPS):
            heads = list(range(g * GDN_GROUP, (g + 1) * GDN_GROUP))
            v_new, q_s = [], []
            for i, h in enumerate(heads):
                r = _dot(wqg_ref[n, h], s_ref[h].astype(BF16))
                v_new.append((u_ref[n, g, i * CHUNK:(i + 1) * CHUNK, :] - r[:CHUNK]).astype(BF16))
                q_s.append(r[CHUNK:])
            o_all = jnp.concatenate(q_s, axis=0) + _dot(qk_ref[n, g], jnp.concatenate(v_new, axis=0))
            for i, h in enumerate(heads):
                vm = jnp.concatenate([v_new[j] if j == i else zeros for j in range(GDN_GROUP)], axis=0)
                s_ref[h] = s_ref[h] * dl_all[:, GDN_HEADS + h:GDN_HEADS + h + 1] + _dot(kdt_ref[n, g], vm)
                o = _rms(o_all[i * CHUNK:(i + 1) * CHUNK], gn_ref[...])
                cols = slice(h * GDN_DV, (h + 1) * GDN_DV)
                o_ref[trows, cols] = (o * _silu(z_ref[trows, cols])).astype(o_ref.dtype)

    @pl.when(c == pl.num_programs(1) - 1)
    def _():
        s_out_ref[...] = s_ref[...]


def _gdn_prompt(qkv, z, bg, conv_w, gdn_norm):
    b, l, _ = qkv.shape
    nc = l // CHUNK
    bg4 = bg.reshape(b, nc, CHUNK, 2 * GDN_HEADS)
    bgt4 = jnp.swapaxes(bg4, 2, 3)
    halo_blocks = CHUNK // SUBLANES
    grid = (b, nc // PREP_CHUNKS)
    seq = lambda w: pl.BlockSpec((None, PREP_CHUNKS * CHUNK, w), lambda i, c: (i, c, 0))
    chunks = lambda *s: pl.BlockSpec((None, PREP_CHUNKS) + s, lambda i, c: (i, c) + (0,) * len(s))
    inter_shapes = [((GDN_HEADS, 2 * CHUNK, GDN_DK), BF16), ((N_GROUPS, GDN_DK, GROUP_ROWS), BF16),
                    ((N_GROUPS, GROUP_ROWS, GROUP_ROWS), BF16), ((N_GROUPS, GROUP_ROWS, GDN_DV), F32),
                    ((1, 2 * GDN_HEADS), F32)]
    inter_specs = [chunks(*s) for s, _ in inter_shapes]
    wqg, kdt, qk, u, gl = pl.pallas_call(
        _gdn_prep_kernel,
        grid=grid,
        in_specs=[seq(QKV_CH),
                  pl.BlockSpec((None, SUBLANES, QKV_CH),
                               lambda i, c: (i, jnp.maximum(c * PREP_CHUNKS * halo_blocks - 1, 0), 0)),
                  chunks(CHUNK, 2 * GDN_HEADS), chunks(2 * GDN_HEADS, CHUNK), _const_spec(conv_w.shape)],
        out_specs=inter_specs,
        out_shape=[jax.ShapeDtypeStruct((b, nc) + s, d) for s, d in inter_shapes],
        scratch_shapes=[pltpu.VMEM((SUBLANES + PREP_CHUNKS * CHUNK, QKV_CH), F32)],
        compiler_params=_params(("parallel", "parallel")),
        name="gdn_prep",
    )(qkv, qkv, bg4, bgt4, conv_w)
    return pl.pallas_call(
        _gdn_scan_kernel,
        grid=grid,
        in_specs=inter_specs + [seq(V_DIM), _const_spec((1, GDN_DV))],
        out_specs=[seq(V_DIM),
                   pl.BlockSpec((None, GDN_HEADS, GDN_DK, GDN_DV), lambda i, c: (i, 0, 0, 0))],
        out_shape=[jax.ShapeDtypeStruct((b, l, V_DIM), BF16),
                   jax.ShapeDtypeStruct((b, GDN_HEADS, GDN_DK, GDN_DV), F32)],
        scratch_shapes=[pltpu.VMEM((GDN_HEADS, GDN_DK, GDN_DV), F32)],
        compiler_params=_params(("parallel", "arbitrary")),
        name="gdn_scan",
    )(wqg, kdt, qk, u, gl, z, gdn_norm)


GDN_SAMPLE_ROWS = 8


def _gdn_sample_kernel(raw_ref, cs_ref, z_ref, bg_ref, s_in_ref, cw_ref, gn_ref, o_ref, s_out_ref):
    conv = cw_ref[SHORT_CONV - 1:SHORT_CONV, :] * raw_ref[...]
    for j in range(SHORT_CONV - 1):
        conv = conv + cw_ref[j:j + 1, :] * cs_ref[j]
    qkv = _silu(conv)
    bg = bg_ref[...]
    for h in range(GDN_HEADS):
        q = qkv[:, h * GDN_DK:(h + 1) * GDN_DK]
        k = qkv[:, QK_DIM + h * GDN_DK:QK_DIM + (h + 1) * GDN_DK]
        v = qkv[:, 2 * QK_DIM + h * GDN_DV:2 * QK_DIM + (h + 1) * GDN_DV]
        q = q * lax.rsqrt(jnp.sum(q * q, axis=-1, keepdims=True) + NORM_EPS) * (GDN_DK ** -0.5)
        k = k * lax.rsqrt(jnp.sum(k * k, axis=-1, keepdims=True) + NORM_EPS)
        beta = bg[:, h:h + 1]
        eg = jnp.exp(bg[:, GDN_HEADS + h:GDN_HEADS + h + 1])
        u = v * beta
        qk = jnp.sum(q * k, axis=-1, keepdims=True)
        w_t = (k * (beta * eg)).T
        qg_t = (q * eg).T
        k_t = k.T
        o_rows = []
        for r in range(GDN_SAMPLE_ROWS):
            s = s_in_ref[r, h]
            w_s = jnp.sum(w_t[:, r:r + 1] * s, axis=0, keepdims=True)
            q_s = jnp.sum(qg_t[:, r:r + 1] * s, axis=0, keepdims=True)
            v_new = u[r:r + 1] - w_s
            o_rows.append(q_s + qk[r:r + 1] * v_new)
            s_out_ref[r, h] = s * eg[r:r + 1] + k_t[:, r:r + 1] * v_new
        o = _rms(jnp.concatenate(o_rows, axis=0), gn_ref[...])
        cols = slice(h * GDN_DV, (h + 1) * GDN_DV)
        o_ref[:, cols] = (o * _silu(z_ref[:, cols])).astype(o_ref.dtype)


def _gdn_sample(raw, conv_state_t, z, bg, state, conv_w, gdn_norm):
    n = raw.shape[0]
    rb = GDN_SAMPLE_ROWS
    row = lambda w: pl.BlockSpec((rb, w), lambda i: (i, 0))
    st = pl.BlockSpec((rb, GDN_HEADS, GDN_DK, GDN_DV), lambda i: (i, 0, 0, 0))
    return pl.pallas_call(
        _gdn_sample_kernel,
        grid=(n // rb,),
        in_specs=[row(QKV_CH), pl.BlockSpec((SHORT_CONV - 1, rb, QKV_CH), lambda i: (0, i, 0)),
                  row(V_DIM), row(2 * GDN_HEADS), st, _const_spec(conv_w.shape), _const_spec((1, GDN_DV))],
        out_specs=[row(V_DIM), st],
        out_shape=[jax.ShapeDtypeStruct((n, V_DIM), BF16), jax.ShapeDtypeStruct(state.shape, F32)],
        compiler_params=_params(("parallel",)),
        name="gdn_sample",
    )(raw, conv_state_t, z, bg, state, conv_w, gdn_norm)


CONF_TILE = 256
CONF_ROWS = 64


def _ln_silu(x, g, b):
    mu = jnp.mean(x, axis=-1, keepdims=True)
    xc = x - mu
    var = jnp.mean(xc * xc, axis=-1, keepdims=True)
    return _silu(xc * lax.rsqrt(var + LN_EPS) * g + b)


def _conf_prompt_kernel(u_ref, w_ref, b_ref, lng_ref, lnb_ref, cv_ref, ubuf_ref, sh_ref, acc_ref):
    @pl.when(pl.program_id(1) == 0)
    def _():
        ubuf_ref[0:CONF_HALO, :] = jnp.zeros((CONF_HALO, CONF_CH), F32)

    ubuf_ref[CONF_HALO:CONF_HALO + CONF_TILE, :] = u_ref[...]
    span = CONF_HALO + CONF_TILE - SUBLANES
    for k in range(1, SUBLANES):
        sh_ref[k - 1, 0:span, :] = ubuf_ref[k:k + span, :]
    base = CONF_HALO - (CONF_K - 1)
    for rb in range(CONF_TILE // CONF_ROWS):
        for cb in range(CONF_CH // LANES):
            cols = slice(cb * LANES, (cb + 1) * LANES)
            acc = b_ref[:, cols]
            for j in range(CONF_K):
                k = (base + j) % SUBLANES
                r0 = base + j - k + rb * CONF_ROWS
                rows = slice(r0, r0 + CONF_ROWS)
                src = ubuf_ref[rows, cols] if k == 0 else sh_ref[k - 1, rows, cols]
                acc = acc + w_ref[j:j + 1, cols] * src
            acc_ref[rb * CONF_ROWS:(rb + 1) * CONF_ROWS, cols] = acc
    ubuf_ref[0:CONF_HALO, :] = ubuf_ref[CONF_TILE:CONF_TILE + CONF_HALO, :]
    cv_ref[...] = _ln_silu(acc_ref[...], lng_ref[...], lnb_ref[...]).astype(cv_ref.dtype)


def _conf_prompt(u, w, b, ln_g, ln_b):
    bsz, l, _ = u.shape
    seq = pl.BlockSpec((None, CONF_TILE, CONF_CH), lambda i, t: (i, t, 0))
    return pl.pallas_call(
        _conf_prompt_kernel,
        grid=(bsz, l // CONF_TILE),
        in_specs=[seq, _const_spec(w.shape), _const_spec((1, CONF_CH)), _const_spec((1, CONF_CH)),
                  _const_spec((1, CONF_CH))],
        out_specs=seq,
        out_shape=jax.ShapeDtypeStruct((bsz, l, CONF_CH), BF16),
        scratch_shapes=[pltpu.VMEM((CONF_HALO + CONF_TILE, CONF_CH), F32),
                        pltpu.VMEM((SUBLANES - 1, CONF_HALO + CONF_TILE - SUBLANES, CONF_CH), F32),
                        pltpu.VMEM((CONF_TILE, CONF_CH), F32)],
        compiler_params=_params(("parallel", "arbitrary")),
        name="conf_prompt",
    )(u, w, b, ln_g, ln_b)


def _conf_sample_kernel(u_ref, cs_ref, w_ref, b_ref, lng_ref, lnb_ref, cv_ref):
    acc = w_ref[CONF_K - 1:CONF_K, :] * u_ref[...] + b_ref[...]
    for j in range(CONF_K - 1):
        acc = acc + w_ref[j:j + 1, :] * cs_ref[j]
    cv_ref[...] = _ln_silu(acc, lng_ref[...], lnb_ref[...]).astype(cv_ref.dtype)


def _conf_sample(u, conf_state_t, w, b, ln_g, ln_b):
    n = u.shape[0]
    return pl.pallas_call(
        _conf_sample_kernel,
        out_shape=jax.ShapeDtypeStruct((n, CONF_CH), BF16),
        compiler_params=pltpu.CompilerParams(vmem_limit_bytes=VMEM_LIMIT_BYTES),
        name="conf_sample",
    )(u, conf_state_t, w, b, ln_g, ln_b)


MOE_TILE = 256
SLAB = D_MODEL // LANES


def _mix_kernel(x_ref, o_ref, cv_ref, ga_ref, gb_ref, wgo_ref, wco_ref, wout_ref, nffn_ref, wr_ref, br_ref,
                x1_ref, hm3_ref, topi_ref, topw_ref):
    tm = x_ref.shape[0]
    branch_a = _dot(o_ref[...], wgo_ref[...])
    branch_b = _dot(cv_ref[...], wco_ref[...])
    merged = ga_ref[...] * branch_a + gb_ref[...] * branch_b
    x1 = x_ref[...] + _bdot(merged, wout_ref[...])
    x1_ref[...] = x1
    hm = _rms(x1, nffn_ref[...])
    for s in range(SLAB):
        hm3_ref[pl.ds(s, tm, stride=SLAB), :] = hm[:, s * LANES:(s + 1) * LANES]
    hm_hi = hm.astype(BF16)
    hm_lo = (hm - hm_hi.astype(F32)).astype(BF16)
    by_hi = _dot(hm_hi, wr_ref[...])
    logits = (by_hi[:, :N_EXPERTS] + (by_hi[:, N_EXPERTS:] + _dot(hm_lo, wr_ref[:, :N_EXPERTS]))) + br_ref[...]
    lane = lax.broadcasted_iota(jnp.int32, logits.shape, 1)
    work = logits
    top_v = []
    top_i = jnp.zeros(logits.shape, jnp.int32)
    for k in range(TOP_K):
        m = jnp.max(work, axis=-1, keepdims=True)
        first = jnp.min(jnp.where(work == m, lane, N_EXPERTS), axis=-1, keepdims=True)
        top_v.append(m)
        top_i = jnp.where(lane == k, first, top_i)
        work = jnp.where(lane == first, -jnp.inf, work)
    ex = [jnp.exp(v - top_v[0]) for v in top_v]
    den = ex[0]
    for e in ex[1:]:
        den = den + e
    top_w = jnp.zeros(logits.shape, F32)
    for k, e in enumerate(ex):
        top_w = jnp.where(lane == k, e / den, top_w)
    topi_ref[...] = top_i
    topw_ref[...] = top_w


def _mix(x, o, cv, ga, gb, wgo, wco, wout, norm_ffn, w_router, b_router, tm):
    n = x.shape[0]
    row = lambda w: pl.BlockSpec((tm, w), lambda i: (i, 0))
    return pl.pallas_call(
        _mix_kernel,
        grid=(n // tm,),
        in_specs=[row(D_MODEL), row(V_DIM), row(CONF_CH), row(D_MODEL), row(D_MODEL),
                  _const_spec(wgo.shape), _const_spec(wco.shape), _const_spec(wout.shape),
                  _const_spec((1, D_MODEL)), _const_spec(w_router.shape), _const_spec((1, N_EXPERTS))],
        out_specs=[row(D_MODEL), pl.BlockSpec((tm * SLAB, LANES), lambda i: (i, 0)), row(N_EXPERTS), row(N_EXPERTS)],
        out_shape=[jax.ShapeDtypeStruct((n, D_MODEL), F32), jax.ShapeDtypeStruct((n * SLAB, LANES), F32),
                   jax.ShapeDtypeStruct((n, N_EXPERTS), jnp.int32), jax.ShapeDtypeStruct((n, N_EXPERTS), F32)],
        compiler_params=_params(("parallel",)),
        name="mix",
    )(x, o, cv, ga, gb, wgo, wco, wout, norm_ffn, w_router, b_router)


ROUTE_BLOCK_MAX = 16384


def _equal_blocks(n):
    for d in range(1, n // LANES + 1):
        if n % d == 0 and n // d <= ROUTE_BLOCK_MAX and (n // d) % LANES == 0:
            return d
    raise ValueError(f"cannot split {n} elements into lane-aligned blocks")


def _route_rows_kernel(lo_ref, hi_ref, pos_ref, rp_ref):
    s = pl.program_id(0)
    pairs_per = pos_ref.shape[-1]

    @pl.when(s == 0)
    def _():
        def fill(r, carry):
            rp_ref[r] = -1
            return carry
        for i in range(lo_ref.shape[0]):
            lax.fori_loop(lo_ref[i], hi_ref[i], fill, 0)

    @pl.when(s > 0)
    def _():
        base = (s - 1) * pairs_per

        def place(i, carry):
            for j in range(SUBLANES):
                p = i * SUBLANES + j
                rp_ref[pos_ref[0, p]] = base + p
            return carry
        lax.fori_loop(0, pairs_per // SUBLANES, place, 0)


def _route(top_i):
    n = top_i.shape[0]
    pairs = n * TOP_K
    n_tiles = pairs // MOE_TILE + N_EXPERTS
    e_flat = top_i.reshape(pairs)
    onehot = (e_flat[:, None] == jnp.arange(N_EXPERTS, dtype=jnp.int32)[None, :]).astype(jnp.int32)
    csum = jnp.cumsum(onehot, axis=0)
    rank = jnp.take_along_axis(csum, e_flat[:, None], axis=1)[:, 0] - 1
    tiles_per = (csum[-1] + MOE_TILE - 1) // MOE_TILE
    tile_end = jnp.cumsum(tiles_per)
    n_used = tile_end[-1]
    pos = (tile_end - tiles_per)[e_flat] * MOE_TILE + rank
    n_rows = n_tiles * MOE_TILE
    n_place = _equal_blocks(pairs)
    pad_lo = jnp.concatenate([(tile_end - tiles_per) * MOE_TILE + csum[-1], n_used.reshape(1) * MOE_TILE])
    pad_hi = jnp.concatenate([tile_end * MOE_TILE, jnp.full((1,), n_rows, jnp.int32)])
    row_pair = pl.pallas_call(
        _route_rows_kernel,
        grid_spec=pltpu.PrefetchScalarGridSpec(
            num_scalar_prefetch=2,
            grid=(1 + n_place,),
            in_specs=[pl.BlockSpec((None, 1, pairs // n_place), lambda s, lo, hi: (jnp.maximum(s - 1, 0), 0, 0),
                                   memory_space=pltpu.SMEM)],
            out_specs=pl.BlockSpec(memory_space=pltpu.SMEM)),
        out_shape=jax.ShapeDtypeStruct((n_rows,), jnp.int32),
        compiler_params=_params(("arbitrary",)),
        name="route_rows",
    )(pad_lo, pad_hi, pos.reshape(n_place, 1, pairs // n_place))
    row_pair = jnp.concatenate([jnp.full((MOE_TILE,), -1, jnp.int32), row_pair]).reshape(n_tiles + 1, 1, MOE_TILE)
    t = jnp.minimum(jnp.arange(n_tiles + 1, dtype=jnp.int32), n_used - 1)
    tile_expert = jnp.sum((tile_end[None, :] <= t[:, None]).astype(jnp.int32), axis=1)
    return tile_expert, n_used.reshape(1), row_pair


def _experts_kernel(te_ref, nu_ref, idx0_ref, idxn_ref, idxp_ref, x3_hbm, wgu_ref, bgu_ref, wd_ref, bd_ref,
                    y_hbm, x0, x1, y0, y1, wgu16, wd16, sem_in, sem_out):
    t = pl.program_id(0)
    n_used = nu_ref[0]
    xs, ys = (x0, x1), (y0, y1)
    tile_rows = MOE_TILE * SLAB
    pair_rows = y_hbm.shape[0] - 2 * tile_rows

    def start_gather(idx_ref, c):
        for i in range(MOE_TILE):
            tok = jnp.maximum(idx_ref[0, i], 0) // TOP_K
            pltpu.make_async_copy(x3_hbm.at[tok], xs[c].at[pl.ds(i * SLAB, SLAB), :],
                                  sem_in.at[c]).start(priority=i % 2)

    def start_scatter(idx_ref, c):
        for i in range(MOE_TILE):
            pair = idx_ref[0, i]
            row = jnp.where(pair >= 0, pair * SLAB, pair_rows + c * tile_rows + i * SLAB)
            pltpu.make_async_copy(ys[c].at[pl.ds(i * SLAB, SLAB), :],
                                  y_hbm.at[pl.ds(pl.multiple_of(row, SLAB), SLAB), :],
                                  sem_out.at[c]).start(priority=i % 2)

    def wait_tile(buf, sem):
        pltpu.make_async_copy(buf, buf, sem).wait()

    @pl.when(t == 0)
    def _():
        y0[...] = jnp.zeros(y0.shape, F32)
        y1[...] = jnp.zeros(y1.shape, F32)
        pltpu.make_async_copy(y0, y_hbm.at[pl.ds(pair_rows, tile_rows), :], sem_out.at[0]).start()
        start_gather(idx0_ref, 0)

    @pl.when((t < n_used) & ((t == 0) | (te_ref[t] != te_ref[jnp.maximum(t - 1, 0)])))
    def _():
        wgu16[...] = wgu_ref[...].astype(BF16)
        wd16[...] = wd_ref[...].astype(BF16)

    def step(c):
        o = 1 - c
        wait_tile(xs[c], sem_in.at[c])
        start_gather(idxn_ref, o)
        start_scatter(idxp_ref, o)
        x = jnp.concatenate([xs[c][pl.ds(s, MOE_TILE, stride=SLAB), :] for s in range(SLAB)], axis=1)
        gu = _dot(x.astype(BF16), wgu16[...]) + bgu_ref[...]
        gate = jnp.minimum(gu[:, :D_FF], SWIGLU_LIMIT)
        up = jnp.clip(gu[:, D_FF:], -SWIGLU_LIMIT, SWIGLU_LIMIT)
        hid = (up + 1.0) * (gate * _sigmoid(SWIGLU_ALPHA * gate))
        ye = _dot(hid.astype(BF16), wd16[...]) + bd_ref[...]
        wait_tile(ys[c], sem_out.at[c])
        for s in range(SLAB):
            ys[c][pl.ds(s, MOE_TILE, stride=SLAB), :] = ye[:, s * LANES:(s + 1) * LANES]

    def drain(c):
        o = 1 - c
        start_scatter(idxp_ref, o)
        wait_tile(ys[o], sem_out.at[o])
        wait_tile(ys[c], sem_out.at[c])
        wait_tile(xs[c], sem_in.at[c])

    for c in range(2):
        pl.when((t < n_used) & (t % 2 == c))(functools.partial(step, c))
        pl.when((t == n_used) & (t % 2 == c))(functools.partial(drain, c))


def _experts(hm3, tile_expert, n_used, row_pair, wgu, bgu, wd, bd):
    n_steps = row_pair.shape[0]
    n_tiles = n_steps - 1
    pairs = hm3.shape[0] // SLAB * TOP_K
    x3 = hm3.reshape(-1, SLAB, LANES)
    idx = lambda f: pl.BlockSpec((None, 1, MOE_TILE), f, memory_space=pltpu.SMEM)
    tile_buf = pltpu.VMEM((MOE_TILE * SLAB, LANES), F32)
    grid_spec = pltpu.PrefetchScalarGridSpec(
        num_scalar_prefetch=2,
        grid=(n_steps,),
        in_specs=[idx(lambda t, te, nu: (1, 0, 0)),
                  idx(lambda t, te, nu: (jnp.minimum(t + 2, n_tiles), 0, 0)),
                  idx(lambda t, te, nu: (t, 0, 0)),
                  pl.BlockSpec(memory_space=pl.ANY),
                  pl.BlockSpec((None, D_MODEL, 2 * D_FF), lambda t, te, nu: (te[t], 0, 0)),
                  pl.BlockSpec((None, 1, 2 * D_FF), lambda t, te, nu: (te[t], 0, 0)),
                  pl.BlockSpec((None, D_FF, D_MODEL), lambda t, te, nu: (te[t], 0, 0)),
                  pl.BlockSpec((None, 1, D_MODEL), lambda t, te, nu: (te[t], 0, 0))],
        out_specs=pl.BlockSpec(memory_space=pl.ANY),
        scratch_shapes=[tile_buf, tile_buf, tile_buf, tile_buf,
                        pltpu.VMEM((D_MODEL, 2 * D_FF), BF16), pltpu.VMEM((D_FF, D_MODEL), BF16),
                        pltpu.SemaphoreType.DMA((2,)), pltpu.SemaphoreType.DMA((2,))])
    return pl.pallas_call(
        _experts_kernel,
        grid_spec=grid_spec,
        out_shape=jax.ShapeDtypeStruct(((pairs + 2 * MOE_TILE) * SLAB, LANES), F32),
        compiler_params=_params(("arbitrary",)),
        name="experts",
    )(tile_expert, n_used, row_pair, row_pair, row_pair, x3, wgu, bgu, wd, bd)


def _ple_kernel(x1_ref, y_ref, topw_ref, p_ref, nple_ref, wg_ref, wp_ref, nfin_ref, out_ref):
    tm = x1_ref.shape[0]
    topw = topw_ref[...]
    moe = []
    for s in range(SLAB):
        acc = topw[:, 0:1] * y_ref[pl.ds(s, tm, stride=TOP_K * SLAB), :]
        for k in range(1, TOP_K):
            acc = acc + topw[:, k:k + 1] * y_ref[pl.ds(k * SLAB + s, tm, stride=TOP_K * SLAB), :]
        moe.append(acc)
    x2 = x1_ref[...] + jnp.concatenate(moe, axis=1)
    gate = _sigmoid(_bdot(_rms(x2, nple_ref[...]), wg_ref[...]))
    x3 = x2 + _bdot(p_ref[...], wp_ref[...]) * gate
    out_ref[...] = _rms(x3, nfin_ref[...])


def _ple(x1, y2d, topw, p, norm_ple, wg, wp, norm_final, tm, row_offset):
    n = x1.shape[0]
    off = row_offset // tm
    row = lambda w: pl.BlockSpec((tm, w), lambda i: (i, 0))
    return pl.pallas_call(
        _ple_kernel,
        grid=(n // tm,),
        in_specs=[row(D_MODEL), pl.BlockSpec((tm * TOP_K * SLAB, LANES), lambda i: (i + off, 0)), row(N_EXPERTS),
                  row(PLE_DIM), _const_spec((1, D_MODEL)), _const_spec(wg.shape), _const_spec(wp.shape),
                  _const_spec((1, D_MODEL))],
        out_specs=row(D_MODEL),
        out_shape=jax.ShapeDtypeStruct((n, D_MODEL), F32),
        compiler_params=_params(("parallel",)),
        name="ple",
    )(x1, y2d, topw, p, norm_ple, wg, wp, norm_final)


def kernel(x_prompt, x_sample, p_prompt, p_sample, state_gdn, state_qkv_conv, state_conf_conv, norm_mix, w_in, conv_qkv_w, a_log, dt_bias, gdn_norm, w_gdn_out, conf_dw_w, conf_dw_b, conf_ln_g, conf_ln_b, w_conf_out, w_out, norm_ffn, w_router, b_router, w_gate_up, b_gate_up, w_down, b_down, norm_ple, w_ple_gate, w_ple_proj, norm_final):
    assert w_in.shape[0] == 1, "single-layer trunk"
    bsz, seq, _ = x_prompt.shape
    n_p = bsz * seq
    n_s = x_sample.shape[0] * x_sample.shape[1]
    assert x_sample.shape[1] == 1

    w_in0 = w_in[0]
    wqkv = w_in0[:, :OFF_Z].astype(BF16)
    wz = w_in0[:, OFF_Z:OFF_BETA].astype(BF16)
    wba = w_in0[:, OFF_BETA:OFF_GLU].astype(BF16)
    wglu = w_in0[:, OFF_GLU:OFF_GATE].astype(BF16)
    wgate = w_in0[:, OFF_GATE:].astype(BF16)
    zeros_h = jnp.zeros((GDN_HEADS,), F32)
    alog16 = jnp.concatenate([zeros_h, a_log[0]]).reshape(1, 2 * GDN_HEADS)
    dtb16 = jnp.concatenate([zeros_h, dt_bias[0]]).reshape(1, 2 * GDN_HEADS)
    wgo = w_gdn_out[0].astype(BF16)
    wco = w_conf_out[0].astype(BF16)
    wout = w_out[0].astype(BF16)
    wpg = w_ple_gate[0].astype(BF16)
    wpp = w_ple_proj[0].astype(BF16)
    bgu = b_gate_up[0].reshape(N_EXPERTS, 1, 2 * D_FF)
    bd = b_down[0].reshape(N_EXPERTS, 1, D_MODEL)
    row = lambda v: v.reshape(1, -1)
    inproj_w = (row(norm_mix[0]), wqkv, wz, wba, wglu, wgate, alog16, dtb16)
    wr_hi = w_router[0].astype(BF16)
    wr_lo = (w_router[0] - wr_hi.astype(F32)).astype(BF16)
    mix_w = (wgo, wco, wout, row(norm_ffn[0]), jnp.concatenate([wr_hi, wr_lo], axis=1), row(b_router[0]))
    ple_w = (row(norm_ple[0]), wpg, wpp, row(norm_final))
    conf_w = (conf_dw_w[0], row(conf_dw_b[0]), row(conf_ln_g[0]), row(conf_ln_b[0]))

    xp = x_prompt.reshape(n_p, D_MODEL)
    qkv_p, z_p, bg_p, u_p, ga_p, gb_p = _inproj(xp, *inproj_w, tm=256)
    qkv_p3 = qkv_p.reshape(bsz, seq, QKV_CH)
    u_p3 = u_p.reshape(bsz, seq, CONF_CH)
    o_p, s_p = _gdn_prompt(qkv_p3, z_p.reshape(bsz, seq, V_DIM), bg_p, conv_qkv_w[0], row(gdn_norm[0]))
    cv_p = _conf_prompt(u_p3, *conf_w)

    xs = x_sample.reshape(n_s, D_MODEL)
    qkv_s, z_s, bg_s, u_s, ga_s, gb_s = _inproj(xs, *inproj_w, tm=n_s)
    o_s, s_s = _gdn_sample(qkv_s, jnp.swapaxes(state_qkv_conv[0], 0, 1), z_s, bg_s, state_gdn[0],
                           conv_qkv_w[0], row(gdn_norm[0]))
    cv_s = _conf_sample(u_s, jnp.swapaxes(state_conf_conv[0], 0, 1), *conf_w)

    tm_p = 512
    x1_p, hm3_p, ti_p, tw_p = _mix(xp, o_p.reshape(n_p, V_DIM), cv_p.reshape(n_p, CONF_CH), ga_p, gb_p, *mix_w,
                                   tm=tm_p)
    x1_s, hm3_s, ti_s, tw_s = _mix(xs, o_s, cv_s, ga_s, gb_s, *mix_w, tm=n_s)
    routing = _route(jnp.concatenate([ti_p, ti_s], axis=0)[:, :TOP_K])
    y2d = _experts(jnp.concatenate([hm3_p, hm3_s], axis=0), *routing, w_gate_up[0], bgu, w_down[0], bd)
    y_p = _ple(x1_p, y2d, tw_p, p_prompt[0].reshape(n_p, PLE_DIM), *ple_w, tm=tm_p, row_offset=0)
    y_s = _ple(x1_s, y2d, tw_s, p_sample[0].reshape(n_s, PLE_DIM), *ple_w, tm=n_s, row_offset=n_p)

    new_qkv_s = jnp.concatenate([state_qkv_conv[0][:, 1:], qkv_s[:, None, :]], axis=1)
    new_conf_s = jnp.concatenate([state_conf_conv[0][:, 1:], u_s[:, None, :]], axis=1)
    return (y_p.reshape(bsz, seq, D_MODEL), y_s.reshape(n_s, 1, D_MODEL),
            s_p[None], qkv_p3[:, seq - (SHORT_CONV - 1):][None], u_p3[:, seq - (CONF_K - 1):][None],
            s_s[None], new_qkv_s[None], new_conf_s[None])
```

```python
import functools

import jax
import jax.numpy as jnp
from jax import lax
from jax.experimental import pallas as pl
from jax.experimental.pallas import tpu as pltpu

F32 = jnp.float32
BF16 = jnp.bfloat16

D_MODEL = 1024
GDN_HEADS = 8
GDN_DK = 128
GDN_DV = 128
QK_DIM = GDN_HEADS * GDN_DK
V_DIM = GDN_HEADS * GDN_DV
QKV_CH = 2 * QK_DIM + V_DIM
SHORT_CONV = 4
CHUNK = 64
CONF_CH = D_MODEL // 2
CONF_K = 31
N_EXPERTS = 32
TOP_K = 4
D_FF = D_MODEL
SWIGLU_LIMIT = 7.0
SWIGLU_ALPHA = 1.702
PLE_DIM = 256
NORM_EPS = 1e-6
LN_EPS = 1e-5

OFF_Z = QKV_CH
OFF_BETA = OFF_Z + V_DIM
OFF_GLU = OFF_BETA + 2 * GDN_HEADS
OFF_GATE = OFF_GLU + 2 * CONF_CH

VMEM_LIMIT_BYTES = 52 * 1024 * 1024
SUBLANES = 8
LANES = 128
CONF_HALO = 32


def _dot(a, b):
    return jnp.dot(a, b, preferred_element_type=F32)


def _bdot(a, b):
    return jnp.dot(a.astype(BF16), b.astype(BF16), preferred_element_type=F32)


def _split3(x):
    hi = x.astype(BF16)
    r1 = x - hi.astype(F32)
    mid = r1.astype(BF16)
    lo = (r1 - mid.astype(F32)).astype(BF16)
    return hi, mid, lo


def _mask_dot(mask, x, mask_first):
    m = jnp.where(mask, 1.0, 0.0).astype(BF16)
    parts = [_dot(m, p) if mask_first else _dot(p, m) for p in _split3(x)]
    return parts[0] + (parts[1] + parts[2])


def _sigmoid(x):
    return 1.0 / (1.0 + jnp.exp(-x))


def _silu(x):
    return x * _sigmoid(x)


def _rms(x, g):
    return x * lax.rsqrt(jnp.mean(x * x, axis=-1, keepdims=True) + NORM_EPS) * g


def _const_spec(shape):
    nd = len(shape)
    return pl.BlockSpec(shape, lambda *_: (0,) * nd, pipeline_mode=pl.Buffered(1))


def _params(sem):
    return pltpu.CompilerParams(dimension_semantics=sem, vmem_limit_bytes=VMEM_LIMIT_BYTES)


def _inproj_kernel(x_ref, nw_ref, wqkv_ref, wz_ref, wba_ref, wglu_ref, wgate_ref, alog_ref, dtb_ref,
                   qkv_ref, z_ref, bg_ref, u_ref, ga_ref, gb_ref):
    a = _rms(x_ref[...], nw_ref[...]).astype(BF16)
    for c in range(QKV_CH // D_MODEL):
        cols = slice(c * D_MODEL, (c + 1) * D_MODEL)
        qkv_ref[:, cols] = _dot(a, wqkv_ref[:, cols])
    z_ref[...] = _dot(a, wz_ref[...])
    ba = _dot(a, wba_ref[...])
    lane = lax.broadcasted_iota(jnp.int32, ba.shape, 1)
    t = ba + dtb_ref[...]
    softplus = jnp.maximum(t, 0.0) + jnp.log(1.0 + jnp.exp(-jnp.abs(t)))
    bg_ref[...] = jnp.where(lane < GDN_HEADS, _sigmoid(ba), -jnp.exp(alog_ref[...]) * softplus)
    glu = _dot(a, wglu_ref[...])
    u_ref[...] = glu[:, :CONF_CH] * _sigmoid(glu[:, CONF_CH:])
    ga_ref[...] = _sigmoid(_dot(a, wgate_ref[:, :D_MODEL]))
    gb_ref[...] = _sigmoid(_dot(a, wgate_ref[:, D_MODEL:]))


def _inproj(x, norm_w, wqkv, wz, wba, wglu, wgate, alog16, dtb16, tm):
    n = x.shape[0]
    row = lambda w: pl.BlockSpec((tm, w), lambda i: (i, 0))
    return pl.pallas_call(
        _inproj_kernel,
        grid=(n // tm,),
        in_specs=[row(D_MODEL), _const_spec((1, D_MODEL)), _const_spec(wqkv.shape), _const_spec(wz.shape),
                  _const_spec(wba.shape), _const_spec(wglu.shape), _const_spec(wgate.shape),
                  _const_spec((1, 2 * GDN_HEADS)), _const_spec((1, 2 * GDN_HEADS))],
        out_specs=[row(QKV_CH), row(V_DIM), row(2 * GDN_HEADS), row(CONF_CH), row(D_MODEL), row(D_MODEL)],
        out_shape=[jax.ShapeDtypeStruct((n, w), F32)
                   for w in (QKV_CH, V_DIM, 2 * GDN_HEADS, CONF_CH, D_MODEL, D_MODEL)],
        compiler_params=_params(("parallel",)),
        name="inproj",
    )(x, norm_w, wqkv, wz, wba, wglu, wgate, alog16, dtb16)


GDN_GROUP = 4
PREP_CHUNKS = 4
GROUP_ROWS = GDN_GROUP * CHUNK
N_GROUPS = GDN_HEADS // GDN_GROUP


def _block_unit_lower_inverse(a, eye):
    b = -a
    t = eye + b
    p = _bdot(b, b)
    covered = 2
    while covered < CHUNK:
        covered *= 2
        if covered < CHUNK:
            r = _bdot(jnp.concatenate([t, p], axis=0), p)
            t = t + r[:GROUP_ROWS]
            p = r[GROUP_ROWS:]
        else:
            t = t + _bdot(t, p)
    return t


def _gdn_prep_kernel(qkv_ref, halo_ref, bg_ref, bgt_ref, cw_ref, wqg_ref, kdt_ref, qk_ref, u_ref, gl_ref, ext_ref):
    c = pl.program_id(1)
    ext_ref[0:SUBLANES, :] = jnp.where(c > 0, halo_ref[...], 0.0)
    ext_ref[SUBLANES:SUBLANES + PREP_CHUNKS * CHUNK, :] = qkv_ref[...]
    for n in range(PREP_CHUNKS):
        _gdn_prep_chunk(n, bg_ref, bgt_ref, cw_ref, wqg_ref, kdt_ref, qk_ref, u_ref, gl_ref, ext_ref)


def _gdn_prep_chunk(n, bg_ref, bgt_ref, cw_ref, wqg_ref, kdt_ref, qk_ref, u_ref, gl_ref, ext_ref):
    base = SUBLANES + n * CHUNK

    def convsilu(c0):
        cols = slice(c0, c0 + LANES)
        acc = cw_ref[SHORT_CONV - 1:SHORT_CONV, cols] * ext_ref[base:base + CHUNK, cols]
        for j in range(SHORT_CONV - 1):
            r0 = base - (SHORT_CONV - 1) + j
            acc = acc + cw_ref[j:j + 1, cols] * ext_ref[r0:r0 + CHUNK, cols]
        return _silu(acc)

    r64 = lax.broadcasted_iota(jnp.int32, (CHUNK, CHUNK), 0)
    c64 = lax.broadcasted_iota(jnp.int32, (CHUNK, CHUNK), 1)
    bg = bg_ref[n]
    gc_all = _mask_dot(r64 >= c64, bg, mask_first=True)
    gct_all = _mask_dot(r64 <= c64, bgt_ref[n], mask_first=False)
    gl_ref[n] = gc_all[CHUNK - 1:CHUNK, :]

    row = lax.broadcasted_iota(jnp.int32, (GROUP_ROWS, GROUP_ROWS), 0)
    col = lax.broadcasted_iota(jnp.int32, (GROUP_ROWS, GROUP_ROWS), 1)
    blk = row - col + (col & (CHUNK - 1))
    same = (blk >= 0) & (blk < CHUNK)
    tril = same & (row >= col)
    strict = same & (row > col)
    eye = (row == col).astype(F32)
    nt = (((1,), (1,)), ((), ()))

    for g in range(N_GROUPS):
        heads = range(g * GDN_GROUP, (g + 1) * GDN_GROUP)
        stack = lambda f: jnp.concatenate([f(h) for h in heads], axis=0)
        q = stack(lambda h: convsilu(h * GDN_DK))
        k = stack(lambda h: convsilu(QK_DIM + h * GDN_DK))
        v = stack(lambda h: convsilu(2 * QK_DIM + h * GDN_DV))
        q = q * lax.rsqrt(jnp.sum(q * q, axis=-1, keepdims=True) + NORM_EPS) * (GDN_DK ** -0.5)
        k = k * lax.rsqrt(jnp.sum(k * k, axis=-1, keepdims=True) + NORM_EPS)
        beta = stack(lambda h: bg[:, h:h + 1])
        gc = stack(lambda h: gc_all[:, GDN_HEADS + h:GDN_HEADS + h + 1])
        g_last = stack(lambda h: jnp.broadcast_to(
            gc_all[CHUNK - 1:CHUNK, GDN_HEADS + h:GDN_HEADS + h + 1], (CHUNK, 1)))
        gct = jnp.concatenate([gct_all[GDN_HEADS + h:GDN_HEADS + h + 1, :] for h in heads], axis=1)
        decay = jnp.exp(jnp.where(tril, gc - gct, -jnp.inf))
        kb = k * beta
        kq_k = lax.dot_general(jnp.concatenate([kb.astype(BF16), q.astype(BF16)], axis=0), k.astype(BF16), nt,
                               preferred_element_type=F32)
        kk = kq_k[:GROUP_ROWS]
        t_inv = _block_unit_lower_inverse(jnp.where(strict, kk * decay, 0.0), eye)
        eg = jnp.exp(gc)
        uw = _bdot(t_inv, jnp.concatenate([v * beta, kb * eg], axis=1))
        u_ref[n, g] = uw[:, :GDN_DV]
        w = uw[:, GDN_DV:].astype(BF16)
        qg = (q * eg).astype(BF16)
        for i, h in enumerate(heads):
            rows = slice(i * CHUNK, (i + 1) * CHUNK)
            wqg_ref[n, h, 0:CHUNK, :] = w[rows]
            wqg_ref[n, h, CHUNK:2 * CHUNK, :] = qg[rows]
        qk_ref[n, g] = (kq_k[GROUP_ROWS:] * decay).astype(BF16)
        kdt_ref[n, g] = (k * jnp.exp(g_last - gc)).T.astype(BF16)


def _gdn_scan_kernel(wqg_ref, kdt_ref, qk_ref, u_ref, gl_ref, z_ref, gn_ref, o_ref, s_out_ref, s_ref):
    c = pl.program_id(1)

    @pl.when(c == 0)
    def _():
        s_ref[...] = jnp.zeros(s_ref.shape, F32)

    zeros = jnp.zeros((CHUNK, GDN_DV), BF16)
    for n in range(PREP_CHUNKS):
        dl_all = jnp.exp(gl_ref[n])
        trows = slice(n * CHUNK, (n + 1) * CHUNK)
        for g in range(N_GROUPS):
            heads = list(range(g * GDN_GROUP, (g + 1) * GDN_GROUP))
            v_new, q_s = [], []
            for i, h in enumerate(heads):
                r = _dot(wqg_ref[n, h], s_ref[h].astype(BF16))
                v_new.append((u_ref[n, g, i * CHUNK:(i + 1) * CHUNK, :] - r[:CHUNK]).astype(BF16))
                q_s.append(r[CHUNK:])
            o_all = jnp.concatenate(q_s, axis=0) + _dot(qk_ref[n, g], jnp.concatenate(v_new, axis=0))
            for i, h in enumerate(heads):
                vm = jnp.concatenate([v_new[j] if j == i else zeros for j in range(GDN_GROUP)], axis=0)
                s_ref[h] = s_ref[h] * dl_all[:, GDN_HEADS + h:GDN_HEADS + h + 1] + _dot(kdt_ref[n, g], vm)
                o = _rms(o_all[i * CHUNK:(i + 1) * CHUNK], gn_ref[...])
                cols = slice(h * GDN_DV, (h + 1) * GDN_DV)
                o_ref[trows, cols] = (o * _silu(z_ref[trows, cols])).astype(o_ref.dtype)

    @pl.when(c == pl.num_programs(1) - 1)
    def _():
        s_out_ref[...] = s_ref[...]


def _gdn_prompt(qkv, z, bg, conv_w, gdn_norm):
    b, l, _ = qkv.shape
    nc = l // CHUNK
    bg4 = bg.reshape(b, nc, CHUNK, 2 * GDN_HEADS)
    bgt4 = jnp.swapaxes(bg4, 2, 3)
    halo_blocks = CHUNK // SUBLANES
    grid = (b, nc // PREP_CHUNKS)
    seq = lambda w: pl.BlockSpec((None, PREP_CHUNKS * CHUNK, w), lambda i, c: (i, c, 0))
    chunks = lambda *s: pl.BlockSpec((None, PREP_CHUNKS) + s, lambda i, c: (i, c) + (0,) * len(s))
    inter_shapes = [((GDN_HEADS, 2 * CHUNK, GDN_DK), BF16), ((N_GROUPS, GDN_DK, GROUP_ROWS), BF16),
                    ((N_GROUPS, GROUP_ROWS, GROUP_ROWS), BF16), ((N_GROUPS, GROUP_ROWS, GDN_DV), F32),
                    ((1, 2 * GDN_HEADS), F32)]
    inter_specs = [chunks(*s) for s, _ in inter_shapes]
    wqg, kdt, qk, u, gl = pl.pallas_call(
        _gdn_prep_kernel,
        grid=grid,
        in_specs=[seq(QKV_CH),
                  pl.BlockSpec((None, SUBLANES, QKV_CH),
                               lambda i, c: (i, jnp.maximum(c * PREP_CHUNKS * halo_blocks - 1, 0), 0)),
                  chunks(CHUNK, 2 * GDN_HEADS), chunks(2 * GDN_HEADS, CHUNK), _const_spec(conv_w.shape)],
        out_specs=inter_specs,
        out_shape=[jax.ShapeDtypeStruct((b, nc) + s, d) for s, d in inter_shapes],
        scratch_shapes=[pltpu.VMEM((SUBLANES + PREP_CHUNKS * CHUNK, QKV_CH), F32)],
        compiler_params=_params(("parallel", "parallel")),
        name="gdn_prep",
    )(qkv, qkv, bg4, bgt4, conv_w)
    return pl.pallas_call(
        _gdn_scan_kernel,
        grid=grid,
        in_specs=inter_specs + [seq(V_DIM), _const_spec((1, GDN_DV))],
        out_specs=[seq(V_DIM),
                   pl.BlockSpec((None, GDN_HEADS, GDN_DK, GDN_DV), lambda i, c: (i, 0, 0, 0))],
        out_shape=[jax.ShapeDtypeStruct((b, l, V_DIM), BF16),
                   jax.ShapeDtypeStruct((b, GDN_HEADS, GDN_DK, GDN_DV), F32)],
        scratch_shapes=[pltpu.VMEM((GDN_HEADS, GDN_DK, GDN_DV), F32)],
        compiler_params=_params(("parallel", "arbitrary")),
        name="gdn_scan",
    )(wqg, kdt, qk, u, gl, z, gdn_norm)


GDN_SAMPLE_ROWS = 8


def _gdn_sample_kernel(raw_ref, cs_ref, z_ref, bg_ref, s_in_ref, cw_ref, gn_ref, o_ref, s_out_ref):
    conv = cw_ref[SHORT_CONV - 1:SHORT_CONV, :] * raw_ref[...]
    for j in range(SHORT_CONV - 1):
        conv = conv + cw_ref[j:j + 1, :] * cs_ref[j]
    qkv = _silu(conv)
    bg = bg_ref[...]
    for h in range(GDN_HEADS):
        q = qkv[:, h * GDN_DK:(h + 1) * GDN_DK]
        k = qkv[:, QK_DIM + h * GDN_DK:QK_DIM + (h + 1) * GDN_DK]
        v = qkv[:, 2 * QK_DIM + h * GDN_DV:2 * QK_DIM + (h + 1) * GDN_DV]
        q = q * lax.rsqrt(jnp.sum(q * q, axis=-1, keepdims=True) + NORM_EPS) * (GDN_DK ** -0.5)
        k = k * lax.rsqrt(jnp.sum(k * k, axis=-1, keepdims=True) + NORM_EPS)
        beta = bg[:, h:h + 1]
        eg = jnp.exp(bg[:, GDN_HEADS + h:GDN_HEADS + h + 1])
        u = v * beta
        qk = jnp.sum(q * k, axis=-1, keepdims=True)
        w_t = (k * (beta * eg)).T
        qg_t = (q * eg).T
        k_t = k.T
        o_rows = []
        for r in range(GDN_SAMPLE_ROWS):
            s = s_in_ref[r, h]
            w_s = jnp.sum(w_t[:, r:r + 1] * s, axis=0, keepdims=True)
            q_s = jnp.sum(qg_t[:, r:r + 1] * s, axis=0, keepdims=True)
            v_new = u[r:r + 1] - w_s
            o_rows.append(q_s + qk[r:r + 1] * v_new)
            s_out_ref[r, h] = s * eg[r:r + 1] + k_t[:, r:r + 1] * v_new
        o = _rms(jnp.concatenate(o_rows, axis=0), gn_ref[...])
        cols = slice(h * GDN_DV, (h + 1) * GDN_DV)
        o_ref[:, cols] = (o * _silu(z_ref[:, cols])).astype(o_ref.dtype)


def _gdn_sample(raw, conv_state_t, z, bg, state, conv_w, gdn_norm):
    n = raw.shape[0]
    rb = GDN_SAMPLE_ROWS
    row = lambda w: pl.BlockSpec((rb, w), lambda i: (i, 0))
    st = pl.BlockSpec((rb, GDN_HEADS, GDN_DK, GDN_DV), lambda i: (i, 0, 0, 0))
    return pl.pallas_call(
        _gdn_sample_kernel,
        grid=(n // rb,),
        in_specs=[row(QKV_CH), pl.BlockSpec((SHORT_CONV - 1, rb, QKV_CH), lambda i: (0, i, 0)),
                  row(V_DIM), row(2 * GDN_HEADS), st, _const_spec(conv_w.shape), _const_spec((1, GDN_DV))],
        out_specs=[row(V_DIM), st],
        out_shape=[jax.ShapeDtypeStruct((n, V_DIM), BF16), jax.ShapeDtypeStruct(state.shape, F32)],
        compiler_params=_params(("parallel",)),
        name="gdn_sample",
    )(raw, conv_state_t, z, bg, state, conv_w, gdn_norm)


CONF_TILE = 256
CONF_ROWS = 64


def _ln_silu(x, g, b):
    mu = jnp.mean(x, axis=-1, keepdims=True)
    xc = x - mu
    var = jnp.mean(xc * xc, axis=-1, keepdims=True)
    return _silu(xc * lax.rsqrt(var + LN_EPS) * g + b)


def _conf_prompt_kernel(u_ref, w_ref, b_ref, lng_ref, lnb_ref, cv_ref, ubuf_ref, sh_ref, acc_ref):
    @pl.when(pl.program_id(1) == 0)
    def _():
        ubuf_ref[0:CONF_HALO, :] = jnp.zeros((CONF_HALO, CONF_CH), F32)

    ubuf_ref[CONF_HALO:CONF_HALO + CONF_TILE, :] = u_ref[...]
    span = CONF_HALO + CONF_TILE - SUBLANES
    for k in range(1, SUBLANES):
        sh_ref[k - 1, 0:span, :] = ubuf_ref[k:k + span, :]
    base = CONF_HALO - (CONF_K - 1)
    for rb in range(CONF_TILE // CONF_ROWS):
        for cb in range(CONF_CH // LANES):
            cols = slice(cb * LANES, (cb + 1) * LANES)
            acc = b_ref[:, cols]
            for j in range(CONF_K):
                k = (base + j) % SUBLANES
                r0 = base + j - k + rb * CONF_ROWS
                rows = slice(r0, r0 + CONF_ROWS)
                src = ubuf_ref[rows, cols] if k == 0 else sh_ref[k - 1, rows, cols]
                acc = acc + w_ref[j:j + 1, cols] * src
            acc_ref[rb * CONF_ROWS:(rb + 1) * CONF_ROWS, cols] = acc
    ubuf_ref[0:CONF_HALO, :] = ubuf_ref[CONF_TILE:CONF_TILE + CONF_HALO, :]
    cv_ref[...] = _ln_silu(acc_ref[...], lng_ref[...], lnb_ref[...]).astype(cv_ref.dtype)


def _conf_prompt(u, w, b, ln_g, ln_b):
    bsz, l, _ = u.shape
    seq = pl.BlockSpec((None, CONF_TILE, CONF_CH), lambda i, t: (i, t, 0))
    return pl.pallas_call(
        _conf_prompt_kernel,
        grid=(bsz, l // CONF_TILE),
        in_specs=[seq, _const_spec(w.shape), _const_spec((1, CONF_CH)), _const_spec((1, CONF_CH)),
                  _const_spec((1, CONF_CH))],
        out_specs=seq,
        out_shape=jax.ShapeDtypeStruct((bsz, l, CONF_CH), BF16),
        scratch_shapes=[pltpu.VMEM((CONF_HALO + CONF_TILE, CONF_CH), F32),
                        pltpu.VMEM((SUBLANES - 1, CONF_HALO + CONF_TILE - SUBLANES, CONF_CH), F32),
                        pltpu.VMEM((CONF_TILE, CONF_CH), F32)],
        compiler_params=_params(("parallel", "arbitrary")),
        name="conf_prompt",
    )(u, w, b, ln_g, ln_b)


def _conf_sample_kernel(u_ref, cs_ref, w_ref, b_ref, lng_ref, lnb_ref, cv_ref):
    acc = w_ref[CONF_K - 1:CONF_K, :] * u_ref[...] + b_ref[...]
    for j in range(CONF_K - 1):
        acc = acc + w_ref[j:j + 1, :] * cs_ref[j]
    cv_ref[...] = _ln_silu(acc, lng_ref[...], lnb_ref[...]).astype(cv_ref.dtype)


def _conf_sample(u, conf_state_t, w, b, ln_g, ln_b):
    n = u.shape[0]
    return pl.pallas_call(
        _conf_sample_kernel,
        out_shape=jax.ShapeDtypeStruct((n, CONF_CH), BF16),
        compiler_params=pltpu.CompilerParams(vmem_limit_bytes=VMEM_LIMIT_BYTES),
        name="conf_sample",
    )(u, conf_state_t, w, b, ln_g, ln_b)


MOE_TILE = 256
SLAB = D_MODEL // LANES


def _mix_kernel(x_ref, o_ref, cv_ref, ga_ref, gb_ref, wgo_ref, wco_ref, wout_ref, nffn_ref, wr_ref, br_ref,
                x1_ref, hm3_ref, topi_ref, topw_ref):
    tm = x_ref.shape[0]
    branch_a = _dot(o_ref[...], wgo_ref[...])
    branch_b = _dot(cv_ref[...], wco_ref[...])
    merged = ga_ref[...] * branch_a + gb_ref[...] * branch_b
    x1 = x_ref[...] + _bdot(merged, wout_ref[...])
    x1_ref[...] = x1
    hm = _rms(x1, nffn_ref[...])
    for s in range(SLAB):
        hm3_ref[pl.ds(s, tm, stride=SLAB), :] = hm[:, s * LANES:(s + 1) * LANES]
    hm_hi = hm.astype(BF16)
    hm_lo = (hm - hm_hi.astype(F32)).astype(BF16)
    by_hi = _dot(hm_hi, wr_ref[...])
    logits = (by_hi[:, :N_EXPERTS] + (by_hi[:, N_EXPERTS:] + _dot(hm_lo, wr_ref[:, :N_EXPERTS]))) + br_ref[...]
    lane = lax.broadcasted_iota(jnp.int32, logits.shape, 1)
    work = logits
    top_v = []
    top_i = jnp.zeros(logits.shape, jnp.int32)
    for k in range(TOP_K):
        m = jnp.max(work, axis=-1, keepdims=True)
        first = jnp.min(jnp.where(work == m, lane, N_EXPERTS), axis=-1, keepdims=True)
        top_v.append(m)
        top_i = jnp.where(lane == k, first, top_i)
        work = jnp.where(lane == first, -jnp.inf, work)
    ex = [jnp.exp(v - top_v[0]) for v in top_v]
    den = ex[0]
    for e in ex[1:]:
        den = den + e
    top_w = jnp.zeros(logits.shape, F32)
    for k, e in enumerate(ex):
        top_w = jnp.where(lane == k, e / den, top_w)
    topi_ref[...] = top_i
    topw_ref[...] = top_w


def _mix(x, o, cv, ga, gb, wgo, wco, wout, norm_ffn, w_router, b_router, tm):
    n = x.shape[0]
    row = lambda w: pl.BlockSpec((tm, w), lambda i: (i, 0))
    return pl.pallas_call(
        _mix_kernel,
        grid=(n // tm,),
        in_specs=[row(D_MODEL), row(V_DIM), row(CONF_CH), row(D_MODEL), row(D_MODEL),
                  _const_spec(wgo.shape), _const_spec(wco.shape), _const_spec(wout.shape),
                  _const_spec((1, D_MODEL)), _const_spec(w_router.shape), _const_spec((1, N_EXPERTS))],
        out_specs=[row(D_MODEL), pl.BlockSpec((tm * SLAB, LANES), lambda i: (i, 0)), row(N_EXPERTS), row(N_EXPERTS)],
        out_shape=[jax.ShapeDtypeStruct((n, D_MODEL), F32), jax.ShapeDtypeStruct((n * SLAB, LANES), F32),
                   jax.ShapeDtypeStruct((n, N_EXPERTS), jnp.int32), jax.ShapeDtypeStruct((n, N_EXPERTS), F32)],
        compiler_params=_params(("parallel",)),
        name="mix",
    )(x, o, cv, ga, gb, wgo, wco, wout, norm_ffn, w_router, b_router)


ROUTE_BLOCK_MAX = 16384


def _equal_blocks(n):
    for d in range(1, n // LANES + 1):
        if n % d == 0 and n // d <= ROUTE_BLOCK_MAX and (n // d) % LANES == 0:
            return d
    raise ValueError(f"cannot split {n} elements into lane-aligned blocks")


def _route_rows_kernel(lo_ref, hi_ref, pos_ref, rp_ref):
    s = pl.program_id(0)
    pairs_per = pos_ref.shape[-1]

    @pl.when(s == 0)
    def _():
        def fill(r, carry):
            rp_ref[r] = -1
            return carry
        for i in range(lo_ref.shape[0]):
            lax.fori_loop(lo_ref[i], hi_ref[i], fill, 0)

    @pl.when(s > 0)
    def _():
        base = (s - 1) * pairs_per

        def place(i, carry):
            for j in range(SUBLANES):
                p = i * SUBLANES + j
                rp_ref[pos_ref[0, p]] = base + p
            return carry
        lax.fori_loop(0, pairs_per // SUBLANES, place, 0)


def _route(top_i):
    n = top_i.shape[0]
    pairs = n * TOP_K
    n_tiles = pairs // MOE_TILE + N_EXPERTS
    e_flat = top_i.reshape(pairs)
    onehot = (e_flat[:, None] == jnp.arange(N_EXPERTS, dtype=jnp.int32)[None, :]).astype(jnp.int32)
    csum = jnp.cumsum(onehot, axis=0)
    rank = jnp.take_along_axis(csum, e_flat[:, None], axis=1)[:, 0] - 1
    tiles_per = (csum[-1] + MOE_TILE - 1) // MOE_TILE
    tile_end = jnp.cumsum(tiles_per)
    n_used = tile_end[-1]
    pos = (tile_end - tiles_per)[e_flat] * MOE_TILE + rank
    n_rows = n_tiles * MOE_TILE
    n_place = _equal_blocks(pairs)
    pad_lo = jnp.concatenate([(tile_end - tiles_per) * MOE_TILE + csum[-1], n_used.reshape(1) * MOE_TILE])
    pad_hi = jnp.concatenate([tile_end * MOE_TILE, jnp.full((1,), n_rows, jnp.int32)])
    row_pair = pl.pallas_call(
        _route_rows_kernel,
        grid_spec=pltpu.PrefetchScalarGridSpec(
            num_scalar_prefetch=2,
            grid=(1 + n_place,),
            in_specs=[pl.BlockSpec((None, 1, pairs // n_place), lambda s, lo, hi: (jnp.maximum(s - 1, 0), 0, 0),
                                   memory_space=pltpu.SMEM)],
            out_specs=pl.BlockSpec(memory_space=pltpu.SMEM)),
        out_shape=jax.ShapeDtypeStruct((n_rows,), jnp.int32),
        compiler_params=_params(("arbitrary",)),
        name="route_rows",
    )(pad_lo, pad_hi, pos.reshape(n_place, 1, pairs // n_place))
    row_pair = jnp.concatenate([jnp.full((MOE_TILE,), -1, jnp.int32), row_pair]).reshape(n_tiles + 1, 1, MOE_TILE)
    t = jnp.minimum(jnp.arange(n_tiles + 1, dtype=jnp.int32), n_used - 1)
    tile_expert = jnp.sum((tile_end[None, :] <= t[:, None]).astype(jnp.int32), axis=1)
    return tile_expert, n_used.reshape(1), row_pair


def _experts_kernel(te_ref, nu_ref, idx0_ref, idxn_ref, idxp_ref, x3_hbm, wgu_ref, bgu_ref, wd_ref, bd_ref,
                    y_hbm, x0, x1, y0, y1, wgu16, wd16, sem_in, sem_out):
    t = pl.program_id(0)
    n_used = nu_ref[0]
    xs, ys = (x0, x1), (y0, y1)
    tile_rows = MOE_TILE * SLAB
    pair_rows = y_hbm.shape[0] - 2 * tile_rows

    def start_gather(idx_ref, c):
        for i in range(MOE_TILE):
            tok = jnp.maximum(idx_ref[0, i], 0) // TOP_K
            pltpu.make_async_copy(x3_hbm.at[tok], xs[c].at[pl.ds(i * SLAB, SLAB), :],
                                  sem_in.at[c]).start(priority=i % 2)

    def start_scatter(idx_ref, c):
        for i in range(MOE_TILE):
            pair = idx_ref[0, i]
            row = jnp.where(pair >= 0, pair * SLAB, pair_rows + c * tile_rows + i * SLAB)
            pltpu.make_async_copy(ys[c].at[pl.ds(i * SLAB, SLAB), :],
                                  y_hbm.at[pl.ds(pl.multiple_of(row, SLAB), SLAB), :],
                                  sem_out.at[c]).start(priority=i % 2)

    def wait_tile(buf, sem):
        pltpu.make_async_copy(buf, buf, sem).wait()

    @pl.when(t == 0)
    def _():
        y0[...] = jnp.zeros(y0.shape, F32)
        y1[...] = jnp.zeros(y1.shape, F32)
        pltpu.make_async_copy(y0, y_hbm.at[pl.ds(pair_rows, tile_rows), :], sem_out.at[0]).start()
        start_gather(idx0_ref, 0)

    @pl.when((t < n_used) & ((t == 0) | (te_ref[t] != te_ref[jnp.maximum(t - 1, 0)])))
    def _():
        wgu16[...] = wgu_ref[...].astype(BF16)
        wd16[...] = wd_ref[...].astype(BF16)

    def step(c):
        o = 1 - c
        wait_tile(xs[c], sem_in.at[c])
        start_gather(idxn_ref, o)
        start_scatter(idxp_ref, o)
        x = jnp.concatenate([xs[c][pl.ds(s, MOE_TILE, stride=SLAB), :] for s in range(SLAB)], axis=1)
        gu = _dot(x.astype(BF16), wgu16[...]) + bgu_ref[...]
        gate = jnp.minimum(gu[:, :D_FF], SWIGLU_LIMIT)
        up = jnp.clip(gu[:, D_FF:], -SWIGLU_LIMIT, SWIGLU_LIMIT)
        hid = (up + 1.0) * (gate * _sigmoid(SWIGLU_ALPHA * gate))
        ye = _dot(hid.astype(BF16), wd16[...]) + bd_ref[...]
        wait_tile(ys[c], sem_out.at[c])
        for s in range(SLAB):
            ys[c][pl.ds(s, MOE_TILE, stride=SLAB), :] = ye[:, s * LANES:(s + 1) * LANES]

    def drain(c):
        o = 1 - c
        start_scatter(idxp_ref, o)
        wait_tile(ys[o], sem_out.at[o])
        wait_tile(ys[c], sem_out.at[c])
        wait_tile(xs[c], sem_in.at[c])

    for c in range(2):
        pl.when((t < n_used) & (t % 2 == c))(functools.partial(step, c))
        pl.when((t == n_used) & (t % 2 == c))(functools.partial(drain, c))


def _experts(hm3, tile_expert, n_used, row_pair, wgu, bgu, wd, bd):
    n_steps = row_pair.shape[0]
    n_tiles = n_steps - 1
    pairs = hm3.shape[0] // SLAB * TOP_K
    x3 = hm3.reshape(-1, SLAB, LANES)
    idx = lambda f: pl.BlockSpec((None, 1, MOE_TILE), f, memory_space=pltpu.SMEM)
    tile_buf = pltpu.VMEM((MOE_TILE * SLAB, LANES), F32)
    grid_spec = pltpu.PrefetchScalarGridSpec(
        num_scalar_prefetch=2,
        grid=(n_steps,),
        in_specs=[idx(lambda t, te, nu: (1, 0, 0)),
                  idx(lambda t, te, nu: (jnp.minimum(t + 2, n_tiles), 0, 0)),
                  idx(lambda t, te, nu: (t, 0, 0)),
                  pl.BlockSpec(memory_space=pl.ANY),
                  pl.BlockSpec((None, D_MODEL, 2 * D_FF), lambda t, te, nu: (te[t], 0, 0)),
                  pl.BlockSpec((None, 1, 2 * D_FF), lambda t, te, nu: (te[t], 0, 0)),
                  pl.BlockSpec((None, D_FF, D_MODEL), lambda t, te, nu: (te[t], 0, 0)),
                  pl.BlockSpec((None, 1, D_MODEL), lambda t, te, nu: (te[t], 0, 0))],
        out_specs=pl.BlockSpec(memory_space=pl.ANY),
        scratch_shapes=[tile_buf, tile_buf, tile_buf, tile_buf,
                        pltpu.VMEM((D_MODEL, 2 * D_FF), BF16), pltpu.VMEM((D_FF, D_MODEL), BF16),
                        pltpu.SemaphoreType.DMA((2,)), pltpu.SemaphoreType.DMA((2,))])
    return pl.pallas_call(
        _experts_kernel,
        grid_spec=grid_spec,
        out_shape=jax.ShapeDtypeStruct(((pairs + 2 * MOE_TILE) * SLAB, LANES), F32),
        compiler_params=_params(("arbitrary",)),
        name="experts",
    )(tile_expert, n_used, row_pair, row_pair, row_pair, x3, wgu, bgu, wd, bd)


def _ple_kernel(x1_ref, y_ref, topw_ref, p_ref, nple_ref, wg_ref, wp_ref, nfin_ref, out_ref, wrep_ref, acc_ref):
    tm = x1_ref.shape[0]
    topw = topw_ref[...]
    for k in range(TOP_K):
        wrep_ref[k] = jnp.broadcast_to(topw[:, k:k + 1], (tm, LANES))

    def combine(g, carry):
        for i in range(SUBLANES):
            t = g * SUBLANES + i
            acc = None
            for k in range(TOP_K):
                rows = pl.ds(pl.multiple_of((t * TOP_K + k) * SLAB, SLAB), SLAB)
                term = wrep_ref[k, pl.ds(t, 1), :] * y_ref[rows, :]
                acc = term if acc is None else acc + term
            acc_ref[pl.ds(pl.multiple_of(t * SLAB, SLAB), SLAB), :] = acc
        return carry
    lax.fori_loop(0, tm // SUBLANES, combine, 0)
    moe = [acc_ref[pl.ds(s, tm, stride=SLAB), :] for s in range(SLAB)]
    x2 = x1_ref[...] + jnp.concatenate(moe, axis=1)
    gate = _sigmoid(_bdot(_rms(x2, nple_ref[...]), wg_ref[...]))
    x3 = x2 + _bdot(p_ref[...], wp_ref[...]) * gate
    out_ref[...] = _rms(x3, nfin_ref[...])


def _ple(x1, y2d, topw, p, norm_ple, wg, wp, norm_final, tm, row_offset):
    n = x1.shape[0]
    off = row_offset // tm
    row = lambda w: pl.BlockSpec((tm, w), lambda i: (i, 0))
    return pl.pallas_call(
        _ple_kernel,
        grid=(n // tm,),
        in_specs=[row(D_MODEL), pl.BlockSpec((tm * TOP_K * SLAB, LANES), lambda i: (i + off, 0)), row(N_EXPERTS),
                  row(PLE_DIM), _const_spec((1, D_MODEL)), _const_spec(wg.shape), _const_spec(wp.shape),
                  _const_spec((1, D_MODEL))],
        out_specs=row(D_MODEL),
        out_shape=jax.ShapeDtypeStruct((n, D_MODEL), F32),
        scratch_shapes=[pltpu.VMEM((TOP_K, tm, LANES), F32), pltpu.VMEM((tm * SLAB, LANES), F32)],
        compiler_params=_params(("parallel",)),
        name="ple",
    )(x1, y2d, topw, p, norm_ple, wg, wp, norm_final)


def kernel(x_prompt, x_sample, p_prompt, p_sample, state_gdn, state_qkv_conv, state_conf_conv, norm_mix, w_in, conv_qkv_w, a_log, dt_bias, gdn_norm, w_gdn_out, conf_dw_w, conf_dw_b, conf_ln_g, conf_ln_b, w_conf_out, w_out, norm_ffn, w_router, b_router, w_gate_up, b_gate_up, w_down, b_down, norm_ple, w_ple_gate, w_ple_proj, norm_final):
    assert w_in.shape[0] == 1, "single-layer trunk"
    bsz, seq, _ = x_prompt.shape
    n_p = bsz * seq
    n_s = x_sample.shape[0] * x_sample.shape[1]
    assert x_sample.shape[1] == 1

    w_in0 = w_in[0]
    wqkv = w_in0[:, :OFF_Z].astype(BF16)
    wz = w_in0[:, OFF_Z:OFF_BETA].astype(BF16)
    wba = w_in0[:, OFF_BETA:OFF_GLU].astype(BF16)
    wglu = w_in0[:, OFF_GLU:OFF_GATE].astype(BF16)
    wgate = w_in0[:, OFF_GATE:].astype(BF16)
    zeros_h = jnp.zeros((GDN_HEADS,), F32)
    alog16 = jnp.concatenate([zeros_h, a_log[0]]).reshape(1, 2 * GDN_HEADS)
    dtb16 = jnp.concatenate([zeros_h, dt_bias[0]]).reshape(1, 2 * GDN_HEADS)
    wgo = w_gdn_out[0].astype(BF16)
    wco = w_conf_out[0].astype(BF16)
    wout = w_out[0].astype(BF16)
    wpg = w_ple_gate[0].astype(BF16)
    wpp = w_ple_proj[0].astype(BF16)
    bgu = b_gate_up[0].reshape(N_EXPERTS, 1, 2 * D_FF)
    bd = b_down[0].reshape(N_EXPERTS, 1, D_MODEL)
    row = lambda v: v.reshape(1, -1)
    inproj_w = (row(norm_mix[0]), wqkv, wz, wba, wglu, wgate, alog16, dtb16)
    wr_hi = w_router[0].astype(BF16)
    wr_lo = (w_router[0] - wr_hi.astype(F32)).astype(BF16)
    mix_w = (wgo, wco, wout, row(norm_ffn[0]), jnp.concatenate([wr_hi, wr_lo], axis=1), row(b_router[0]))
    ple_w = (row(norm_ple[0]), wpg, wpp, row(norm_final))
    conf_w = (conf_dw_w[0], row(conf_dw_b[0]), row(conf_ln_g[0]), row(conf_ln_b[0]))

    xp = x_prompt.reshape(n_p, D_MODEL)
    qkv_p, z_p, bg_p, u_p, ga_p, gb_p = _inproj(xp, *inproj_w, tm=256)
    qkv_p3 = qkv_p.reshape(bsz, seq, QKV_CH)
    u_p3 = u_p.reshape(bsz, seq, CONF_CH)
    o_p, s_p = _gdn_prompt(qkv_p3, z_p.reshape(bsz, seq, V_DIM), bg_p, conv_qkv_w[0], row(gdn_norm[0]))
    cv_p = _conf_prompt(u_p3, *conf_w)

    xs = x_sample.reshape(n_s, D_MODEL)
    qkv_s, z_s, bg_s, u_s, ga_s, gb_s = _inproj(xs, *inproj_w, tm=n_s)
    o_s, s_s = _gdn_sample(qkv_s, jnp.swapaxes(state_qkv_conv[0], 0, 1), z_s, bg_s, state_gdn[0],
                           conv_qkv_w[0], row(gdn_norm[0]))
    cv_s = _conf_sample(u_s, jnp.swapaxes(state_conf_conv[0], 0, 1), *conf_w)

    tm_p = 512
    x1_p, hm3_p, ti_p, tw_p = _mix(xp, o_p.reshape(n_p, V_DIM), cv_p.reshape(n_p, CONF_CH), ga_p, gb_p, *mix_w,
                                   tm=tm_p)
    x1_s, hm3_s, ti_s, tw_s = _mix(xs, o_s, cv_s, ga_s, gb_s, *mix_w, tm=n_s)
    routing = _route(jnp.concatenate([ti_p, ti_s], axis=0)[:, :TOP_K])
    y2d = _experts(jnp.concatenate([hm3_p, hm3_s], axis=0), *routing, w_gate_up[0], bgu, w_down[0], bd)
    y_p = _ple(x1_p, y2d, tw_p, p_prompt[0].reshape(n_p, PLE_DIM), *ple_w, tm=tm_p, row_offset=0)
    y_s = _ple(x1_s, y2d, tw_s, p_sample[0].reshape(n_s, PLE_DIM), *ple_w, tm=n_s, row_offset=n_p)

    new_qkv_s = jnp.concatenate([state_qkv_conv[0][:, 1:], qkv_s[:, None, :]], axis=1)
    new_conf_s = jnp.concatenate([state_conf_conv[0][:, 1:], u_s[:, None, :]], axis=1)
    return (y_p.reshape(bsz, seq, D_MODEL), y_s.reshape(n_s, 1, D_MODEL),
            s_p[None], qkv_p3[:, seq - (SHORT_CONV - 1):][None], u_p3[:, seq - (CONF_K - 1):][None],
            s_s[None], new_qkv_s[None], new_conf_s[None])
```

```python
import functools

import jax
import jax.numpy as jnp
from jax import lax
from jax.experimental import pallas as pl
from jax.experimental.pallas import tpu as pltpu

F32 = jnp.float32
BF16 = jnp.bfloat16

D_MODEL = 1024
GDN_HEADS = 8
GDN_DK = 128
GDN_DV = 128
QK_DIM = GDN_HEADS * GDN_DK
V_DIM = GDN_HEADS * GDN_DV
QKV_CH = 2 * QK_DIM + V_DIM
SHORT_CONV = 4
CHUNK = 64
CONF_CH = D_MODEL // 2
CONF_K = 31
N_EXPERTS = 32
TOP_K = 4
D_FF = D_MODEL
SWIGLU_LIMIT = 7.0
SWIGLU_ALPHA = 1.702
PLE_DIM = 256
NORM_EPS = 1e-6
LN_EPS = 1e-5

OFF_Z = QKV_CH
OFF_BETA = OFF_Z + V_DIM
OFF_GLU = OFF_BETA + 2 * GDN_HEADS
OFF_GATE = OFF_GLU + 2 * CONF_CH

VMEM_LIMIT_BYTES = 52 * 1024 * 1024
SUBLANES = 8
LANES = 128
CONF_HALO = 32


def _dot(a, b):
    return jnp.dot(a, b, preferred_element_type=F32)


def _bdot(a, b):
    return jnp.dot(a.astype(BF16), b.astype(BF16), preferred_element_type=F32)


def _split3(x):
    hi = x.astype(BF16)
    r1 = x - hi.astype(F32)
    mid = r1.astype(BF16)
    lo = (r1 - mid.astype(F32)).astype(BF16)
    return hi, mid, lo


def _mask_dot(mask, x, mask_first):
    m = jnp.where(mask, 1.0, 0.0).astype(BF16)
    parts = [_dot(m, p) if mask_first else _dot(p, m) for p in _split3(x)]
    return parts[0] + (parts[1] + parts[2])


def _sigmoid(x):
    return 1.0 / (1.0 + jnp.exp(-x))


def _silu(x):
    return x * _sigmoid(x)


def _rms(x, g):
    return x * lax.rsqrt(jnp.mean(x * x, axis=-1, keepdims=True) + NORM_EPS) * g


def _const_spec(shape):
    nd = len(shape)
    return pl.BlockSpec(shape, lambda *_: (0,) * nd, pipeline_mode=pl.Buffered(1))


def _params(sem):
    return pltpu.CompilerParams(dimension_semantics=sem, vmem_limit_bytes=VMEM_LIMIT_BYTES)


def _inproj_kernel(x_ref, nw_ref, wqkv_ref, wz_ref, wba_ref, wglu_ref, wgate_ref, alog_ref, dtb_ref,
                   qkv_ref, z_ref, bg_ref, u_ref, ga_ref, gb_ref):
    a = _rms(x_ref[...], nw_ref[...]).astype(BF16)
    for c in range(QKV_CH // D_MODEL):
        cols = slice(c * D_MODEL, (c + 1) * D_MODEL)
        qkv_ref[:, cols] = _dot(a, wqkv_ref[:, cols])
    z_ref[...] = _dot(a, wz_ref[...])
    ba = _dot(a, wba_ref[...])
    lane = lax.broadcasted_iota(jnp.int32, ba.shape, 1)
    t = ba + dtb_ref[...]
    softplus = jnp.maximum(t, 0.0) + jnp.log(1.0 + jnp.exp(-jnp.abs(t)))
    bg_ref[...] = jnp.where(lane < GDN_HEADS, _sigmoid(ba), -jnp.exp(alog_ref[...]) * softplus)
    glu = _dot(a, wglu_ref[...])
    u_ref[...] = glu[:, :CONF_CH] * _sigmoid(glu[:, CONF_CH:])
    ga_ref[...] = _sigmoid(_dot(a, wgate_ref[:, :D_MODEL]))
    gb_ref[...] = _sigmoid(_dot(a, wgate_ref[:, D_MODEL:]))


def _inproj(x, norm_w, wqkv, wz, wba, wglu, wgate, alog16, dtb16, tm):
    n = x.shape[0]
    row = lambda w: pl.BlockSpec((tm, w), lambda i: (i, 0))
    return pl.pallas_call(
        _inproj_kernel,
        grid=(n // tm,),
        in_specs=[row(D_MODEL), _const_spec((1, D_MODEL)), _const_spec(wqkv.shape), _const_spec(wz.shape),
                  _const_spec(wba.shape), _const_spec(wglu.shape), _const_spec(wgate.shape),
                  _const_spec((1, 2 * GDN_HEADS)), _const_spec((1, 2 * GDN_HEADS))],
        out_specs=[row(QKV_CH), row(V_DIM), row(2 * GDN_HEADS), row(CONF_CH), row(D_MODEL), row(D_MODEL)],
        out_shape=[jax.ShapeDtypeStruct((n, w), F32)
                   for w in (QKV_CH, V_DIM, 2 * GDN_HEADS, CONF_CH, D_MODEL, D_MODEL)],
        compiler_params=_params(("parallel",)),
        name="inproj",
    )(x, norm_w, wqkv, wz, wba, wglu, wgate, alog16, dtb16)


GDN_GROUP = 4
PREP_CHUNKS = 4
GROUP_ROWS = GDN_GROUP * CHUNK
N_GROUPS = GDN_HEADS // GDN_GROUP


def _block_unit_lower_inverses(a_list, eye):
    ts = [eye - a for a in a_list]
    ps = [_bdot(a, a) for a in a_list]
    covered = 2
    while covered < CHUNK:
        covered *= 2
        if covered < CHUNK:
            rs = [_bdot(jnp.concatenate([t, p], axis=0), p) for t, p in zip(ts, ps)]
            ts = [t + r[:GROUP_ROWS] for t, r in zip(ts, rs)]
            ps = [r[GROUP_ROWS:] for r in rs]
        else:
            ts = [t + _bdot(t, p) for t, p in zip(ts, ps)]
    return ts


def _gdn_prep_kernel(qkv_ref, halo_ref, bg_ref, bgt_ref, cw_ref, wqg_ref, kdt_ref, qk_ref, u_ref, gl_ref, ext_ref):
    c = pl.program_id(1)
    ext_ref[0:SUBLANES, :] = jnp.where(c > 0, halo_ref[...], 0.0)
    ext_ref[SUBLANES:SUBLANES + PREP_CHUNKS * CHUNK, :] = qkv_ref[...]

    row = lax.broadcasted_iota(jnp.int32, (GROUP_ROWS, GROUP_ROWS), 0)
    col = lax.broadcasted_iota(jnp.int32, (GROUP_ROWS, GROUP_ROWS), 1)
    blk = row - col + (col & (CHUNK - 1))
    same = (blk >= 0) & (blk < CHUNK)
    tril = same & (row >= col)
    strict = same & (row > col)
    eye = (row == col).astype(F32)
    r64 = lax.broadcasted_iota(jnp.int32, (CHUNK, CHUNK), 0)
    c64 = lax.broadcasted_iota(jnp.int32, (CHUNK, CHUNK), 1)
    nt = (((1,), (1,)), ((), ()))

    def convsilu(n, c0):
        base = SUBLANES + n * CHUNK
        cols = slice(c0, c0 + LANES)
        acc = cw_ref[SHORT_CONV - 1:SHORT_CONV, cols] * ext_ref[base:base + CHUNK, cols]
        for j in range(SHORT_CONV - 1):
            r0 = base - (SHORT_CONV - 1) + j
            acc = acc + cw_ref[j:j + 1, cols] * ext_ref[r0:r0 + CHUNK, cols]
        return _silu(acc)

    parts = []
    for n in range(PREP_CHUNKS):
        bg = bg_ref[n]
        gc_all = _mask_dot(r64 >= c64, bg, mask_first=True)
        gct_all = _mask_dot(r64 <= c64, bgt_ref[n], mask_first=False)
        gl_ref[n] = gc_all[CHUNK - 1:CHUNK, :]
        for g in range(N_GROUPS):
            heads = range(g * GDN_GROUP, (g + 1) * GDN_GROUP)
            stack = lambda f: jnp.concatenate([f(h) for h in heads], axis=0)
            q = stack(lambda h: convsilu(n, h * GDN_DK))
            k = stack(lambda h: convsilu(n, QK_DIM + h * GDN_DK))
            v = stack(lambda h: convsilu(n, 2 * QK_DIM + h * GDN_DV))
            q = q * lax.rsqrt(jnp.sum(q * q, axis=-1, keepdims=True) + NORM_EPS) * (GDN_DK ** -0.5)
            k = k * lax.rsqrt(jnp.sum(k * k, axis=-1, keepdims=True) + NORM_EPS)
            beta = stack(lambda h: bg[:, h:h + 1])
            gc = stack(lambda h: gc_all[:, GDN_HEADS + h:GDN_HEADS + h + 1])
            g_last = stack(lambda h: jnp.broadcast_to(
                gc_all[CHUNK - 1:CHUNK, GDN_HEADS + h:GDN_HEADS + h + 1], (CHUNK, 1)))
            gct = jnp.concatenate([gct_all[GDN_HEADS + h:GDN_HEADS + h + 1, :] for h in heads], axis=1)
            decay = jnp.exp(jnp.where(tril, gc - gct, -jnp.inf))
            kb = k * beta
            kq_k = lax.dot_general(jnp.concatenate([kb.astype(BF16), q.astype(BF16)], axis=0), k.astype(BF16), nt,
                                   preferred_element_type=F32)
            a = jnp.where(strict, kq_k[:GROUP_ROWS] * decay, 0.0)
            eg = jnp.exp(gc)
            qk_ref[n, g] = (kq_k[GROUP_ROWS:] * decay).astype(BF16)
            kdt_ref[n, g] = (k * jnp.exp(g_last - gc)).T.astype(BF16)
            qg = (q * eg).astype(BF16)
            for i, h in enumerate(heads):
                wqg_ref[n, h, CHUNK:2 * CHUNK, :] = qg[i * CHUNK:(i + 1) * CHUNK]
            parts.append((n, g, a, jnp.concatenate([v * beta, kb * eg], axis=1)))

    t_invs = _block_unit_lower_inverses([a for _, _, a, _ in parts], eye)
    for (n, g, _, rhs), t_inv in zip(parts, t_invs):
        uw = _bdot(t_inv, rhs)
        u_ref[n, g] = uw[:, :GDN_DV]
        w = uw[:, GDN_DV:].astype(BF16)
        for i in range(GDN_GROUP):
            wqg_ref[n, g * GDN_GROUP + i, 0:CHUNK, :] = w[i * CHUNK:(i + 1) * CHUNK]


def _gdn_scan_kernel(wqg_ref, kdt_ref, qk_ref, u_ref, gl_ref, z_ref, gn_ref, o_ref, s_out_ref, s_ref):
    c = pl.program_id(1)

    @pl.when(c == 0)
    def _():
        s_ref[...] = jnp.zeros(s_ref.shape, F32)

    zeros = jnp.zeros((CHUNK, GDN_DV), BF16)
    for n in range(PREP_CHUNKS):
        dl_all = jnp.exp(gl_ref[n])
        trows = slice(n * CHUNK, (n + 1) * CHUNK)
        for g in range(N_GROUPS):
            heads = list(range(g * GDN_GROUP, (g + 1) * GDN_GROUP))
            v_new, q_s = [], []
            for i, h in enumerate(heads):
                r = _dot(wqg_ref[n, h], s_ref[h].astype(BF16))
                v_new.append((u_ref[n, g, i * CHUNK:(i + 1) * CHUNK, :] - r[:CHUNK]).astype(BF16))
                q_s.append(r[CHUNK:])
            o_all = jnp.concatenate(q_s, axis=0) + _dot(qk_ref[n, g], jnp.concatenate(v_new, axis=0))
            for i, h in enumerate(heads):
                vm = jnp.concatenate([v_new[j] if j == i else zeros for j in range(GDN_GROUP)], axis=0)
                s_ref[h] = s_ref[h] * dl_all[:, GDN_HEADS + h:GDN_HEADS + h + 1] + _dot(kdt_ref[n, g], vm)
                o = _rms(o_all[i * CHUNK:(i + 1) * CHUNK], gn_ref[...])
                cols = slice(h * GDN_DV, (h + 1) * GDN_DV)
                o_ref[trows, cols] = (o * _silu(z_ref[trows, cols])).astype(o_ref.dtype)

    @pl.when(c == pl.num_programs(1) - 1)
    def _():
        s_out_ref[...] = s_ref[...]


def _gdn_prompt(qkv, z, bg, conv_w, gdn_norm):
    b, l, _ = qkv.shape
    nc = l // CHUNK
    bg4 = bg.reshape(b, nc, CHUNK, 2 * GDN_HEADS)
    bgt4 = jnp.swapaxes(bg4, 2, 3)
    halo_blocks = CHUNK // SUBLANES
    grid = (b, nc // PREP_CHUNKS)
    seq = lambda w: pl.BlockSpec((None, PREP_CHUNKS * CHUNK, w), lambda i, c: (i, c, 0))
    chunks = lambda *s: pl.BlockSpec((None, PREP_CHUNKS) + s, lambda i, c: (i, c) + (0,) * len(s))
    inter_shapes = [((GDN_HEADS, 2 * CHUNK, GDN_DK), BF16), ((N_GROUPS, GDN_DK, GROUP_ROWS), BF16),
                    ((N_GROUPS, GROUP_ROWS, GROUP_ROWS), BF16), ((N_GROUPS, GROUP_ROWS, GDN_DV), F32),
                    ((1, 2 * GDN_HEADS), F32)]
    inter_specs = [chunks(*s) for s, _ in inter_shapes]
    wqg, kdt, qk, u, gl = pl.pallas_call(
        _gdn_prep_kernel,
        grid=grid,
        in_specs=[seq(QKV_CH),
                  pl.BlockSpec((None, SUBLANES, QKV_CH),
                               lambda i, c: (i, jnp.maximum(c * PREP_CHUNKS * halo_blocks - 1, 0), 0)),
                  chunks(CHUNK, 2 * GDN_HEADS), chunks(2 * GDN_HEADS, CHUNK), _const_spec(conv_w.shape)],
        out_specs=inter_specs,
        out_shape=[jax.ShapeDtypeStruct((b, nc) + s, d) for s, d in inter_shapes],
        scratch_shapes=[pltpu.VMEM((SUBLANES + PREP_CHUNKS * CHUNK, QKV_CH), F32)],
        compiler_params=_params(("parallel", "parallel")),
        name="gdn_prep",
    )(qkv, qkv, bg4, bgt4, conv_w)
    return pl.pallas_call(
        _gdn_scan_kernel,
        grid=grid,
        in_specs=inter_specs + [seq(V_DIM), _const_spec((1, GDN_DV))],
        out_specs=[seq(V_DIM),
                   pl.BlockSpec((None, GDN_HEADS, GDN_DK, GDN_DV), lambda i, c: (i, 0, 0, 0))],
        out_shape=[jax.ShapeDtypeStruct((b, l, V_DIM), BF16),
                   jax.ShapeDtypeStruct((b, GDN_HEADS, GDN_DK, GDN_DV), F32)],
        scratch_shapes=[pltpu.VMEM((GDN_HEADS, GDN_DK, GDN_DV), F32)],
        compiler_params=_params(("parallel", "arbitrary")),
        name="gdn_scan",
    )(wqg, kdt, qk, u, gl, z, gdn_norm)


GDN_SAMPLE_ROWS = 8


def _gdn_sample_kernel(raw_ref, cs_ref, z_ref, bg_ref, s_in_ref, cw_ref, gn_ref, o_ref, s_out_ref):
    conv = cw_ref[SHORT_CONV - 1:SHORT_CONV, :] * raw_ref[...]
    for j in range(SHORT_CONV - 1):
        conv = conv + cw_ref[j:j + 1, :] * cs_ref[j]
    qkv = _silu(conv)
    bg = bg_ref[...]
    for h in range(GDN_HEADS):
        q = qkv[:, h * GDN_DK:(h + 1) * GDN_DK]
        k = qkv[:, QK_DIM + h * GDN_DK:QK_DIM + (h + 1) * GDN_DK]
        v = qkv[:, 2 * QK_DIM + h * GDN_DV:2 * QK_DIM + (h + 1) * GDN_DV]
        q = q * lax.rsqrt(jnp.sum(q * q, axis=-1, keepdims=True) + NORM_EPS) * (GDN_DK ** -0.5)
        k = k * lax.rsqrt(jnp.sum(k * k, axis=-1, keepdims=True) + NORM_EPS)
        beta = bg[:, h:h + 1]
        eg = jnp.exp(bg[:, GDN_HEADS + h:GDN_HEADS + h + 1])
        u = v * beta
        qk = jnp.sum(q * k, axis=-1, keepdims=True)
        w_t = (k * (beta * eg)).T
        qg_t = (q * eg).T
        k_t = k.T
        o_rows = []
        for r in range(GDN_SAMPLE_ROWS):
            s = s_in_ref[r, h]
            w_s = jnp.sum(w_t[:, r:r + 1] * s, axis=0, keepdims=True)
            q_s = jnp.sum(qg_t[:, r:r + 1] * s, axis=0, keepdims=True)
            v_new = u[r:r + 1] - w_s
            o_rows.append(q_s + qk[r:r + 1] * v_new)
            s_out_ref[r, h] = s * eg[r:r + 1] + k_t[:, r:r + 1] * v_new
        o = _rms(jnp.concatenate(o_rows, axis=0), gn_ref[...])
        cols = slice(h * GDN_DV, (h + 1) * GDN_DV)
        o_ref[:, cols] = (o * _silu(z_ref[:, cols])).astype(o_ref.dtype)


def _gdn_sample(raw, conv_state_t, z, bg, state, conv_w, gdn_norm):
    n = raw.shape[0]
    rb = GDN_SAMPLE_ROWS
    row = lambda w: pl.BlockSpec((rb, w), lambda i: (i, 0))
    st = pl.BlockSpec((rb, GDN_HEADS, GDN_DK, GDN_DV), lambda i: (i, 0, 0, 0))
    return pl.pallas_call(
        _gdn_sample_kernel,
        grid=(n // rb,),
        in_specs=[row(QKV_CH), pl.BlockSpec((SHORT_CONV - 1, rb, QKV_CH), lambda i: (0, i, 0)),
                  row(V_DIM), row(2 * GDN_HEADS), st, _const_spec(conv_w.shape), _const_spec((1, GDN_DV))],
        out_specs=[row(V_DIM), st],
        out_shape=[jax.ShapeDtypeStruct((n, V_DIM), BF16), jax.ShapeDtypeStruct(state.shape, F32)],
        compiler_params=_params(("parallel",)),
        name="gdn_sample",
    )(raw, conv_state_t, z, bg, state, conv_w, gdn_norm)


CONF_TILE = 256
CONF_ROWS = 64


def _ln_silu(x, g, b):
    mu = jnp.mean(x, axis=-1, keepdims=True)
    xc = x - mu
    var = jnp.mean(xc * xc, axis=-1, keepdims=True)
    return _silu(xc * lax.rsqrt(var + LN_EPS) * g + b)


def _conf_prompt_kernel(u_ref, w_ref, b_ref, lng_ref, lnb_ref, cv_ref, ubuf_ref, sh_ref, acc_ref):
    @pl.when(pl.program_id(1) == 0)
    def _():
        ubuf_ref[0:CONF_HALO, :] = jnp.zeros((CONF_HALO, CONF_CH), F32)

    ubuf_ref[CONF_HALO:CONF_HALO + CONF_TILE, :] = u_ref[...]
    span = CONF_HALO + CONF_TILE - SUBLANES
    for k in range(1, SUBLANES):
        sh_ref[k - 1, 0:span, :] = ubuf_ref[k:k + span, :]
    base = CONF_HALO - (CONF_K - 1)
    for rb in range(CONF_TILE // CONF_ROWS):
        for cb in range(CONF_CH // LANES):
            cols = slice(cb * LANES, (cb + 1) * LANES)
            acc = b_ref[:, cols]
            for j in range(CONF_K):
                k = (base + j) % SUBLANES
                r0 = base + j - k + rb * CONF_ROWS
                rows = slice(r0, r0 + CONF_ROWS)
                src = ubuf_ref[rows, cols] if k == 0 else sh_ref[k - 1, rows, cols]
                acc = acc + w_ref[j:j + 1, cols] * src
            acc_ref[rb * CONF_ROWS:(rb + 1) * CONF_ROWS, cols] = acc
    ubuf_ref[0:CONF_HALO, :] = ubuf_ref[CONF_TILE:CONF_TILE + CONF_HALO, :]
    cv_ref[...] = _ln_silu(acc_ref[...], lng_ref[...], lnb_ref[...]).astype(cv_ref.dtype)


def _conf_prompt(u, w, b, ln_g, ln_b):
    bsz, l, _ = u.shape
    seq = pl.BlockSpec((None, CONF_TILE, CONF_CH), lambda i, t: (i, t, 0))
    return pl.pallas_call(
        _conf_prompt_kernel,
        grid=(bsz, l // CONF_TILE),
        in_specs=[seq, _const_spec(w.shape), _const_spec((1, CONF_CH)), _const_spec((1, CONF_CH)),
                  _const_spec((1, CONF_CH))],
        out_specs=seq,
        out_shape=jax.ShapeDtypeStruct((bsz, l, CONF_CH), BF16),
        scratch_shapes=[pltpu.VMEM((CONF_HALO + CONF_TILE, CONF_CH), F32),
                        pltpu.VMEM((SUBLANES - 1, CONF_HALO + CONF_TILE - SUBLANES, CONF_CH), F32),
                        pltpu.VMEM((CONF_TILE, CONF_CH), F32)],
        compiler_params=_params(("parallel", "arbitrary")),
        name="conf_prompt",
    )(u, w, b, ln_g, ln_b)


def _conf_sample_kernel(u_ref, cs_ref, w_ref, b_ref, lng_ref, lnb_ref, cv_ref):
    acc = w_ref[CONF_K - 1:CONF_K, :] * u_ref[...] + b_ref[...]
    for j in range(CONF_K - 1):
        acc = acc + w_ref[j:j + 1, :] * cs_ref[j]
    cv_ref[...] = _ln_silu(acc, lng_ref[...], lnb_ref[...]).astype(cv_ref.dtype)


def _conf_sample(u, conf_state_t, w, b, ln_g, ln_b):
    n = u.shape[0]
    return pl.pallas_call(
        _conf_sample_kernel,
        out_shape=jax.ShapeDtypeStruct((n, CONF_CH), BF16),
        compiler_params=pltpu.CompilerParams(vmem_limit_bytes=VMEM_LIMIT_BYTES),
        name="conf_sample",
    )(u, conf_state_t, w, b, ln_g, ln_b)


MOE_TILE = 256
SLAB = D_MODEL // LANES


def _mix_kernel(x_ref, o_ref, cv_ref, ga_ref, gb_ref, wgo_ref, wco_ref, wout_ref, nffn_ref, wr_ref, br_ref,
                x1_ref, hm3_ref, topi_ref, topw_ref):
    tm = x_ref.shape[0]
    branch_a = _dot(o_ref[...], wgo_ref[...])
    branch_b = _dot(cv_ref[...], wco_ref[...])
    merged = ga_ref[...] * branch_a + gb_ref[...] * branch_b
    x1 = x_ref[...] + _bdot(merged, wout_ref[...])
    x1_ref[...] = x1
    hm = _rms(x1, nffn_ref[...])
    for s in range(SLAB):
        hm3_ref[pl.ds(s, tm, stride=SLAB), :] = hm[:, s * LANES:(s + 1) * LANES]
    hm_hi = hm.astype(BF16)
    hm_lo = (hm - hm_hi.astype(F32)).astype(BF16)
    by_hi = _dot(hm_hi, wr_ref[...])
    logits = (by_hi[:, :N_EXPERTS] + (by_hi[:, N_EXPERTS:] + _dot(hm_lo, wr_ref[:, :N_EXPERTS]))) + br_ref[...]
    lane = lax.broadcasted_iota(jnp.int32, logits.shape, 1)
    work = logits
    top_v = []
    top_i = jnp.zeros(logits.shape, jnp.int32)
    for k in range(TOP_K):
        m = jnp.max(work, axis=-1, keepdims=True)
        first = jnp.min(jnp.where(work == m, lane, N_EXPERTS), axis=-1, keepdims=True)
        top_v.append(m)
        top_i = jnp.where(lane == k, first, top_i)
        work = jnp.where(lane == first, -jnp.inf, work)
    ex = [jnp.exp(v - top_v[0]) for v in top_v]
    den = ex[0]
    for e in ex[1:]:
        den = den + e
    top_w = jnp.zeros(logits.shape, F32)
    for k, e in enumerate(ex):
        top_w = jnp.where(lane == k, e / den, top_w)
    topi_ref[...] = top_i
    topw_ref[...] = top_w


def _mix(x, o, cv, ga, gb, wgo, wco, wout, norm_ffn, w_router, b_router, tm):
    n = x.shape[0]
    row = lambda w: pl.BlockSpec((tm, w), lambda i: (i, 0))
    return pl.pallas_call(
        _mix_kernel,
        grid=(n // tm,),
        in_specs=[row(D_MODEL), row(V_DIM), row(CONF_CH), row(D_MODEL), row(D_MODEL),
                  _const_spec(wgo.shape), _const_spec(wco.shape), _const_spec(wout.shape),
                  _const_spec((1, D_MODEL)), _const_spec(w_router.shape), _const_spec((1, N_EXPERTS))],
        out_specs=[row(D_MODEL), pl.BlockSpec((tm * SLAB, LANES), lambda i: (i, 0)), row(N_EXPERTS), row(N_EXPERTS)],
        out_shape=[jax.ShapeDtypeStruct((n, D_MODEL), F32), jax.ShapeDtypeStruct((n * SLAB, LANES), F32),
                   jax.ShapeDtypeStruct((n, N_EXPERTS), jnp.int32), jax.ShapeDtypeStruct((n, N_EXPERTS), F32)],
        compiler_params=_params(("parallel",)),
        name="mix",
    )(x, o, cv, ga, gb, wgo, wco, wout, norm_ffn, w_router, b_router)


ROUTE_BLOCK_MAX = 16384


def _equal_blocks(n):
    for d in range(1, n // LANES + 1):
        if n % d == 0 and n // d <= ROUTE_BLOCK_MAX and (n // d) % LANES == 0:
            return d
    raise ValueError(f"cannot split {n} elements into lane-aligned blocks")


def _route_rows_kernel(lo_ref, hi_ref, pos_ref, rp_ref):
    s = pl.program_id(0)
    pairs_per = pos_ref.shape[-1]

    @pl.when(s == 0)
    def _():
        def fill(r, carry):
            rp_ref[r] = -1
            return carry
        for i in range(lo_ref.shape[0]):
            lax.fori_loop(lo_ref[i], hi_ref[i], fill, 0)

    @pl.when(s > 0)
    def _():
        base = (s - 1) * pairs_per

        def place(i, carry):
            for j in range(SUBLANES):
                p = i * SUBLANES + j
                rp_ref[pos_ref[0, p]] = base + p
            return carry
        lax.fori_loop(0, pairs_per // SUBLANES, place, 0)


def _route(top_i):
    n = top_i.shape[0]
    pairs = n * TOP_K
    n_tiles = pairs // MOE_TILE + N_EXPERTS
    e_flat = top_i.reshape(pairs)
    onehot = (e_flat[:, None] == jnp.arange(N_EXPERTS, dtype=jnp.int32)[None, :]).astype(jnp.int32)
    csum = jnp.cumsum(onehot, axis=0)
    rank = jnp.take_along_axis(csum, e_flat[:, None], axis=1)[:, 0] - 1
    tiles_per = (csum[-1] + MOE_TILE - 1) // MOE_TILE
    tile_end = jnp.cumsum(tiles_per)
    n_used = tile_end[-1]
    pos = (tile_end - tiles_per)[e_flat] * MOE_TILE + rank
    n_rows = n_tiles * MOE_TILE
    n_place = _equal_blocks(pairs)
    pad_lo = jnp.concatenate([(tile_end - tiles_per) * MOE_TILE + csum[-1], n_used.reshape(1) * MOE_TILE])
    pad_hi = jnp.concatenate([tile_end * MOE_TILE, jnp.full((1,), n_rows, jnp.int32)])
    row_pair = pl.pallas_call(
        _route_rows_kernel,
        grid_spec=pltpu.PrefetchScalarGridSpec(
            num_scalar_prefetch=2,
            grid=(1 + n_place,),
            in_specs=[pl.BlockSpec((None, 1, pairs // n_place), lambda s, lo, hi: (jnp.maximum(s - 1, 0), 0, 0),
                                   memory_space=pltpu.SMEM)],
            out_specs=pl.BlockSpec(memory_space=pltpu.SMEM)),
        out_shape=jax.ShapeDtypeStruct((n_rows,), jnp.int32),
        compiler_params=_params(("arbitrary",)),
        name="route_rows",
    )(pad_lo, pad_hi, pos.reshape(n_place, 1, pairs // n_place))
    row_pair = jnp.concatenate([jnp.full((MOE_TILE,), -1, jnp.int32), row_pair]).reshape(n_tiles + 1, 1, MOE_TILE)
    t = jnp.minimum(jnp.arange(n_tiles + 1, dtype=jnp.int32), n_used - 1)
    tile_expert = jnp.sum((tile_end[None, :] <= t[:, None]).astype(jnp.int32), axis=1)
    return tile_expert, n_used.reshape(1), row_pair


def _experts_kernel(te_ref, nu_ref, idx0_ref, idxn_ref, idxp_ref, x3_hbm, wgu_ref, bgu_ref, wd_ref, bd_ref,
                    y_hbm, x0, x1, y0, y1, wgu16, wd16, sem_in, sem_out):
    t = pl.program_id(0)
    n_used = nu_ref[0]
    xs, ys = (x0, x1), (y0, y1)
    tile_rows = MOE_TILE * SLAB
    pair_rows = y_hbm.shape[0] - 2 * tile_rows

    def start_gather(idx_ref, c):
        for i in range(MOE_TILE):
            tok = jnp.maximum(idx_ref[0, i], 0) // TOP_K
            pltpu.make_async_copy(x3_hbm.at[tok], xs[c].at[pl.ds(i * SLAB, SLAB), :],
                                  sem_in.at[c]).start(priority=i % 2)

    def start_scatter(idx_ref, c):
        for i in range(MOE_TILE):
            pair = idx_ref[0, i]
            row = jnp.where(pair >= 0, pair * SLAB, pair_rows + c * tile_rows + i * SLAB)
            pltpu.make_async_copy(ys[c].at[pl.ds(i * SLAB, SLAB), :],
                                  y_hbm.at[pl.ds(pl.multiple_of(row, SLAB), SLAB), :],
                                  sem_out.at[c]).start(priority=i % 2)

    def wait_tile(buf, sem):
        pltpu.make_async_copy(buf, buf, sem).wait()

    @pl.when(t == 0)
    def _():
        y0[...] = jnp.zeros(y0.shape, F32)
        y1[...] = jnp.zeros(y1.shape, F32)
        pltpu.make_async_copy(y0, y_hbm.at[pl.ds(pair_rows, tile_rows), :], sem_out.at[0]).start()
        start_gather(idx0_ref, 0)

    @pl.when((t < n_used) & ((t == 0) | (te_ref[t] != te_ref[jnp.maximum(t - 1, 0)])))
    def _():
        wgu16[...] = wgu_ref[...].astype(BF16)
        wd16[...] = wd_ref[...].astype(BF16)

    def step(c):
        o = 1 - c
        wait_tile(xs[c], sem_in.at[c])
        start_gather(idxn_ref, o)
        start_scatter(idxp_ref, o)
        x = jnp.concatenate([xs[c][pl.ds(s, MOE_TILE, stride=SLAB), :] for s in range(SLAB)], axis=1)
        gu = _dot(x.astype(BF16), wgu16[...]) + bgu_ref[...]
        gate = jnp.minimum(gu[:, :D_FF], SWIGLU_LIMIT)
        up = jnp.clip(gu[:, D_FF:], -SWIGLU_LIMIT, SWIGLU_LIMIT)
        hid = (up + 1.0) * (gate * _sigmoid(SWIGLU_ALPHA * gate))
        ye = _dot(hid.astype(BF16), wd16[...]) + bd_ref[...]
        wait_tile(ys[c], sem_out.at[c])
        for s in range(SLAB):
            ys[c][pl.ds(s, MOE_TILE, stride=SLAB), :] = ye[:, s * LANES:(s + 1) * LANES]

    def drain(c):
        o = 1 - c
        start_scatter(idxp_ref, o)
        wait_tile(ys[o], sem_out.at[o])
        wait_tile(ys[c], sem_out.at[c])
        wait_tile(xs[c], sem_in.at[c])

    for c in range(2):
        pl.when((t < n_used) & (t % 2 == c))(functools.partial(step, c))
        pl.when((t == n_used) & (t % 2 == c))(functools.partial(drain, c))


def _experts(hm3, tile_expert, n_used, row_pair, wgu, bgu, wd, bd):
    n_steps = row_pair.shape[0]
    n_tiles = n_steps - 1
    pairs = hm3.shape[0] // SLAB * TOP_K
    x3 = hm3.reshape(-1, SLAB, LANES)
    idx = lambda f: pl.BlockSpec((None, 1, MOE_TILE), f, memory_space=pltpu.SMEM)
    tile_buf = pltpu.VMEM((MOE_TILE * SLAB, LANES), F32)
    grid_spec = pltpu.PrefetchScalarGridSpec(
        num_scalar_prefetch=2,
        grid=(n_steps,),
        in_specs=[idx(lambda t, te, nu: (1, 0, 0)),
                  idx(lambda t, te, nu: (jnp.minimum(t + 2, n_tiles), 0, 0)),
                  idx(lambda t, te, nu: (t, 0, 0)),
                  pl.BlockSpec(memory_space=pl.ANY),
                  pl.BlockSpec((None, D_MODEL, 2 * D_FF), lambda t, te, nu: (te[t], 0, 0)),
                  pl.BlockSpec((None, 1, 2 * D_FF), lambda t, te, nu: (te[t], 0, 0)),
                  pl.BlockSpec((None, D_FF, D_MODEL), lambda t, te, nu: (te[t], 0, 0)),
                  pl.BlockSpec((None, 1, D_MODEL), lambda t, te, nu: (te[t], 0, 0))],
        out_specs=pl.BlockSpec(memory_space=pl.ANY),
        scratch_shapes=[tile_buf, tile_buf, tile_buf, tile_buf,
                        pltpu.VMEM((D_MODEL, 2 * D_FF), BF16), pltpu.VMEM((D_FF, D_MODEL), BF16),
                        pltpu.SemaphoreType.DMA((2,)), pltpu.SemaphoreType.DMA((2,))])
    return pl.pallas_call(
        _experts_kernel,
        grid_spec=grid_spec,
        out_shape=jax.ShapeDtypeStruct(((pairs + 2 * MOE_TILE) * SLAB, LANES), F32),
        compiler_params=_params(("arbitrary",)),
        name="experts",
    )(tile_expert, n_used, row_pair, row_pair, row_pair, x3, wgu, bgu, wd, bd)


def _ple_kernel(x1_ref, y_ref, topw_ref, p_ref, nple_ref, wg_ref, wp_ref, nfin_ref, out_ref, wrep_ref, acc_ref):
    tm = x1_ref.shape[0]
    topw = topw_ref[...]
    for k in range(TOP_K):
        wrep_ref[k] = jnp.broadcast_to(topw[:, k:k + 1], (tm, LANES))

    def combine(g, carry):
        for i in range(SUBLANES):
            t = g * SUBLANES + i
            acc = None
            for k in range(TOP_K):
                rows = pl.ds(pl.multiple_of((t * TOP_K + k) * SLAB, SLAB), SLAB)
                term = wrep_ref[k, pl.ds(t, 1), :] * y_ref[rows, :]
                acc = term if acc is None else acc + term
            acc_ref[pl.ds(pl.multiple_of(t * SLAB, SLAB), SLAB), :] = acc
        return carry
    lax.fori_loop(0, tm // SUBLANES, combine, 0)
    moe = [acc_ref[pl.ds(s, tm, stride=SLAB), :] for s in range(SLAB)]
    x2 = x1_ref[...] + jnp.concatenate(moe, axis=1)
    gate = _sigmoid(_bdot(_rms(x2, nple_ref[...]), wg_ref[...]))
    x3 = x2 + _bdot(p_ref[...], wp_ref[...]) * gate
    out_ref[...] = _rms(x3, nfin_ref[...])


def _ple(x1, y2d, topw, p, norm_ple, wg, wp, norm_final, tm, row_offset):
    n = x1.shape[0]
    off = row_offset // tm
    row = lambda w: pl.BlockSpec((tm, w), lambda i: (i, 0))
    return pl.pallas_call(
        _ple_kernel,
        grid=(n // tm,),
        in_specs=[row(D_MODEL), pl.BlockSpec((tm * TOP_K * SLAB, LANES), lambda i: (i + off, 0)), row(N_EXPERTS),
                  row(PLE_DIM), _const_spec((1, D_MODEL)), _const_spec(wg.shape), _const_spec(wp.shape),
                  _const_spec((1, D_MODEL))],
        out_specs=row(D_MODEL),
        out_shape=jax.ShapeDtypeStruct((n, D_MODEL), F32),
        scratch_shapes=[pltpu.VMEM((TOP_K, tm, LANES), F32), pltpu.VMEM((tm * SLAB, LANES), F32)],
        compiler_params=_params(("parallel",)),
        name="ple",
    )(x1, y2d, topw, p, norm_ple, wg, wp, norm_final)


def kernel(x_prompt, x_sample, p_prompt, p_sample, state_gdn, state_qkv_conv, state_conf_conv, norm_mix, w_in, conv_qkv_w, a_log, dt_bias, gdn_norm, w_gdn_out, conf_dw_w, conf_dw_b, conf_ln_g, conf_ln_b, w_conf_out, w_out, norm_ffn, w_router, b_router, w_gate_up, b_gate_up, w_down, b_down, norm_ple, w_ple_gate, w_ple_proj, norm_final):
    assert w_in.shape[0] == 1, "single-layer trunk"
    bsz, seq, _ = x_prompt.shape
    n_p = bsz * seq
    n_s = x_sample.shape[0] * x_sample.shape[1]
    assert x_sample.shape[1] == 1

    w_in0 = w_in[0]
    wqkv = w_in0[:, :OFF_Z].astype(BF16)
    wz = w_in0[:, OFF_Z:OFF_BETA].astype(BF16)
    wba = w_in0[:, OFF_BETA:OFF_GLU].astype(BF16)
    wglu = w_in0[:, OFF_GLU:OFF_GATE].astype(BF16)
    wgate = w_in0[:, OFF_GATE:].astype(BF16)
    zeros_h = jnp.zeros((GDN_HEADS,), F32)
    alog16 = jnp.concatenate([zeros_h, a_log[0]]).reshape(1, 2 * GDN_HEADS)
    dtb16 = jnp.concatenate([zeros_h, dt_bias[0]]).reshape(1, 2 * GDN_HEADS)
    wgo = w_gdn_out[0].astype(BF16)
    wco = w_conf_out[0].astype(BF16)
    wout = w_out[0].astype(BF16)
    wpg = w_ple_gate[0].astype(BF16)
    wpp = w_ple_proj[0].astype(BF16)
    bgu = b_gate_up[0].reshape(N_EXPERTS, 1, 2 * D_FF)
    bd = b_down[0].reshape(N_EXPERTS, 1, D_MODEL)
    row = lambda v: v.reshape(1, -1)
    inproj_w = (row(norm_mix[0]), wqkv, wz, wba, wglu, wgate, alog16, dtb16)
    wr_hi = w_router[0].astype(BF16)
    wr_lo = (w_router[0] - wr_hi.astype(F32)).astype(BF16)
    mix_w = (wgo, wco, wout, row(norm_ffn[0]), jnp.concatenate([wr_hi, wr_lo], axis=1), row(b_router[0]))
    ple_w = (row(norm_ple[0]), wpg, wpp, row(norm_final))
    conf_w = (conf_dw_w[0], row(conf_dw_b[0]), row(conf_ln_g[0]), row(conf_ln_b[0]))

    xp = x_prompt.reshape(n_p, D_MODEL)
    qkv_p, z_p, bg_p, u_p, ga_p, gb_p = _inproj(xp, *inproj_w, tm=256)
    qkv_p3 = qkv_p.reshape(bsz, seq, QKV_CH)
    u_p3 = u_p.reshape(bsz, seq, CONF_CH)
    o_p, s_p = _gdn_prompt(qkv_p3, z_p.reshape(bsz, seq, V_DIM), bg_p, conv_qkv_w[0], row(gdn_norm[0]))
    cv_p = _conf_prompt(u_p3, *conf_w)

    xs = x_sample.reshape(n_s, D_MODEL)
    qkv_s, z_s, bg_s, u_s, ga_s, gb_s = _inproj(xs, *inproj_w, tm=n_s)
    o_s, s_s = _gdn_sample(qkv_s, jnp.swapaxes(state_qkv_conv[0], 0, 1), z_s, bg_s, state_gdn[0],
                           conv_qkv_w[0], row(gdn_norm[0]))
    cv_s = _conf_sample(u_s, jnp.swapaxes(state_conf_conv[0], 0, 1), *conf_w)

    tm_p = 512
    x1_p, hm3_p, ti_p, tw_p = _mix(xp, o_p.reshape(n_p, V_DIM), cv_p.reshape(n_p, CONF_CH), ga_p, gb_p, *mix_w,
                                   tm=tm_p)
    x1_s, hm3_s, ti_s, tw_s = _mix(xs, o_s, cv_s, ga_s, gb_s, *mix_w, tm=n_s)
    routing = _route(jnp.concatenate([ti_p, ti_s], axis=0)[:, :TOP_K])
    y2d = _experts(jnp.concatenate([hm3_p, hm3_s], axis=0), *routing, w_gate_up[0], bgu, w_down[0], bd)
    y_p = _ple(x1_p, y2d, tw_p, p_prompt[0].reshape(n_p, PLE_DIM), *ple_w, tm=tm_p, row_offset=0)
    y_s = _ple(x1_s, y2d, tw_s, p_sample[0].reshape(n_s, PLE_DIM), *ple_w, tm=n_s, row_offset=n_p)

    new_qkv_s = jnp.concatenate([state_qkv_conv[0][:, 1:], qkv_s[:, None, :]], axis=1)
    new_conf_s = jnp.concatenate([state_conf_conv[0][:, 1:], u_s[:, None, :]], axis=1)
    return (y_p.reshape(bsz, seq, D_MODEL), y_s.reshape(n_s, 1, D_MODEL),
            s_p[None], qkv_p3[:, seq - (SHORT_CONV - 1):][None], u_p3[:, seq - (CONF_K - 1):][None],
            s_s[None], new_qkv_s[None], new_conf_s[None])
```

```python
import functools

import jax
import jax.numpy as jnp
from jax import lax
from jax.experimental import pallas as pl
from jax.experimental.pallas import tpu as pltpu

F32 = jnp.float32
BF16 = jnp.bfloat16

D_MODEL = 1024
GDN_HEADS = 8
GDN_DK = 128
GDN_DV = 128
QK_DIM = GDN_HEADS * GDN_DK
V_DIM = GDN_HEADS * GDN_DV
QKV_CH = 2 * QK_DIM + V_DIM
SHORT_CONV = 4
CHUNK = 64
CONF_CH = D_MODEL // 2
CONF_K = 31
N_EXPERTS = 32
TOP_K = 4
D_FF = D_MODEL
SWIGLU_LIMIT = 7.0
SWIGLU_ALPHA = 1.702
PLE_DIM = 256
NORM_EPS = 1e-6
LN_EPS = 1e-5

OFF_Z = QKV_CH
OFF_BETA = OFF_Z + V_DIM
OFF_GLU = OFF_BETA + 2 * GDN_HEADS
OFF_GATE = OFF_GLU + 2 * CONF_CH

VMEM_LIMIT_BYTES = 52 * 1024 * 1024
SUBLANES = 8
LANES = 128
CONF_HALO = 32


def _dot(a, b):
    return jnp.dot(a, b, preferred_element_type=F32)


def _bdot(a, b):
    return jnp.dot(a.astype(BF16), b.astype(BF16), preferred_element_type=F32)


def _split3(x):
    hi = x.astype(BF16)
    r1 = x - hi.astype(F32)
    mid = r1.astype(BF16)
    lo = (r1 - mid.astype(F32)).astype(BF16)
    return hi, mid, lo


def _mask_dot(mask, x, mask_first):
    m = jnp.where(mask, 1.0, 0.0).astype(BF16)
    parts = [_dot(m, p) if mask_first else _dot(p, m) for p in _split3(x)]
    return parts[0] + (parts[1] + parts[2])


def _sigmoid(x):
    return 1.0 / (1.0 + jnp.exp(-x))


def _silu(x):
    return x * _sigmoid(x)


def _rms(x, g):
    return x * lax.rsqrt(jnp.mean(x * x, axis=-1, keepdims=True) + NORM_EPS) * g


def _const_spec(shape):
    nd = len(shape)
    return pl.BlockSpec(shape, lambda *_: (0,) * nd, pipeline_mode=pl.Buffered(1))


def _params(sem):
    return pltpu.CompilerParams(dimension_semantics=sem, vmem_limit_bytes=VMEM_LIMIT_BYTES)


def _inproj_kernel(x_ref, nw_ref, wqkv_ref, wz_ref, wba_ref, wglu_ref, wgate_ref, alog_ref, dtb_ref,
                   qkv_ref, z_ref, bg_ref, u_ref, ga_ref, gb_ref):
    a = _rms(x_ref[...], nw_ref[...]).astype(BF16)
    for c in range(QKV_CH // D_MODEL):
        cols = slice(c * D_MODEL, (c + 1) * D_MODEL)
        qkv_ref[:, cols] = _dot(a, wqkv_ref[:, cols])
    z_ref[...] = _dot(a, wz_ref[...])
    ba = _dot(a, wba_ref[...])
    lane = lax.broadcasted_iota(jnp.int32, ba.shape, 1)
    t = ba + dtb_ref[...]
    softplus = jnp.maximum(t, 0.0) + jnp.log(1.0 + jnp.exp(-jnp.abs(t)))
    bg_ref[...] = jnp.where(lane < GDN_HEADS, _sigmoid(ba), -jnp.exp(alog_ref[...]) * softplus)
    glu = _dot(a, wglu_ref[...])
    u_ref[...] = glu[:, :CONF_CH] * _sigmoid(glu[:, CONF_CH:])
    ga_ref[...] = _sigmoid(_dot(a, wgate_ref[:, :D_MODEL]))
    gb_ref[...] = _sigmoid(_dot(a, wgate_ref[:, D_MODEL:]))


def _inproj(x, norm_w, wqkv, wz, wba, wglu, wgate, alog16, dtb16, tm):
    n = x.shape[0]
    row = lambda w: pl.BlockSpec((tm, w), lambda i: (i, 0))
    return pl.pallas_call(
        _inproj_kernel,
        grid=(n // tm,),
        in_specs=[row(D_MODEL), _const_spec((1, D_MODEL)), _const_spec(wqkv.shape), _const_spec(wz.shape),
                  _const_spec(wba.shape), _const_spec(wglu.shape), _const_spec(wgate.shape),
                  _const_spec((1, 2 * GDN_HEADS)), _const_spec((1, 2 * GDN_HEADS))],
        out_specs=[row(QKV_CH), row(V_DIM), row(2 * GDN_HEADS), row(CONF_CH), row(D_MODEL), row(D_MODEL)],
        out_shape=[jax.ShapeDtypeStruct((n, w), F32)
                   for w in (QKV_CH, V_DIM, 2 * GDN_HEADS, CONF_CH, D_MODEL, D_MODEL)],
        compiler_params=_params(("parallel",)),
        name="inproj",
    )(x, norm_w, wqkv, wz, wba, wglu, wgate, alog16, dtb16)


GDN_GROUP = 4
PREP_CHUNKS = 4
GROUP_ROWS = GDN_GROUP * CHUNK
N_GROUPS = GDN_HEADS // GDN_GROUP


def _block_unit_lower_inverses(a_list, eye):
    ts = [eye - a for a in a_list]
    ps = [_bdot(a, a) for a in a_list]
    covered = 2
    while covered < CHUNK:
        covered *= 2
        if covered < CHUNK:
            rs = [_bdot(jnp.concatenate([t, p], axis=0), p) for t, p in zip(ts, ps)]
            ts = [t + r[:GROUP_ROWS] for t, r in zip(ts, rs)]
            ps = [r[GROUP_ROWS:] for r in rs]
        else:
            ts = [t + _bdot(t, p) for t, p in zip(ts, ps)]
    return ts


def _gdn_prep_kernel(qkv_ref, halo_ref, bg_ref, bgt_ref, cw_ref, wqg_ref, kdt_ref, qk_ref, u_ref, gl_ref, ext_ref):
    c = pl.program_id(1)
    ext_ref[0:SUBLANES, :] = jnp.where(c > 0, halo_ref[...], 0.0)
    ext_ref[SUBLANES:SUBLANES + PREP_CHUNKS * CHUNK, :] = qkv_ref[...]

    row = lax.broadcasted_iota(jnp.int32, (GROUP_ROWS, GROUP_ROWS), 0)
    col = lax.broadcasted_iota(jnp.int32, (GROUP_ROWS, GROUP_ROWS), 1)
    blk = row - col + (col & (CHUNK - 1))
    same = (blk >= 0) & (blk < CHUNK)
    tril = same & (row >= col)
    strict = same & (row > col)
    eye = (row == col).astype(F32)
    r64 = lax.broadcasted_iota(jnp.int32, (CHUNK, CHUNK), 0)
    c64 = lax.broadcasted_iota(jnp.int32, (CHUNK, CHUNK), 1)
    nt = (((1,), (1,)), ((), ()))

    def convsilu(n, c0):
        base = SUBLANES + n * CHUNK
        cols = slice(c0, c0 + LANES)
        acc = cw_ref[SHORT_CONV - 1:SHORT_CONV, cols] * ext_ref[base:base + CHUNK, cols]
        for j in range(SHORT_CONV - 1):
            r0 = base - (SHORT_CONV - 1) + j
            acc = acc + cw_ref[j:j + 1, cols] * ext_ref[r0:r0 + CHUNK, cols]
        return _silu(acc)

    parts = []
    for n in range(PREP_CHUNKS):
        bg = bg_ref[n]
        gc_all = _mask_dot(r64 >= c64, bg, mask_first=True)
        gct_all = _mask_dot(r64 <= c64, bgt_ref[n], mask_first=False)
        gl_ref[n] = gc_all[CHUNK - 1:CHUNK, :]
        for g in range(N_GROUPS):
            heads = range(g * GDN_GROUP, (g + 1) * GDN_GROUP)
            stack = lambda f: jnp.concatenate([f(h) for h in heads], axis=0)
            q = stack(lambda h: convsilu(n, h * GDN_DK))
            k = stack(lambda h: convsilu(n, QK_DIM + h * GDN_DK))
            v = stack(lambda h: convsilu(n, 2 * QK_DIM + h * GDN_DV))
            q = q * lax.rsqrt(jnp.sum(q * q, axis=-1, keepdims=True) + NORM_EPS) * (GDN_DK ** -0.5)
            k = k * lax.rsqrt(jnp.sum(k * k, axis=-1, keepdims=True) + NORM_EPS)
            beta = stack(lambda h: bg[:, h:h + 1])
            gc = stack(lambda h: gc_all[:, GDN_HEADS + h:GDN_HEADS + h + 1])
            g_last = stack(lambda h: jnp.broadcast_to(
                gc_all[CHUNK - 1:CHUNK, GDN_HEADS + h:GDN_HEADS + h + 1], (CHUNK, 1)))
            gct = jnp.concatenate([gct_all[GDN_HEADS + h:GDN_HEADS + h + 1, :] for h in heads], axis=1)
            decay = jnp.exp(jnp.where(tril, gc - gct, -jnp.inf))
            kb = k * beta
            kq_k = lax.dot_general(jnp.concatenate([kb.astype(BF16), q.astype(BF16)], axis=0), k.astype(BF16), nt,
                                   preferred_element_type=F32)
            a = jnp.where(strict, kq_k[:GROUP_ROWS] * decay, 0.0)
            eg = jnp.exp(gc)
            qk_ref[n, g] = (kq_k[GROUP_ROWS:] * decay).astype(BF16)
            kdt_ref[n, g] = (k * jnp.exp(g_last - gc)).T.astype(BF16)
            qg = (q * eg).astype(BF16)
            for i, h in enumerate(heads):
                wqg_ref[n, h, CHUNK:2 * CHUNK, :] = qg[i * CHUNK:(i + 1) * CHUNK]
            parts.append((n, g, a, jnp.concatenate([v * beta, kb * eg], axis=1)))

    t_invs = _block_unit_lower_inverses([a for _, _, a, _ in parts], eye)
    for (n, g, _, rhs), t_inv in zip(parts, t_invs):
        uw = _bdot(t_inv, rhs)
        u_ref[n, g] = uw[:, :GDN_DV]
        w = uw[:, GDN_DV:].astype(BF16)
        for i in range(GDN_GROUP):
            wqg_ref[n, g * GDN_GROUP + i, 0:CHUNK, :] = w[i * CHUNK:(i + 1) * CHUNK]


def _gdn_scan_kernel(wqg_ref, kdt_ref, qk_ref, u_ref, gl_ref, z_ref, gn_ref, o_ref, s_out_ref, s_ref):
    c = pl.program_id(1)

    @pl.when(c == 0)
    def _():
        s_ref[...] = jnp.zeros(s_ref.shape, F32)

    zeros = jnp.zeros((CHUNK, GDN_DV), BF16)
    for n in range(PREP_CHUNKS):
        dl_all = jnp.exp(gl_ref[n])
        trows = slice(n * CHUNK, (n + 1) * CHUNK)
        ws_qs = [_dot(wqg_ref[n, h], s_ref[h].astype(BF16)) for h in range(GDN_HEADS)]
        v_new = [(u_ref[n, h // GDN_GROUP, (h % GDN_GROUP) * CHUNK:(h % GDN_GROUP + 1) * CHUNK, :]
                  - ws_qs[h][:CHUNK]).astype(BF16) for h in range(GDN_HEADS)]
        o_all = [jnp.concatenate([ws_qs[h][CHUNK:] for h in range(g * GDN_GROUP, (g + 1) * GDN_GROUP)], axis=0)
                 + _dot(qk_ref[n, g], jnp.concatenate(v_new[g * GDN_GROUP:(g + 1) * GDN_GROUP], axis=0))
                 for g in range(N_GROUPS)]
        for h in range(GDN_HEADS):
            g, i = h // GDN_GROUP, h % GDN_GROUP
            vm = jnp.concatenate([v_new[h] if j == i else zeros for j in range(GDN_GROUP)], axis=0)
            s_ref[h] = s_ref[h] * dl_all[:, GDN_HEADS + h:GDN_HEADS + h + 1] + _dot(kdt_ref[n, g], vm)
        for h in range(GDN_HEADS):
            g, i = h // GDN_GROUP, h % GDN_GROUP
            o = _rms(o_all[g][i * CHUNK:(i + 1) * CHUNK], gn_ref[...])
            cols = slice(h * GDN_DV, (h + 1) * GDN_DV)
            o_ref[trows, cols] = (o * _silu(z_ref[trows, cols])).astype(o_ref.dtype)

    @pl.when(c == pl.num_programs(1) - 1)
    def _():
        s_out_ref[...] = s_ref[...]


def _gdn_prompt(qkv, z, bg, conv_w, gdn_norm):
    b, l, _ = qkv.shape
    nc = l // CHUNK
    bg4 = bg.reshape(b, nc, CHUNK, 2 * GDN_HEADS)
    bgt4 = jnp.swapaxes(bg4, 2, 3)
    halo_blocks = CHUNK // SUBLANES
    grid = (b, nc // PREP_CHUNKS)
    seq = lambda w: pl.BlockSpec((None, PREP_CHUNKS * CHUNK, w), lambda i, c: (i, c, 0))
    chunks = lambda *s: pl.BlockSpec((None, PREP_CHUNKS) + s, lambda i, c: (i, c) + (0,) * len(s))
    inter_shapes = [((GDN_HEADS, 2 * CHUNK, GDN_DK), BF16), ((N_GROUPS, GDN_DK, GROUP_ROWS), BF16),
                    ((N_GROUPS, GROUP_ROWS, GROUP_ROWS), BF16), ((N_GROUPS, GROUP_ROWS, GDN_DV), F32),
                    ((1, 2 * GDN_HEADS), F32)]
    inter_specs = [chunks(*s) for s, _ in inter_shapes]
    wqg, kdt, qk, u, gl = pl.pallas_call(
        _gdn_prep_kernel,
        grid=grid,
        in_specs=[seq(QKV_CH),
                  pl.BlockSpec((None, SUBLANES, QKV_CH),
                               lambda i, c: (i, jnp.maximum(c * PREP_CHUNKS * halo_blocks - 1, 0), 0)),
                  chunks(CHUNK, 2 * GDN_HEADS), chunks(2 * GDN_HEADS, CHUNK), _const_spec(conv_w.shape)],
        out_specs=inter_specs,
        out_shape=[jax.ShapeDtypeStruct((b, nc) + s, d) for s, d in inter_shapes],
        scratch_shapes=[pltpu.VMEM((SUBLANES + PREP_CHUNKS * CHUNK, QKV_CH), F32)],
        compiler_params=_params(("parallel", "parallel")),
        name="gdn_prep",
    )(qkv, qkv, bg4, bgt4, conv_w)
    return pl.pallas_call(
        _gdn_scan_kernel,
        grid=grid,
        in_specs=inter_specs + [seq(V_DIM), _const_spec((1, GDN_DV))],
        out_specs=[seq(V_DIM),
                   pl.BlockSpec((None, GDN_HEADS, GDN_DK, GDN_DV), lambda i, c: (i, 0, 0, 0))],
        out_shape=[jax.ShapeDtypeStruct((b, l, V_DIM), BF16),
                   jax.ShapeDtypeStruct((b, GDN_HEADS, GDN_DK, GDN_DV), F32)],
        scratch_shapes=[pltpu.VMEM((GDN_HEADS, GDN_DK, GDN_DV), F32)],
        compiler_params=_params(("parallel", "arbitrary")),
        name="gdn_scan",
    )(wqg, kdt, qk, u, gl, z, gdn_norm)


GDN_SAMPLE_ROWS = 8


def _gdn_sample_kernel(raw_ref, cs_ref, z_ref, bg_ref, s_in_ref, cw_ref, gn_ref, o_ref, s_out_ref):
    conv = cw_ref[SHORT_CONV - 1:SHORT_CONV, :] * raw_ref[...]
    for j in range(SHORT_CONV - 1):
        conv = conv + cw_ref[j:j + 1, :] * cs_ref[j]
    qkv = _silu(conv)
    bg = bg_ref[...]
    per_head = []
    for h in range(GDN_HEADS):
        q = qkv[:, h * GDN_DK:(h + 1) * GDN_DK]
        k = qkv[:, QK_DIM + h * GDN_DK:QK_DIM + (h + 1) * GDN_DK]
        v = qkv[:, 2 * QK_DIM + h * GDN_DV:2 * QK_DIM + (h + 1) * GDN_DV]
        q = q * lax.rsqrt(jnp.sum(q * q, axis=-1, keepdims=True) + NORM_EPS) * (GDN_DK ** -0.5)
        k = k * lax.rsqrt(jnp.sum(k * k, axis=-1, keepdims=True) + NORM_EPS)
        beta = bg[:, h:h + 1]
        eg = jnp.exp(bg[:, GDN_HEADS + h:GDN_HEADS + h + 1])
        per_head.append(dict(u=v * beta, qk=jnp.sum(q * k, axis=-1, keepdims=True), eg=eg,
                             w_t=(k * (beta * eg)).T, qg_t=(q * eg).T, k_t=k.T))
    rows = range(GDN_SAMPLE_ROWS)
    w_s = [[jnp.sum(p["w_t"][:, r:r + 1] * s_in_ref[r, h], axis=0, keepdims=True) for r in rows]
           for h, p in enumerate(per_head)]
    q_s = [[jnp.sum(p["qg_t"][:, r:r + 1] * s_in_ref[r, h], axis=0, keepdims=True) for r in rows]
           for h, p in enumerate(per_head)]
    v_new = [[p["u"][r:r + 1] - w_s[h][r] for r in rows] for h, p in enumerate(per_head)]
    for h, p in enumerate(per_head):
        for r in rows:
            s_out_ref[r, h] = s_in_ref[r, h] * p["eg"][r:r + 1] + p["k_t"][:, r:r + 1] * v_new[h][r]
    for h, p in enumerate(per_head):
        o = jnp.concatenate([q_s[h][r] + p["qk"][r:r + 1] * v_new[h][r] for r in rows], axis=0)
        cols = slice(h * GDN_DV, (h + 1) * GDN_DV)
        o_ref[:, cols] = (_rms(o, gn_ref[...]) * _silu(z_ref[:, cols])).astype(o_ref.dtype)


def _gdn_sample(raw, conv_state_t, z, bg, state, conv_w, gdn_norm):
    n = raw.shape[0]
    rb = GDN_SAMPLE_ROWS
    row = lambda w: pl.BlockSpec((rb, w), lambda i: (i, 0))
    st = pl.BlockSpec((rb, GDN_HEADS, GDN_DK, GDN_DV), lambda i: (i, 0, 0, 0))
    return pl.pallas_call(
        _gdn_sample_kernel,
        grid=(n // rb,),
        in_specs=[row(QKV_CH), pl.BlockSpec((SHORT_CONV - 1, rb, QKV_CH), lambda i: (0, i, 0)),
                  row(V_DIM), row(2 * GDN_HEADS), st, _const_spec(conv_w.shape), _const_spec((1, GDN_DV))],
        out_specs=[row(V_DIM), st],
        out_shape=[jax.ShapeDtypeStruct((n, V_DIM), BF16), jax.ShapeDtypeStruct(state.shape, F32)],
        compiler_params=_params(("parallel",)),
        name="gdn_sample",
    )(raw, conv_state_t, z, bg, state, conv_w, gdn_norm)


CONF_TILE = 256
CONF_ROWS = 64


def _ln_silu(x, g, b):
    mu = jnp.mean(x, axis=-1, keepdims=True)
    xc = x - mu
    var = jnp.mean(xc * xc, axis=-1, keepdims=True)
    return _silu(xc * lax.rsqrt(var + LN_EPS) * g + b)


def _conf_prompt_kernel(u_ref, w_ref, b_ref, lng_ref, lnb_ref, cv_ref, ubuf_ref, sh_ref, acc_ref):
    @pl.when(pl.program_id(1) == 0)
    def _():
        ubuf_ref[0:CONF_HALO, :] = jnp.zeros((CONF_HALO, CONF_CH), F32)

    ubuf_ref[CONF_HALO:CONF_HALO + CONF_TILE, :] = u_ref[...]
    span = CONF_HALO + CONF_TILE - SUBLANES
    for k in range(1, SUBLANES):
        sh_ref[k - 1, 0:span, :] = ubuf_ref[k:k + span, :]
    base = CONF_HALO - (CONF_K - 1)
    for rb in range(CONF_TILE // CONF_ROWS):
        for cb in range(CONF_CH // LANES):
            cols = slice(cb * LANES, (cb + 1) * LANES)
            acc = b_ref[:, cols]
            for j in range(CONF_K):
                k = (base + j) % SUBLANES
                r0 = base + j - k + rb * CONF_ROWS
                rows = slice(r0, r0 + CONF_ROWS)
                src = ubuf_ref[rows, cols] if k == 0 else sh_ref[k - 1, rows, cols]
                acc = acc + w_ref[j:j + 1, cols] * src
            acc_ref[rb * CONF_ROWS:(rb + 1) * CONF_ROWS, cols] = acc
    ubuf_ref[0:CONF_HALO, :] = ubuf_ref[CONF_TILE:CONF_TILE + CONF_HALO, :]
    cv_ref[...] = _ln_silu(acc_ref[...], lng_ref[...], lnb_ref[...]).astype(cv_ref.dtype)


def _conf_prompt(u, w, b, ln_g, ln_b):
    bsz, l, _ = u.shape
    seq = pl.BlockSpec((None, CONF_TILE, CONF_CH), lambda i, t: (i, t, 0))
    return pl.pallas_call(
        _conf_prompt_kernel,
        grid=(bsz, l // CONF_TILE),
        in_specs=[seq, _const_spec(w.shape), _const_spec((1, CONF_CH)), _const_spec((1, CONF_CH)),
                  _const_spec((1, CONF_CH))],
        out_specs=seq,
        out_shape=jax.ShapeDtypeStruct((bsz, l, CONF_CH), BF16),
        scratch_shapes=[pltpu.VMEM((CONF_HALO + CONF_TILE, CONF_CH), F32),
                        pltpu.VMEM((SUBLANES - 1, CONF_HALO + CONF_TILE - SUBLANES, CONF_CH), F32),
                        pltpu.VMEM((CONF_TILE, CONF_CH), F32)],
        compiler_params=_params(("parallel", "arbitrary")),
        name="conf_prompt",
    )(u, w, b, ln_g, ln_b)


def _conf_sample_kernel(u_ref, cs_ref, w_ref, b_ref, lng_ref, lnb_ref, cv_ref):
    acc = w_ref[CONF_K - 1:CONF_K, :] * u_ref[...] + b_ref[...]
    for j in range(CONF_K - 1):
        acc = acc + w_ref[j:j + 1, :] * cs_ref[j]
    cv_ref[...] = _ln_silu(acc, lng_ref[...], lnb_ref[...]).astype(cv_ref.dtype)


def _conf_sample(u, conf_state_t, w, b, ln_g, ln_b):
    n = u.shape[0]
    return pl.pallas_call(
        _conf_sample_kernel,
        out_shape=jax.ShapeDtypeStruct((n, CONF_CH), BF16),
        compiler_params=pltpu.CompilerParams(vmem_limit_bytes=VMEM_LIMIT_BYTES),
        name="conf_sample",
    )(u, conf_state_t, w, b, ln_g, ln_b)


MOE_TILE = 256
SLAB = D_MODEL // LANES


def _mix_kernel(x_ref, o_ref, cv_ref, ga_ref, gb_ref, wgo_ref, wco_ref, wout_ref, nffn_ref, wr_ref, br_ref,
                x1_ref, hm3_ref, topi_ref, topw_ref):
    tm = x_ref.shape[0]
    branch_a = _dot(o_ref[...], wgo_ref[...])
    branch_b = _dot(cv_ref[...], wco_ref[...])
    merged = ga_ref[...] * branch_a + gb_ref[...] * branch_b
    x1 = x_ref[...] + _bdot(merged, wout_ref[...])
    x1_ref[...] = x1
    hm = _rms(x1, nffn_ref[...])
    for s in range(SLAB):
        hm3_ref[pl.ds(s, tm, stride=SLAB), :] = hm[:, s * LANES:(s + 1) * LANES]
    hm_hi = hm.astype(BF16)
    hm_lo = (hm - hm_hi.astype(F32)).astype(BF16)
    by_hi = _dot(hm_hi, wr_ref[...])
    logits = (by_hi[:, :N_EXPERTS] + (by_hi[:, N_EXPERTS:] + _dot(hm_lo, wr_ref[:, :N_EXPERTS]))) + br_ref[...]
    lane = lax.broadcasted_iota(jnp.int32, logits.shape, 1)
    work = logits
    top_v = []
    top_i = jnp.zeros(logits.shape, jnp.int32)
    for k in range(TOP_K):
        m = jnp.max(work, axis=-1, keepdims=True)
        first = jnp.min(jnp.where(work == m, lane, N_EXPERTS), axis=-1, keepdims=True)
        top_v.append(m)
        top_i = jnp.where(lane == k, first, top_i)
        work = jnp.where(lane == first, -jnp.inf, work)
    ex = [jnp.exp(v - top_v[0]) for v in top_v]
    den = ex[0]
    for e in ex[1:]:
        den = den + e
    top_w = jnp.zeros(logits.shape, F32)
    for k, e in enumerate(ex):
        top_w = jnp.where(lane == k, e / den, top_w)
    topi_ref[...] = top_i
    topw_ref[...] = top_w


def _mix(x, o, cv, ga, gb, wgo, wco, wout, norm_ffn, w_router, b_router, tm):
    n = x.shape[0]
    row = lambda w: pl.BlockSpec((tm, w), lambda i: (i, 0))
    return pl.pallas_call(
        _mix_kernel,
        grid=(n // tm,),
        in_specs=[row(D_MODEL), row(V_DIM), row(CONF_CH), row(D_MODEL), row(D_MODEL),
                  _const_spec(wgo.shape), _const_spec(wco.shape), _const_spec(wout.shape),
                  _const_spec((1, D_MODEL)), _const_spec(w_router.shape), _const_spec((1, N_EXPERTS))],
        out_specs=[row(D_MODEL), pl.BlockSpec((tm * SLAB, LANES), lambda i: (i, 0)), row(N_EXPERTS), row(N_EXPERTS)],
        out_shape=[jax.ShapeDtypeStruct((n, D_MODEL), F32), jax.ShapeDtypeStruct((n * SLAB, LANES), F32),
                   jax.ShapeDtypeStruct((n, N_EXPERTS), jnp.int32), jax.ShapeDtypeStruct((n, N_EXPERTS), F32)],
        compiler_params=_params(("parallel",)),
        name="mix",
    )(x, o, cv, ga, gb, wgo, wco, wout, norm_ffn, w_router, b_router)


ROUTE_BLOCK_MAX = 16384


def _equal_blocks(n):
    for d in range(1, n // LANES + 1):
        if n % d == 0 and n // d <= ROUTE_BLOCK_MAX and (n // d) % LANES == 0:
            return d
    raise ValueError(f"cannot split {n} elements into lane-aligned blocks")


def _route_rows_kernel(lo_ref, hi_ref, pos_ref, rp_ref):
    s = pl.program_id(0)
    pairs_per = pos_ref.shape[-1]

    @pl.when(s == 0)
    def _():
        def fill(r, carry):
            rp_ref[r] = -1
            return carry
        for i in range(lo_ref.shape[0]):
            lax.fori_loop(lo_ref[i], hi_ref[i], fill, 0)

    @pl.when(s > 0)
    def _():
        base = (s - 1) * pairs_per

        def place(i, carry):
            for j in range(SUBLANES):
                p = i * SUBLANES + j
                rp_ref[pos_ref[0, p]] = base + p
            return carry
        lax.fori_loop(0, pairs_per // SUBLANES, place, 0)


def _route(top_i):
    n = top_i.shape[0]
    pairs = n * TOP_K
    n_tiles = pairs // MOE_TILE + N_EXPERTS
    e_flat = top_i.reshape(pairs)
    onehot = (e_flat[:, None] == jnp.arange(N_EXPERTS, dtype=jnp.int32)[None, :]).astype(jnp.int32)
    csum = jnp.cumsum(onehot, axis=0)
    rank = jnp.take_along_axis(csum, e_flat[:, None], axis=1)[:, 0] - 1
    tiles_per = (csum[-1] + MOE_TILE - 1) // MOE_TILE
    tile_end = jnp.cumsum(tiles_per)
    n_used = tile_end[-1]
    pos = (tile_end - tiles_per)[e_flat] * MOE_TILE + rank
    n_rows = n_tiles * MOE_TILE
    n_place = _equal_blocks(pairs)
    pad_lo = jnp.concatenate([(tile_end - tiles_per) * MOE_TILE + csum[-1], n_used.reshape(1) * MOE_TILE])
    pad_hi = jnp.concatenate([tile_end * MOE_TILE, jnp.full((1,), n_rows, jnp.int32)])
    row_pair = pl.pallas_call(
        _route_rows_kernel,
        grid_spec=pltpu.PrefetchScalarGridSpec(
            num_scalar_prefetch=2,
            grid=(1 + n_place,),
            in_specs=[pl.BlockSpec((None, 1, pairs // n_place), lambda s, lo, hi: (jnp.maximum(s - 1, 0), 0, 0),
                                   memory_space=pltpu.SMEM)],
            out_specs=pl.BlockSpec(memory_space=pltpu.SMEM)),
        out_shape=jax.ShapeDtypeStruct((n_rows,), jnp.int32),
        compiler_params=_params(("arbitrary",)),
        name="route_rows",
    )(pad_lo, pad_hi, pos.reshape(n_place, 1, pairs // n_place))
    row_pair = jnp.concatenate([jnp.full((MOE_TILE,), -1, jnp.int32), row_pair]).reshape(n_tiles + 1, 1, MOE_TILE)
    t = jnp.minimum(jnp.arange(n_tiles + 1, dtype=jnp.int32), n_used - 1)
    tile_expert = jnp.sum((tile_end[None, :] <= t[:, None]).astype(jnp.int32), axis=1)
    return tile_expert, n_used.reshape(1), row_pair


def _experts_kernel(te_ref, nu_ref, idx0_ref, idxn_ref, idxp_ref, x3_hbm, wgu_ref, bgu_ref, wd_ref, bd_ref,
                    y_hbm, x0, x1, y0, y1, wgu16, wd16, sem_in, sem_out):
    t = pl.program_id(0)
    n_used = nu_ref[0]
    xs, ys = (x0, x1), (y0, y1)
    tile_rows = MOE_TILE * SLAB
    pair_rows = y_hbm.shape[0] - 2 * tile_rows

    def start_gather(idx_ref, c):
        for i in range(MOE_TILE):
            tok = jnp.maximum(idx_ref[0, i], 0) // TOP_K
            pltpu.make_async_copy(x3_hbm.at[tok], xs[c].at[pl.ds(i * SLAB, SLAB), :],
                                  sem_in.at[c]).start(priority=i % 2)

    def start_scatter(idx_ref, c):
        for i in range(MOE_TILE):
            pair = idx_ref[0, i]
            row = jnp.where(pair >= 0, pair * SLAB, pair_rows + c * tile_rows + i * SLAB)
            pltpu.make_async_copy(ys[c].at[pl.ds(i * SLAB, SLAB), :],
                                  y_hbm.at[pl.ds(pl.multiple_of(row, SLAB), SLAB), :],
                                  sem_out.at[c]).start(priority=i % 2)

    def wait_tile(buf, sem):
        pltpu.make_async_copy(buf, buf, sem).wait()

    @pl.when(t == 0)
    def _():
        y0[...] = jnp.zeros(y0.shape, F32)
        y1[...] = jnp.zeros(y1.shape, F32)
        pltpu.make_async_copy(y0, y_hbm.at[pl.ds(pair_rows, tile_rows), :], sem_out.at[0]).start()
        start_gather(idx0_ref, 0)

    @pl.when((t < n_used) & ((t == 0) | (te_ref[t] != te_ref[jnp.maximum(t - 1, 0)])))
    def _():
        wgu16[...] = wgu_ref[...].astype(BF16)
        wd16[...] = wd_ref[...].astype(BF16)

    def step(c):
        o = 1 - c
        wait_tile(xs[c], sem_in.at[c])
        start_gather(idxn_ref, o)
        start_scatter(idxp_ref, o)
        x = jnp.concatenate([xs[c][pl.ds(s, MOE_TILE, stride=SLAB), :] for s in range(SLAB)], axis=1)
        gu = _dot(x.astype(BF16), wgu16[...]) + bgu_ref[...]
        gate = jnp.minimum(gu[:, :D_FF], SWIGLU_LIMIT)
        up = jnp.clip(gu[:, D_FF:], -SWIGLU_LIMIT, SWIGLU_LIMIT)
        hid = (up + 1.0) * (gate * _sigmoid(SWIGLU_ALPHA * gate))
        ye = _dot(hid.astype(BF16), wd16[...]) + bd_ref[...]
        wait_tile(ys[c], sem_out.at[c])
        for s in range(SLAB):
            ys[c][pl.ds(s, MOE_TILE, stride=SLAB), :] = ye[:, s * LANES:(s + 1) * LANES]

    def drain(c):
        o = 1 - c
        start_scatter(idxp_ref, o)
        wait_tile(ys[o], sem_out.at[o])
        wait_tile(ys[c], sem_out.at[c])
        wait_tile(xs[c], sem_in.at[c])

    for c in range(2):
        pl.when((t < n_used) & (t % 2 == c))(functools.partial(step, c))
        pl.when((t == n_used) & (t % 2 == c))(functools.partial(drain, c))


def _experts(hm3, tile_expert, n_used, row_pair, wgu, bgu, wd, bd):
    n_steps = row_pair.shape[0]
    n_tiles = n_steps - 1
    pairs = hm3.shape[0] // SLAB * TOP_K
    x3 = hm3.reshape(-1, SLAB, LANES)
    idx = lambda f: pl.BlockSpec((None, 1, MOE_TILE), f, memory_space=pltpu.SMEM)
    tile_buf = pltpu.VMEM((MOE_TILE * SLAB, LANES), F32)
    grid_spec = pltpu.PrefetchScalarGridSpec(
        num_scalar_prefetch=2,
        grid=(n_steps,),
        in_specs=[idx(lambda t, te, nu: (1, 0, 0)),
                  idx(lambda t, te, nu: (jnp.minimum(t + 2, n_tiles), 0, 0)),
                  idx(lambda t, te, nu: (t, 0, 0)),
                  pl.BlockSpec(memory_space=pl.ANY),
                  pl.BlockSpec((None, D_MODEL, 2 * D_FF), lambda t, te, nu: (te[t], 0, 0)),
                  pl.BlockSpec((None, 1, 2 * D_FF), lambda t, te, nu: (te[t], 0, 0)),
                  pl.BlockSpec((None, D_FF, D_MODEL), lambda t, te, nu: (te[t], 0, 0)),
                  pl.BlockSpec((None, 1, D_MODEL), lambda t, te, nu: (te[t], 0, 0))],
        out_specs=pl.BlockSpec(memory_space=pl.ANY),
        scratch_shapes=[tile_buf, tile_buf, tile_buf, tile_buf,
                        pltpu.VMEM((D_MODEL, 2 * D_FF), BF16), pltpu.VMEM((D_FF, D_MODEL), BF16),
                        pltpu.SemaphoreType.DMA((2,)), pltpu.SemaphoreType.DMA((2,))])
    return pl.pallas_call(
        _experts_kernel,
        grid_spec=grid_spec,
        out_shape=jax.ShapeDtypeStruct(((pairs + 2 * MOE_TILE) * SLAB, LANES), F32),
        compiler_params=_params(("arbitrary",)),
        name="experts",
    )(tile_expert, n_used, row_pair, row_pair, row_pair, x3, wgu, bgu, wd, bd)


def _ple_kernel(x1_ref, y_ref, topw_ref, p_ref, nple_ref, wg_ref, wp_ref, nfin_ref, out_ref, wrep_ref, acc_ref):
    tm = x1_ref.shape[0]
    topw = topw_ref[...]
    for k in range(TOP_K):
        wrep_ref[k] = jnp.broadcast_to(topw[:, k:k + 1], (tm, LANES))

    def combine(g, carry):
        for i in range(SUBLANES):
            t = g * SUBLANES + i
            acc = None
            for k in range(TOP_K):
                rows = pl.ds(pl.multiple_of((t * TOP_K + k) * SLAB, SLAB), SLAB)
                term = wrep_ref[k, pl.ds(t, 1), :] * y_ref[rows, :]
                acc = term if acc is None else acc + term
            acc_ref[pl.ds(pl.multiple_of(t * SLAB, SLAB), SLAB), :] = acc
        return carry
    lax.fori_loop(0, tm // SUBLANES, combine, 0)
    moe = [acc_ref[pl.ds(s, tm, stride=SLAB), :] for s in range(SLAB)]
    x2 = x1_ref[...] + jnp.concatenate(moe, axis=1)
    gate = _sigmoid(_bdot(_rms(x2, nple_ref[...]), wg_ref[...]))
    x3 = x2 + _bdot(p_ref[...], wp_ref[...]) * gate
    out_ref[...] = _rms(x3, nfin_ref[...])


def _ple(x1, y2d, topw, p, norm_ple, wg, wp, norm_final, tm, row_offset):
    n = x1.shape[0]
    off = row_offset // tm
    row = lambda w: pl.BlockSpec((tm, w), lambda i: (i, 0))
    return pl.pallas_call(
        _ple_kernel,
        grid=(n // tm,),
        in_specs=[row(D_MODEL), pl.BlockSpec((tm * TOP_K * SLAB, LANES), lambda i: (i + off, 0)), row(N_EXPERTS),
                  row(PLE_DIM), _const_spec((1, D_MODEL)), _const_spec(wg.shape), _const_spec(wp.shape),
                  _const_spec((1, D_MODEL))],
        out_specs=row(D_MODEL),
        out_shape=jax.ShapeDtypeStruct((n, D_MODEL), F32),
        scratch_shapes=[pltpu.VMEM((TOP_K, tm, LANES), F32), pltpu.VMEM((tm * SLAB, LANES), F32)],
        compiler_params=_params(("parallel",)),
        name="ple",
    )(x1, y2d, topw, p, norm_ple, wg, wp, norm_final)


def kernel(x_prompt, x_sample, p_prompt, p_sample, state_gdn, state_qkv_conv, state_conf_conv, norm_mix, w_in, conv_qkv_w, a_log, dt_bias, gdn_norm, w_gdn_out, conf_dw_w, conf_dw_b, conf_ln_g, conf_ln_b, w_conf_out, w_out, norm_ffn, w_router, b_router, w_gate_up, b_gate_up, w_down, b_down, norm_ple, w_ple_gate, w_ple_proj, norm_final):
    assert w_in.shape[0] == 1, "single-layer trunk"
    bsz, seq, _ = x_prompt.shape
    n_p = bsz * seq
    n_s = x_sample.shape[0] * x_sample.shape[1]
    assert x_sample.shape[1] == 1

    w_in0 = w_in[0]
    wqkv = w_in0[:, :OFF_Z].astype(BF16)
    wz = w_in0[:, OFF_Z:OFF_BETA].astype(BF16)
    wba = w_in0[:, OFF_BETA:OFF_GLU].astype(BF16)
    wglu = w_in0[:, OFF_GLU:OFF_GATE].astype(BF16)
    wgate = w_in0[:, OFF_GATE:].astype(BF16)
    zeros_h = jnp.zeros((GDN_HEADS,), F32)
    alog16 = jnp.concatenate([zeros_h, a_log[0]]).reshape(1, 2 * GDN_HEADS)
    dtb16 = jnp.concatenate([zeros_h, dt_bias[0]]).reshape(1, 2 * GDN_HEADS)
    wgo = w_gdn_out[0].astype(BF16)
    wco = w_conf_out[0].astype(BF16)
    wout = w_out[0].astype(BF16)
    wpg = w_ple_gate[0].astype(BF16)
    wpp = w_ple_proj[0].astype(BF16)
    bgu = b_gate_up[0].reshape(N_EXPERTS, 1, 2 * D_FF)
    bd = b_down[0].reshape(N_EXPERTS, 1, D_MODEL)
    row = lambda v: v.reshape(1, -1)
    inproj_w = (row(norm_mix[0]), wqkv, wz, wba, wglu, wgate, alog16, dtb16)
    wr_hi = w_router[0].astype(BF16)
    wr_lo = (w_router[0] - wr_hi.astype(F32)).astype(BF16)
    mix_w = (wgo, wco, wout, row(norm_ffn[0]), jnp.concatenate([wr_hi, wr_lo], axis=1), row(b_router[0]))
    ple_w = (row(norm_ple[0]), wpg, wpp, row(norm_final))
    conf_w = (conf_dw_w[0], row(conf_dw_b[0]), row(conf_ln_g[0]), row(conf_ln_b[0]))

    xp = x_prompt.reshape(n_p, D_MODEL)
    qkv_p, z_p, bg_p, u_p, ga_p, gb_p = _inproj(xp, *inproj_w, tm=256)
    qkv_p3 = qkv_p.reshape(bsz, seq, QKV_CH)
    u_p3 = u_p.reshape(bsz, seq, CONF_CH)
    o_p, s_p = _gdn_prompt(qkv_p3, z_p.reshape(bsz, seq, V_DIM), bg_p, conv_qkv_w[0], row(gdn_norm[0]))
    cv_p = _conf_prompt(u_p3, *conf_w)

    xs = x_sample.reshape(n_s, D_MODEL)
    qkv_s, z_s, bg_s, u_s, ga_s, gb_s = _inproj(xs, *inproj_w, tm=n_s)
    o_s, s_s = _gdn_sample(qkv_s, jnp.swapaxes(state_qkv_conv[0], 0, 1), z_s, bg_s, state_gdn[0],
                           conv_qkv_w[0], row(gdn_norm[0]))
    cv_s = _conf_sample(u_s, jnp.swapaxes(state_conf_conv[0], 0, 1), *conf_w)

    tm_p = 512
    x1_p, hm3_p, ti_p, tw_p = _mix(xp, o_p.reshape(n_p, V_DIM), cv_p.reshape(n_p, CONF_CH), ga_p, gb_p, *mix_w,
                                   tm=tm_p)
    x1_s, hm3_s, ti_s, tw_s = _mix(xs, o_s, cv_s, ga_s, gb_s, *mix_w, tm=n_s)
    routing = _route(jnp.concatenate([ti_p, ti_s], axis=0)[:, :TOP_K])
    y2d = _experts(jnp.concatenate([hm3_p, hm3_s], axis=0), *routing, w_gate_up[0], bgu, w_down[0], bd)
    y_p = _ple(x1_p, y2d, tw_p, p_prompt[0].reshape(n_p, PLE_DIM), *ple_w, tm=tm_p, row_offset=0)
    y_s = _ple(x1_s, y2d, tw_s, p_sample[0].reshape(n_s, PLE_DIM), *ple_w, tm=n_s, row_offset=n_p)

    new_qkv_s = jnp.concatenate([state_qkv_conv[0][:, 1:], qkv_s[:, None, :]], axis=1)
    new_conf_s = jnp.concatenate([state_conf_conv[0][:, 1:], u_s[:, None, :]], axis=1)
    return (y_p.reshape(bsz, seq, D_MODEL), y_s.reshape(n_s, 1, D_MODEL),
            s_p[None], qkv_p3[:, seq - (SHORT_CONV - 1):][None], u_p3[:, seq - (CONF_K - 1):][None],
            s_s[None], new_qkv_s[None], new_conf_s[None])
```

```python
import functools

import jax
import jax.numpy as jnp
from jax import lax
from jax.experimental import pallas as pl
from jax.experimental.pallas import tpu as pltpu

F32 = jnp.float32
BF16 = jnp.bfloat16

D_MODEL = 1024
GDN_HEADS = 8
GDN_DK = 128
GDN_DV = 128
QK_DIM = GDN_HEADS * GDN_DK
V_DIM = GDN_HEADS * GDN_DV
QKV_CH = 2 * QK_DIM + V_DIM
SHORT_CONV = 4
CHUNK = 64
CONF_CH = D_MODEL // 2
CONF_K = 31
N_EXPERTS = 32
TOP_K = 4
D_FF = D_MODEL
SWIGLU_LIMIT = 7.0
SWIGLU_ALPHA = 1.702
PLE_DIM = 256
NORM_EPS = 1e-6
LN_EPS = 1e-5

OFF_Z = QKV_CH
OFF_BETA = OFF_Z + V_DIM
OFF_GLU = OFF_BETA + 2 * GDN_HEADS
OFF_GATE = OFF_GLU + 2 * CONF_CH

VMEM_LIMIT_BYTES = 52 * 1024 * 1024
SUBLANES = 8
LANES = 128
CONF_HALO = 32


def _dot(a, b):
    return jnp.dot(a, b, preferred_element_type=F32)


def _bdot(a, b):
    return jnp.dot(a.astype(BF16), b.astype(BF16), preferred_element_type=F32)


def _split3(x):
    hi = x.astype(BF16)
    r1 = x - hi.astype(F32)
    mid = r1.astype(BF16)
    lo = (r1 - mid.astype(F32)).astype(BF16)
    return hi, mid, lo


def _mask_dot(mask, x, mask_first):
    m = jnp.where(mask, 1.0, 0.0).astype(BF16)
    parts = [_dot(m, p) if mask_first else _dot(p, m) for p in _split3(x)]
    return parts[0] + (parts[1] + parts[2])


def _sigmoid(x):
    return 1.0 / (1.0 + jnp.exp(-x))


def _silu(x):
    return x * _sigmoid(x)


def _rms(x, g):
    return x * lax.rsqrt(jnp.mean(x * x, axis=-1, keepdims=True) + NORM_EPS) * g


def _const_spec(shape):
    nd = len(shape)
    return pl.BlockSpec(shape, lambda *_: (0,) * nd, pipeline_mode=pl.Buffered(1))


def _params(sem):
    return pltpu.CompilerParams(dimension_semantics=sem, vmem_limit_bytes=VMEM_LIMIT_BYTES)


def _inproj_kernel(x_ref, nw_ref, wqkv_ref, wz_ref, wba_ref, wglu_ref, wgate_ref, alog_ref, dtb_ref,
                   qkv_ref, z_ref, bg_ref, u_ref, ga_ref, gb_ref):
    a = _rms(x_ref[...], nw_ref[...]).astype(BF16)
    for c in range(QKV_CH // D_MODEL):
        cols = slice(c * D_MODEL, (c + 1) * D_MODEL)
        qkv_ref[:, cols] = _dot(a, wqkv_ref[:, cols])
    z_ref[...] = _dot(a, wz_ref[...])
    ba = _dot(a, wba_ref[...])
    lane = lax.broadcasted_iota(jnp.int32, ba.shape, 1)
    t = ba + dtb_ref[...]
    softplus = jnp.maximum(t, 0.0) + jnp.log(1.0 + jnp.exp(-jnp.abs(t)))
    bg_ref[...] = jnp.where(lane < GDN_HEADS, _sigmoid(ba), -jnp.exp(alog_ref[...]) * softplus)
    glu = _dot(a, wglu_ref[...])
    u_ref[...] = glu[:, :CONF_CH] * _sigmoid(glu[:, CONF_CH:])
    ga_ref[...] = _sigmoid(_dot(a, wgate_ref[:, :D_MODEL]))
    gb_ref[...] = _sigmoid(_dot(a, wgate_ref[:, D_MODEL:]))


def _inproj(x, norm_w, wqkv, wz, wba, wglu, wgate, alog16, dtb16, tm):
    n = x.shape[0]
    row = lambda w: pl.BlockSpec((tm, w), lambda i: (i, 0))
    return pl.pallas_call(
        _inproj_kernel,
        grid=(n // tm,),
        in_specs=[row(D_MODEL), _const_spec((1, D_MODEL)), _const_spec(wqkv.shape), _const_spec(wz.shape),
                  _const_spec(wba.shape), _const_spec(wglu.shape), _const_spec(wgate.shape),
                  _const_spec((1, 2 * GDN_HEADS)), _const_spec((1, 2 * GDN_HEADS))],
        out_specs=[row(QKV_CH), row(V_DIM), row(2 * GDN_HEADS), row(CONF_CH), row(D_MODEL), row(D_MODEL)],
        out_shape=[jax.ShapeDtypeStruct((n, w), F32)
                   for w in (QKV_CH, V_DIM, 2 * GDN_HEADS, CONF_CH, D_MODEL, D_MODEL)],
        compiler_params=_params(("parallel",)),
        name="inproj",
    )(x, norm_w, wqkv, wz, wba, wglu, wgate, alog16, dtb16)


GDN_GROUP = 4
PREP_CHUNKS = 4
GROUP_ROWS = GDN_GROUP * CHUNK
N_GROUPS = GDN_HEADS // GDN_GROUP


def _block_unit_lower_inverses(a_list, eye):
    ts = [eye - a for a in a_list]
    ps = [_bdot(a, a) for a in a_list]
    covered = 2
    while covered < CHUNK:
        covered *= 2
        if covered < CHUNK:
            rs = [_bdot(jnp.concatenate([t, p], axis=0), p) for t, p in zip(ts, ps)]
            ts = [t + r[:GROUP_ROWS] for t, r in zip(ts, rs)]
            ps = [r[GROUP_ROWS:] for r in rs]
        else:
            ts = [t + _bdot(t, p) for t, p in zip(ts, ps)]
    return ts


def _gdn_prep_kernel(qkv_ref, halo_ref, bg_ref, bgt_ref, cw_ref, wqg_ref, kdt_ref, qk_ref, u_ref, gl_ref, ext_ref):
    c = pl.program_id(1)
    ext_ref[0:SUBLANES, :] = jnp.where(c > 0, halo_ref[...], 0.0)
    ext_ref[SUBLANES:SUBLANES + PREP_CHUNKS * CHUNK, :] = qkv_ref[...]

    row = lax.broadcasted_iota(jnp.int32, (GROUP_ROWS, GROUP_ROWS), 0)
    col = lax.broadcasted_iota(jnp.int32, (GROUP_ROWS, GROUP_ROWS), 1)
    blk = row - col + (col & (CHUNK - 1))
    same = (blk >= 0) & (blk < CHUNK)
    tril = same & (row >= col)
    strict = same & (row > col)
    eye = (row == col).astype(F32)
    r64 = lax.broadcasted_iota(jnp.int32, (CHUNK, CHUNK), 0)
    c64 = lax.broadcasted_iota(jnp.int32, (CHUNK, CHUNK), 1)
    nt = (((1,), (1,)), ((), ()))

    def convsilu(n, c0):
        base = SUBLANES + n * CHUNK
        cols = slice(c0, c0 + LANES)
        acc = cw_ref[SHORT_CONV - 1:SHORT_CONV, cols] * ext_ref[base:base + CHUNK, cols]
        for j in range(SHORT_CONV - 1):
            r0 = base - (SHORT_CONV - 1) + j
            acc = acc + cw_ref[j:j + 1, cols] * ext_ref[r0:r0 + CHUNK, cols]
        return _silu(acc)

    parts = []
    for n in range(PREP_CHUNKS):
        bg = bg_ref[n]
        gc_all = _mask_dot(r64 >= c64, bg, mask_first=True)
        gct_all = _mask_dot(r64 <= c64, bgt_ref[n], mask_first=False)
        gl_ref[n] = gc_all[CHUNK - 1:CHUNK, :]
        for g in range(N_GROUPS):
            heads = range(g * GDN_GROUP, (g + 1) * GDN_GROUP)
            stack = lambda f: jnp.concatenate([f(h) for h in heads], axis=0)
            q = stack(lambda h: convsilu(n, h * GDN_DK))
            k = stack(lambda h: convsilu(n, QK_DIM + h * GDN_DK))
            v = stack(lambda h: convsilu(n, 2 * QK_DIM + h * GDN_DV))
            q = q * lax.rsqrt(jnp.sum(q * q, axis=-1, keepdims=True) + NORM_EPS) * (GDN_DK ** -0.5)
            k = k * lax.rsqrt(jnp.sum(k * k, axis=-1, keepdims=True) + NORM_EPS)
            beta = stack(lambda h: bg[:, h:h + 1])
            gc = stack(lambda h: gc_all[:, GDN_HEADS + h:GDN_HEADS + h + 1])
            g_last = stack(lambda h: jnp.broadcast_to(
                gc_all[CHUNK - 1:CHUNK, GDN_HEADS + h:GDN_HEADS + h + 1], (CHUNK, 1)))
            gct = jnp.concatenate([gct_all[GDN_HEADS + h:GDN_HEADS + h + 1, :] for h in heads], axis=1)
            decay = jnp.exp(jnp.where(tril, gc - gct, -jnp.inf))
            kb = k * beta
            kq_k = lax.dot_general(jnp.concatenate([kb.astype(BF16), q.astype(BF16)], axis=0), k.astype(BF16), nt,
                                   preferred_element_type=F32)
            a = jnp.where(strict, kq_k[:GROUP_ROWS] * decay, 0.0)
            eg = jnp.exp(gc)
            qk_ref[n, g] = (kq_k[GROUP_ROWS:] * decay).astype(BF16)
            kdt_ref[n, g] = (k * jnp.exp(g_last - gc)).T.astype(BF16)
            qg = (q * eg).astype(BF16)
            for i, h in enumerate(heads):
                wqg_ref[n, h, CHUNK:2 * CHUNK, :] = qg[i * CHUNK:(i + 1) * CHUNK]
            parts.append((n, g, a, jnp.concatenate([v * beta, kb * eg], axis=1)))

    t_invs = _block_unit_lower_inverses([a for _, _, a, _ in parts], eye)
    for (n, g, _, rhs), t_inv in zip(parts, t_invs):
        uw = _bdot(t_inv, rhs)
        u_ref[n, g] = uw[:, :GDN_DV]
        w = uw[:, GDN_DV:].astype(BF16)
        for i in range(GDN_GROUP):
            wqg_ref[n, g * GDN_GROUP + i, 0:CHUNK, :] = w[i * CHUNK:(i + 1) * CHUNK]


def _gdn_scan_kernel(wqg_ref, kdt_ref, qk_ref, u_ref, gl_ref, z_ref, gn_ref, o_ref, s_out_ref, s_ref):
    c = pl.program_id(1)

    @pl.when(c == 0)
    def _():
        s_ref[...] = jnp.zeros(s_ref.shape, F32)

    zeros = jnp.zeros((CHUNK, GDN_DV), BF16)
    for n in range(PREP_CHUNKS):
        dl_all = jnp.exp(gl_ref[n])
        trows = slice(n * CHUNK, (n + 1) * CHUNK)
        ws_qs = [_dot(wqg_ref[n, h], s_ref[h].astype(BF16)) for h in range(GDN_HEADS)]
        v_new = [(u_ref[n, h // GDN_GROUP, (h % GDN_GROUP) * CHUNK:(h % GDN_GROUP + 1) * CHUNK, :]
                  - ws_qs[h][:CHUNK]).astype(BF16) for h in range(GDN_HEADS)]
        o_all = [jnp.concatenate([ws_qs[h][CHUNK:] for h in range(g * GDN_GROUP, (g + 1) * GDN_GROUP)], axis=0)
                 + _dot(qk_ref[n, g], jnp.concatenate(v_new[g * GDN_GROUP:(g + 1) * GDN_GROUP], axis=0))
                 for g in range(N_GROUPS)]
        for h in range(GDN_HEADS):
            g, i = h // GDN_GROUP, h % GDN_GROUP
            vm = jnp.concatenate([v_new[h] if j == i else zeros for j in range(GDN_GROUP)], axis=0)
            s_ref[h] = s_ref[h] * dl_all[:, GDN_HEADS + h:GDN_HEADS + h + 1] + _dot(kdt_ref[n, g], vm)
        for h in range(GDN_HEADS):
            g, i = h // GDN_GROUP, h % GDN_GROUP
            o = _rms(o_all[g][i * CHUNK:(i + 1) * CHUNK], gn_ref[...])
            cols = slice(h * GDN_DV, (h + 1) * GDN_DV)
            o_ref[trows, cols] = (o * _silu(z_ref[trows, cols])).astype(o_ref.dtype)

    @pl.when(c == pl.num_programs(1) - 1)
    def _():
        s_out_ref[...] = s_ref[...]


def _gdn_prompt(qkv, z, bg, conv_w, gdn_norm):
    b, l, _ = qkv.shape
    nc = l // CHUNK
    bg4 = bg.reshape(b, nc, CHUNK, 2 * GDN_HEADS)
    bgt4 = jnp.swapaxes(bg4, 2, 3)
    halo_blocks = CHUNK // SUBLANES
    grid = (b, nc // PREP_CHUNKS)
    seq = lambda w: pl.BlockSpec((None, PREP_CHUNKS * CHUNK, w), lambda i, c: (i, c, 0))
    chunks = lambda *s: pl.BlockSpec((None, PREP_CHUNKS) + s, lambda i, c: (i, c) + (0,) * len(s))
    inter_shapes = [((GDN_HEADS, 2 * CHUNK, GDN_DK), BF16), ((N_GROUPS, GDN_DK, GROUP_ROWS), BF16),
                    ((N_GROUPS, GROUP_ROWS, GROUP_ROWS), BF16), ((N_GROUPS, GROUP_ROWS, GDN_DV), F32),
                    ((1, 2 * GDN_HEADS), F32)]
    inter_specs = [chunks(*s) for s, _ in inter_shapes]
    wqg, kdt, qk, u, gl = pl.pallas_call(
        _gdn_prep_kernel,
        grid=grid,
        in_specs=[seq(QKV_CH),
                  pl.BlockSpec((None, SUBLANES, QKV_CH),
                               lambda i, c: (i, jnp.maximum(c * PREP_CHUNKS * halo_blocks - 1, 0), 0)),
                  chunks(CHUNK, 2 * GDN_HEADS), chunks(2 * GDN_HEADS, CHUNK), _const_spec(conv_w.shape)],
        out_specs=inter_specs,
        out_shape=[jax.ShapeDtypeStruct((b, nc) + s, d) for s, d in inter_shapes],
        scratch_shapes=[pltpu.VMEM((SUBLANES + PREP_CHUNKS * CHUNK, QKV_CH), F32)],
        compiler_params=_params(("parallel", "parallel")),
        name="gdn_prep",
    )(qkv, qkv, bg4, bgt4, conv_w)
    return pl.pallas_call(
        _gdn_scan_kernel,
        grid=grid,
        in_specs=inter_specs + [seq(V_DIM), _const_spec((1, GDN_DV))],
        out_specs=[seq(V_DIM),
                   pl.BlockSpec((None, GDN_HEADS, GDN_DK, GDN_DV), lambda i, c: (i, 0, 0, 0))],
        out_shape=[jax.ShapeDtypeStruct((b, l, V_DIM), BF16),
                   jax.ShapeDtypeStruct((b, GDN_HEADS, GDN_DK, GDN_DV), F32)],
        scratch_shapes=[pltpu.VMEM((GDN_HEADS, GDN_DK, GDN_DV), F32)],
        compiler_params=_params(("parallel", "arbitrary")),
        name="gdn_scan",
    )(wqg, kdt, qk, u, gl, z, gdn_norm)


GDN_SAMPLE_ROWS = 8


def _gdn_sample_kernel(raw_ref, cs_ref, z_ref, bg_ref, s_in_ref, cw_ref, gn_ref, o_ref, s_out_ref):
    conv = cw_ref[SHORT_CONV - 1:SHORT_CONV, :] * raw_ref[...]
    for j in range(SHORT_CONV - 1):
        conv = conv + cw_ref[j:j + 1, :] * cs_ref[j]
    qkv = _silu(conv)
    bg = bg_ref[...]
    per_head = []
    for h in range(GDN_HEADS):
        q = qkv[:, h * GDN_DK:(h + 1) * GDN_DK]
        k = qkv[:, QK_DIM + h * GDN_DK:QK_DIM + (h + 1) * GDN_DK]
        v = qkv[:, 2 * QK_DIM + h * GDN_DV:2 * QK_DIM + (h + 1) * GDN_DV]
        q = q * lax.rsqrt(jnp.sum(q * q, axis=-1, keepdims=True) + NORM_EPS) * (GDN_DK ** -0.5)
        k = k * lax.rsqrt(jnp.sum(k * k, axis=-1, keepdims=True) + NORM_EPS)
        beta = bg[:, h:h + 1]
        eg = jnp.exp(bg[:, GDN_HEADS + h:GDN_HEADS + h + 1])
        per_head.append(dict(u=v * beta, qk=jnp.sum(q * k, axis=-1, keepdims=True), eg=eg,
                             w_t=(k * (beta * eg)).T, qg_t=(q * eg).T, k_t=k.T))
    rows = range(GDN_SAMPLE_ROWS)
    w_s = [[jnp.sum(p["w_t"][:, r:r + 1] * s_in_ref[r, h], axis=0, keepdims=True) for r in rows]
           for h, p in enumerate(per_head)]
    q_s = [[jnp.sum(p["qg_t"][:, r:r + 1] * s_in_ref[r, h], axis=0, keepdims=True) for r in rows]
           for h, p in enumerate(per_head)]
    v_new = [[p["u"][r:r + 1] - w_s[h][r] for r in rows] for h, p in enumerate(per_head)]
    for h, p in enumerate(per_head):
        for r in rows:
            s_out_ref[r, h] = s_in_ref[r, h] * p["eg"][r:r + 1] + p["k_t"][:, r:r + 1] * v_new[h][r]
    for h, p in enumerate(per_head):
        o = jnp.concatenate([q_s[h][r] + p["qk"][r:r + 1] * v_new[h][r] for r in rows], axis=0)
        cols = slice(h * GDN_DV, (h + 1) * GDN_DV)
        o_ref[:, cols] = (_rms(o, gn_ref[...]) * _silu(z_ref[:, cols])).astype(o_ref.dtype)


def _gdn_sample(raw, conv_state_t, z, bg, state, conv_w, gdn_norm):
    n = raw.shape[0]
    rb = GDN_SAMPLE_ROWS
    row = lambda w: pl.BlockSpec((rb, w), lambda i: (i, 0))
    st = pl.BlockSpec((rb, GDN_HEADS, GDN_DK, GDN_DV), lambda i: (i, 0, 0, 0))
    return pl.pallas_call(
        _gdn_sample_kernel,
        grid=(n // rb,),
        in_specs=[row(QKV_CH), pl.BlockSpec((SHORT_CONV - 1, rb, QKV_CH), lambda i: (0, i, 0)),
                  row(V_DIM), row(2 * GDN_HEADS), st, _const_spec(conv_w.shape), _const_spec((1, GDN_DV))],
        out_specs=[row(V_DIM), st],
        out_shape=[jax.ShapeDtypeStruct((n, V_DIM), BF16), jax.ShapeDtypeStruct(state.shape, F32)],
        compiler_params=_params(("parallel",)),
        name="gdn_sample",
    )(raw, conv_state_t, z, bg, state, conv_w, gdn_norm)


CONF_TILE = 256
CONF_ROWS = 64


def _ln_silu(x, g, b):
    mu = jnp.mean(x, axis=-1, keepdims=True)
    xc = x - mu
    var = jnp.mean(xc * xc, axis=-1, keepdims=True)
    return _silu(xc * lax.rsqrt(var + LN_EPS) * g + b)


def _conf_prompt_kernel(u_ref, w_ref, b_ref, lng_ref, lnb_ref, cv_ref, ubuf_ref, sh_ref, acc_ref):
    @pl.when(pl.program_id(1) == 0)
    def _():
        ubuf_ref[0:CONF_HALO, :] = jnp.zeros((CONF_HALO, CONF_CH), F32)

    ubuf_ref[CONF_HALO:CONF_HALO + CONF_TILE, :] = u_ref[...]
    span = CONF_HALO + CONF_TILE - SUBLANES
    for k in range(1, SUBLANES):
        sh_ref[k - 1, 0:span, :] = ubuf_ref[k:k + span, :]
    base = CONF_HALO - (CONF_K - 1)
    for rb in range(CONF_TILE // CONF_ROWS):
        for cb in range(CONF_CH // LANES):
            cols = slice(cb * LANES, (cb + 1) * LANES)
            acc = b_ref[:, cols]
            for j in range(CONF_K):
                k = (base + j) % SUBLANES
                r0 = base + j - k + rb * CONF_ROWS
                rows = slice(r0, r0 + CONF_ROWS)
                src = ubuf_ref[rows, cols] if k == 0 else sh_ref[k - 1, rows, cols]
                acc = acc + w_ref[j:j + 1, cols] * src
            acc_ref[rb * CONF_ROWS:(rb + 1) * CONF_ROWS, cols] = acc
    ubuf_ref[0:CONF_HALO, :] = ubuf_ref[CONF_TILE:CONF_TILE + CONF_HALO, :]
    cv_ref[...] = _ln_silu(acc_ref[...], lng_ref[...], lnb_ref[...]).astype(cv_ref.dtype)


def _conf_prompt(u, w, b, ln_g, ln_b):
    bsz, l, _ = u.shape
    seq = pl.BlockSpec((None, CONF_TILE, CONF_CH), lambda i, t: (i, t, 0))
    return pl.pallas_call(
        _conf_prompt_kernel,
        grid=(bsz, l // CONF_TILE),
        in_specs=[seq, _const_spec(w.shape), _const_spec((1, CONF_CH)), _const_spec((1, CONF_CH)),
                  _const_spec((1, CONF_CH))],
        out_specs=seq,
        out_shape=jax.ShapeDtypeStruct((bsz, l, CONF_CH), BF16),
        scratch_shapes=[pltpu.VMEM((CONF_HALO + CONF_TILE, CONF_CH), F32),
                        pltpu.VMEM((SUBLANES - 1, CONF_HALO + CONF_TILE - SUBLANES, CONF_CH), F32),
                        pltpu.VMEM((CONF_TILE, CONF_CH), F32)],
        compiler_params=_params(("parallel", "arbitrary")),
        name="conf_prompt",
    )(u, w, b, ln_g, ln_b)


def _conf_sample_kernel(u_ref, cs_ref, w_ref, b_ref, lng_ref, lnb_ref, cv_ref):
    acc = w_ref[CONF_K - 1:CONF_K, :] * u_ref[...] + b_ref[...]
    for j in range(CONF_K - 1):
        acc = acc + w_ref[j:j + 1, :] * cs_ref[j]
    cv_ref[...] = _ln_silu(acc, lng_ref[...], lnb_ref[...]).astype(cv_ref.dtype)


def _conf_sample(u, conf_state_t, w, b, ln_g, ln_b):
    n = u.shape[0]
    return pl.pallas_call(
        _conf_sample_kernel,
        out_shape=jax.ShapeDtypeStruct((n, CONF_CH), BF16),
        compiler_params=pltpu.CompilerParams(vmem_limit_bytes=VMEM_LIMIT_BYTES),
        name="conf_sample",
    )(u, conf_state_t, w, b, ln_g, ln_b)


MOE_TILE = 256
SLAB = D_MODEL // LANES


def _mix_kernel(x_ref, o_ref, cv_ref, ga_ref, gb_ref, wgo_ref, wco_ref, wout_ref, nffn_ref, wr_ref, br_ref,
                x1_ref, hm3_ref, topi_ref, topw_ref):
    tm = x_ref.shape[0]
    branch_a = _dot(o_ref[...], wgo_ref[...])
    branch_b = _dot(cv_ref[...], wco_ref[...])
    merged = ga_ref[...] * branch_a + gb_ref[...] * branch_b
    x1 = x_ref[...] + _bdot(merged, wout_ref[...])
    x1_ref[...] = x1
    hm = _rms(x1, nffn_ref[...])
    for s in range(SLAB):
        hm3_ref[pl.ds(s, tm, stride=SLAB), :] = hm[:, s * LANES:(s + 1) * LANES]
    hm_hi = hm.astype(BF16)
    hm_lo = (hm - hm_hi.astype(F32)).astype(BF16)
    by_hi = _dot(hm_hi, wr_ref[...])
    logits = (by_hi[:, :N_EXPERTS] + (by_hi[:, N_EXPERTS:] + _dot(hm_lo, wr_ref[:, :N_EXPERTS]))) + br_ref[...]
    lane = lax.broadcasted_iota(jnp.int32, logits.shape, 1)
    work = logits
    top_v = []
    top_i = jnp.zeros(logits.shape, jnp.int32)
    for k in range(TOP_K):
        m = jnp.max(work, axis=-1, keepdims=True)
        first = jnp.min(jnp.where(work == m, lane, N_EXPERTS), axis=-1, keepdims=True)
        top_v.append(m)
        top_i = jnp.where(lane == k, first, top_i)
        work = jnp.where(lane == first, -jnp.inf, work)
    ex = [jnp.exp(v - top_v[0]) for v in top_v]
    den = ex[0]
    for e in ex[1:]:
        den = den + e
    top_w = jnp.zeros(logits.shape, F32)
    for k, e in enumerate(ex):
        top_w = jnp.where(lane == k, e / den, top_w)
    topi_ref[...] = top_i
    topw_ref[...] = top_w


def _mix(x, o, cv, ga, gb, wgo, wco, wout, norm_ffn, w_router, b_router, tm):
    n = x.shape[0]
    row = lambda w: pl.BlockSpec((tm, w), lambda i: (i, 0))
    return pl.pallas_call(
        _mix_kernel,
        grid=(n // tm,),
        in_specs=[row(D_MODEL), row(V_DIM), row(CONF_CH), row(D_MODEL), row(D_MODEL),
                  _const_spec(wgo.shape), _const_spec(wco.shape), _const_spec(wout.shape),
                  _const_spec((1, D_MODEL)), _const_spec(w_router.shape), _const_spec((1, N_EXPERTS))],
        out_specs=[row(D_MODEL), pl.BlockSpec((tm * SLAB, LANES), lambda i: (i, 0)), row(N_EXPERTS), row(N_EXPERTS)],
        out_shape=[jax.ShapeDtypeStruct((n, D_MODEL), F32), jax.ShapeDtypeStruct((n * SLAB, LANES), F32),
                   jax.ShapeDtypeStruct((n, N_EXPERTS), jnp.int32), jax.ShapeDtypeStruct((n, N_EXPERTS), F32)],
        compiler_params=_params(("parallel",)),
        name="mix",
    )(x, o, cv, ga, gb, wgo, wco, wout, norm_ffn, w_router, b_router)


ROUTE_BLOCK_MAX = 16384


def _equal_blocks(n):
    for d in range(1, n // LANES + 1):
        if n % d == 0 and n // d <= ROUTE_BLOCK_MAX and (n // d) % LANES == 0:
            return d
    raise ValueError(f"cannot split {n} elements into lane-aligned blocks")


def _route_rows_kernel(lo_ref, hi_ref, pos_ref, rp_ref):
    s = pl.program_id(0)
    pairs_per = pos_ref.shape[-1]

    @pl.when(s == 0)
    def _():
        def fill(r, carry):
            rp_ref[r] = -1
            return carry
        for i in range(lo_ref.shape[0]):
            lax.fori_loop(lo_ref[i], hi_ref[i], fill, 0)

    @pl.when(s > 0)
    def _():
        base = (s - 1) * pairs_per

        def place(i, carry):
            for j in range(SUBLANES):
                p = i * SUBLANES + j
                rp_ref[pos_ref[0, p]] = base + p
            return carry
        lax.fori_loop(0, pairs_per // SUBLANES, place, 0)


def _route(top_i):
    n = top_i.shape[0]
    pairs = n * TOP_K
    n_tiles = pairs // MOE_TILE + N_EXPERTS
    e_flat = top_i.reshape(pairs)
    onehot = (e_flat[:, None] == jnp.arange(N_EXPERTS, dtype=jnp.int32)[None, :]).astype(jnp.int32)
    csum = jnp.cumsum(onehot, axis=0)
    rank = jnp.take_along_axis(csum, e_flat[:, None], axis=1)[:, 0] - 1
    tiles_per = (csum[-1] + MOE_TILE - 1) // MOE_TILE
    tile_end = jnp.cumsum(tiles_per)
    n_used = tile_end[-1]
    pos = (tile_end - tiles_per)[e_flat] * MOE_TILE + rank
    n_rows = n_tiles * MOE_TILE
    n_place = _equal_blocks(pairs)
    pad_lo = jnp.concatenate([(tile_end - tiles_per) * MOE_TILE + csum[-1], n_used.reshape(1) * MOE_TILE])
    pad_hi = jnp.concatenate([tile_end * MOE_TILE, jnp.full((1,), n_rows, jnp.int32)])
    row_pair = pl.pallas_call(
        _route_rows_kernel,
        grid_spec=pltpu.PrefetchScalarGridSpec(
            num_scalar_prefetch=2,
            grid=(1 + n_place,),
            in_specs=[pl.BlockSpec((None, 1, pairs // n_place), lambda s, lo, hi: (jnp.maximum(s - 1, 0), 0, 0),
                                   memory_space=pltpu.SMEM)],
            out_specs=pl.BlockSpec(memory_space=pltpu.SMEM)),
        out_shape=jax.ShapeDtypeStruct((n_rows,), jnp.int32),
        compiler_params=_params(("arbitrary",)),
        name="route_rows",
    )(pad_lo, pad_hi, pos.reshape(n_place, 1, pairs // n_place))
    row_pair = jnp.concatenate([jnp.full((MOE_TILE,), -1, jnp.int32), row_pair]).reshape(n_tiles + 1, 1, MOE_TILE)
    t = jnp.minimum(jnp.arange(n_tiles + 1, dtype=jnp.int32), n_used - 1)
    tile_expert = jnp.sum((tile_end[None, :] <= t[:, None]).astype(jnp.int32), axis=1)
    return tile_expert, n_used.reshape(1), row_pair


def _experts_kernel(te_ref, nu_ref, idx0_ref, idxn_ref, idxp_ref, x3_hbm, wgu_ref, bgu_ref, wd_ref, bd_ref,
                    y_hbm, x0, x1, y0, y1, wgu16, wd16, sem_in, sem_out):
    t = pl.program_id(0)
    n_used = nu_ref[0]
    xs, ys = (x0, x1), (y0, y1)
    tile_rows = MOE_TILE * SLAB
    pair_rows = y_hbm.shape[0] - 2 * tile_rows

    def start_gather(idx_ref, c):
        for i in range(MOE_TILE):
            tok = jnp.maximum(idx_ref[0, i], 0) // TOP_K
            pltpu.make_async_copy(x3_hbm.at[tok], xs[c].at[pl.ds(i * SLAB, SLAB), :], sem_in.at[c]).start()

    def start_scatter(idx_ref, c):
        for i in range(MOE_TILE):
            pair = idx_ref[0, i]
            row = jnp.where(pair >= 0, pair * SLAB, pair_rows + c * tile_rows + i * SLAB)
            pltpu.make_async_copy(ys[c].at[pl.ds(i * SLAB, SLAB), :],
                                  y_hbm.at[pl.ds(pl.multiple_of(row, SLAB), SLAB), :], sem_out.at[c]).start()

    def wait_tile(buf, sem):
        pltpu.make_async_copy(buf, buf, sem).wait()

    @pl.when(t == 0)
    def _():
        y0[...] = jnp.zeros(y0.shape, F32)
        y1[...] = jnp.zeros(y1.shape, F32)
        pltpu.make_async_copy(y0, y_hbm.at[pl.ds(pair_rows, tile_rows), :], sem_out.at[0]).start()
        start_gather(idx0_ref, 0)

    @pl.when((t < n_used) & ((t == 0) | (te_ref[t] != te_ref[jnp.maximum(t - 1, 0)])))
    def _():
        wgu16[...] = wgu_ref[...].astype(BF16)
        wd16[...] = wd_ref[...].astype(BF16)

    def step(c):
        o = 1 - c
        wait_tile(xs[c], sem_in.at[c])
        start_gather(idxn_ref, o)
        start_scatter(idxp_ref, o)
        x = jnp.concatenate([xs[c][pl.ds(s, MOE_TILE, stride=SLAB), :] for s in range(SLAB)], axis=1)
        gu = _dot(x.astype(BF16), wgu16[...]) + bgu_ref[...]
        gate = jnp.minimum(gu[:, :D_FF], SWIGLU_LIMIT)
        up = jnp.clip(gu[:, D_FF:], -SWIGLU_LIMIT, SWIGLU_LIMIT)
        hid = (up + 1.0) * (gate * _sigmoid(SWIGLU_ALPHA * gate))
        ye = _dot(hid.astype(BF16), wd16[...]) + bd_ref[...]
        wait_tile(ys[c], sem_out.at[c])
        for s in range(SLAB):
            ys[c][pl.ds(s, MOE_TILE, stride=SLAB), :] = ye[:, s * LANES:(s + 1) * LANES]

    def drain(c):
        o = 1 - c
        start_scatter(idxp_ref, o)
        wait_tile(ys[o], sem_out.at[o])
        wait_tile(ys[c], sem_out.at[c])
        wait_tile(xs[c], sem_in.at[c])

    for c in range(2):
        pl.when((t < n_used) & (t % 2 == c))(functools.partial(step, c))
        pl.when((t == n_used) & (t % 2 == c))(functools.partial(drain, c))


def _experts(hm3, tile_expert, n_used, row_pair, wgu, bgu, wd, bd):
    n_steps = row_pair.shape[0]
    n_tiles = n_steps - 1
    pairs = hm3.shape[0] // SLAB * TOP_K
    x3 = hm3.reshape(-1, SLAB, LANES)
    idx = lambda f: pl.BlockSpec((None, 1, MOE_TILE), f, memory_space=pltpu.SMEM)
    tile_buf = pltpu.VMEM((MOE_TILE * SLAB, LANES), F32)
    grid_spec = pltpu.PrefetchScalarGridSpec(
        num_scalar_prefetch=2,
        grid=(n_steps,),
        in_specs=[idx(lambda t, te, nu: (1, 0, 0)),
                  idx(lambda t, te, nu: (jnp.minimum(t + 2, n_tiles), 0, 0)),
                  idx(lambda t, te, nu: (t, 0, 0)),
                  pl.BlockSpec(memory_space=pl.ANY),
                  pl.BlockSpec((None, D_MODEL, 2 * D_FF), lambda t, te, nu: (te[t], 0, 0)),
                  pl.BlockSpec((None, 1, 2 * D_FF), lambda t, te, nu: (te[t], 0, 0)),
                  pl.BlockSpec((None, D_FF, D_MODEL), lambda t, te, nu: (te[t], 0, 0)),
                  pl.BlockSpec((None, 1, D_MODEL), lambda t, te, nu: (te[t], 0, 0))],
        out_specs=pl.BlockSpec(memory_space=pl.ANY),
        scratch_shapes=[tile_buf, tile_buf, tile_buf, tile_buf,
                        pltpu.VMEM((D_MODEL, 2 * D_FF), BF16), pltpu.VMEM((D_FF, D_MODEL), BF16),
                        pltpu.SemaphoreType.DMA((2,)), pltpu.SemaphoreType.DMA((2,))])
    return pl.pallas_call(
        _experts_kernel,
        grid_spec=grid_spec,
        out_shape=jax.ShapeDtypeStruct(((pairs + 2 * MOE_TILE) * SLAB, LANES), F32),
        compiler_params=_params(("arbitrary",)),
        name="experts",
    )(tile_expert, n_used, row_pair, row_pair, row_pair, x3, wgu, bgu, wd, bd)


def _ple_kernel(x1_ref, y_ref, topw_ref, p_ref, nple_ref, wg_ref, wp_ref, nfin_ref, out_ref, wrep_ref, acc_ref):
    tm = x1_ref.shape[0]
    topw = topw_ref[...]
    for k in range(TOP_K):
        wrep_ref[k] = jnp.broadcast_to(topw[:, k:k + 1], (tm, LANES))

    def combine(g, carry):
        for i in range(SUBLANES):
            t = g * SUBLANES + i
            acc = None
            for k in range(TOP_K):
                rows = pl.ds(pl.multiple_of((t * TOP_K + k) * SLAB, SLAB), SLAB)
                term = wrep_ref[k, pl.ds(t, 1), :] * y_ref[rows, :]
                acc = term if acc is None else acc + term
            acc_ref[pl.ds(pl.multiple_of(t * SLAB, SLAB), SLAB), :] = acc
        return carry
    lax.fori_loop(0, tm // SUBLANES, combine, 0)
    moe = [acc_ref[pl.ds(s, tm, stride=SLAB), :] for s in range(SLAB)]
    x2 = x1_ref[...] + jnp.concatenate(moe, axis=1)
    gate = _sigmoid(_bdot(_rms(x2, nple_ref[...]), wg_ref[...]))
    x3 = x2 + _bdot(p_ref[...], wp_ref[...]) * gate
    out_ref[...] = _rms(x3, nfin_ref[...])


def _ple(x1, y2d, topw, p, norm_ple, wg, wp, norm_final, tm, row_offset):
    n = x1.shape[0]
    off = row_offset // tm
    row = lambda w: pl.BlockSpec((tm, w), lambda i: (i, 0))
    return pl.pallas_call(
        _ple_kernel,
        grid=(n // tm,),
        in_specs=[row(D_MODEL), pl.BlockSpec((tm * TOP_K * SLAB, LANES), lambda i: (i + off, 0)), row(N_EXPERTS),
                  row(PLE_DIM), _const_spec((1, D_MODEL)), _const_spec(wg.shape), _const_spec(wp.shape),
                  _const_spec((1, D_MODEL))],
        out_specs=row(D_MODEL),
        out_shape=jax.ShapeDtypeStruct((n, D_MODEL), F32),
        scratch_shapes=[pltpu.VMEM((TOP_K, tm, LANES), F32), pltpu.VMEM((tm * SLAB, LANES), F32)],
        compiler_params=_params(("parallel",)),
        name="ple",
    )(x1, y2d, topw, p, norm_ple, wg, wp, norm_final)


def kernel(x_prompt, x_sample, p_prompt, p_sample, state_gdn, state_qkv_conv, state_conf_conv, norm_mix, w_in, conv_qkv_w, a_log, dt_bias, gdn_norm, w_gdn_out, conf_dw_w, conf_dw_b, conf_ln_g, conf_ln_b, w_conf_out, w_out, norm_ffn, w_router, b_router, w_gate_up, b_gate_up, w_down, b_down, norm_ple, w_ple_gate, w_ple_proj, norm_final):
    assert w_in.shape[0] == 1, "single-layer trunk"
    bsz, seq, _ = x_prompt.shape
    n_p = bsz * seq
    n_s = x_sample.shape[0] * x_sample.shape[1]
    assert x_sample.shape[1] == 1

    w_in0 = w_in[0]
    wqkv = w_in0[:, :OFF_Z].astype(BF16)
    wz = w_in0[:, OFF_Z:OFF_BETA].astype(BF16)
    wba = w_in0[:, OFF_BETA:OFF_GLU].astype(BF16)
    wglu = w_in0[:, OFF_GLU:OFF_GATE].astype(BF16)
    wgate = w_in0[:, OFF_GATE:].astype(BF16)
    zeros_h = jnp.zeros((GDN_HEADS,), F32)
    alog16 = jnp.concatenate([zeros_h, a_log[0]]).reshape(1, 2 * GDN_HEADS)
    dtb16 = jnp.concatenate([zeros_h, dt_bias[0]]).reshape(1, 2 * GDN_HEADS)
    wgo = w_gdn_out[0].astype(BF16)
    wco = w_conf_out[0].astype(BF16)
    wout = w_out[0].astype(BF16)
    wpg = w_ple_gate[0].astype(BF16)
    wpp = w_ple_proj[0].astype(BF16)
    bgu = b_gate_up[0].reshape(N_EXPERTS, 1, 2 * D_FF)
    bd = b_down[0].reshape(N_EXPERTS, 1, D_MODEL)
    row = lambda v: v.reshape(1, -1)
    inproj_w = (row(norm_mix[0]), wqkv, wz, wba, wglu, wgate, alog16, dtb16)
    wr_hi = w_router[0].astype(BF16)
    wr_lo = (w_router[0] - wr_hi.astype(F32)).astype(BF16)
    mix_w = (wgo, wco, wout, row(norm_ffn[0]), jnp.concatenate([wr_hi, wr_lo], axis=1), row(b_router[0]))
    ple_w = (row(norm_ple[0]), wpg, wpp, row(norm_final))
    conf_w = (conf_dw_w[0], row(conf_dw_b[0]), row(conf_ln_g[0]), row(conf_ln_b[0]))

    xp = x_prompt.reshape(n_p, D_MODEL)
    qkv_p, z_p, bg_p, u_p, ga_p, gb_p = _inproj(xp, *inproj_w, tm=256)
    qkv_p3 = qkv_p.reshape(bsz, seq, QKV_CH)
    u_p3 = u_p.reshape(bsz, seq, CONF_CH)
    o_p, s_p = _gdn_prompt(qkv_p3, z_p.reshape(bsz, seq, V_DIM), bg_p, conv_qkv_w[0], row(gdn_norm[0]))
    cv_p = _conf_prompt(u_p3, *conf_w)

    xs = x_sample.reshape(n_s, D_MODEL)
    qkv_s, z_s, bg_s, u_s, ga_s, gb_s = _inproj(xs, *inproj_w, tm=n_s)
    o_s, s_s = _gdn_sample(qkv_s, jnp.swapaxes(state_qkv_conv[0], 0, 1), z_s, bg_s, state_gdn[0],
                           conv_qkv_w[0], row(gdn_norm[0]))
    cv_s = _conf_sample(u_s, jnp.swapaxes(state_conf_conv[0], 0, 1), *conf_w)

    tm_p = 512
    x1_p, hm3_p, ti_p, tw_p = _mix(xp, o_p.reshape(n_p, V_DIM), cv_p.reshape(n_p, CONF_CH), ga_p, gb_p, *mix_w,
                                   tm=tm_p)
    x1_s, hm3_s, ti_s, tw_s = _mix(xs, o_s, cv_s, ga_s, gb_s, *mix_w, tm=n_s)
    routing = _route(jnp.concatenate([ti_p, ti_s], axis=0)[:, :TOP_K])
    y2d = _experts(jnp.concatenate([hm3_p, hm3_s], axis=0), *routing, w_gate_up[0], bgu, w_down[0], bd)
    y_p = _ple(x1_p, y2d, tw_p, p_prompt[0].reshape(n_p, PLE_DIM), *ple_w, tm=tm_p, row_offset=0)
    y_s = _ple(x1_s, y2d, tw_s, p_sample[0].reshape(n_s, PLE_DIM), *ple_w, tm=n_s, row_offset=n_p)

    new_qkv_s = jnp.concatenate([state_qkv_conv[0][:, 1:], qkv_s[:, None, :]], axis=1)
    new_conf_s = jnp.concatenate([state_conf_conv[0][:, 1:], u_s[:, None, :]], axis=1)
    return (y_p.reshape(bsz, seq, D_MODEL), y_s.reshape(n_s, 1, D_MODEL),
            s_p[None], qkv_p3[:, seq - (SHORT_CONV - 1):][None], u_p3[:, seq - (CONF_K - 1):][None],
            s_s[None], new_qkv_s[None], new_conf_s[None])
```
